```python
import math
import jax, jax.numpy as jnp
from jax import lax
import numpy as np

D_MODEL = 2048
BATCH = 1
SEQ = 8192
DEPTH = 1

PLE_DIM = 256
EPS = 1e-6
SSM_WIDTH = D_MODEL // 2
SSM_GROUP = 16
SSM_GROUPS = SSM_WIDTH // SSM_GROUP
SSM_STATE = 64
DT_MIN = 1e-3
DT_MAX = 1e-1
N_HEADS = 8
N_KV_HEADS = 2
HEAD_DIM = 128
ATTN_WIDTH = N_HEADS * HEAD_DIM
Q_LORA_RANK = 512
IDX_HEADS = 16
IDX_DIM = 64
TOPK_MAX = 256
Q_BLOCK = 128
IN_SIZES = (SSM_WIDTH, SSM_WIDTH, Q_LORA_RANK, N_KV_HEADS * HEAD_DIM, N_KV_HEADS * HEAD_DIM,
            ATTN_WIDTH, IDX_DIM, IDX_HEADS, D_MODEL, D_MODEL)
IN_WIDTH = sum(IN_SIZES)

kernel_name = "hybrid_s5_dsa_gated_block"


def rms_norm(x, g):
    x32 = x.astype(jnp.float32)
    y = x32 * lax.rsqrt(jnp.mean(x32 * x32, axis=-1, keepdims=True) + EPS)
    return y.astype(x.dtype) * g


def split_points():
    pts, acc = [], 0
    for s in IN_SIZES[:-1]:
        acc += s
        pts.append(acc)
    return pts


def s5_branch(u, a_re, a_im, log_dt, b_re, b_im, c_re, c_im, d_skip, w_glu):
    bsz, seq, _ = u.shape
    f32 = jnp.float32
    uf = u.astype(f32).reshape(bsz, seq, SSM_GROUPS, SSM_GROUP)
    dt = jnp.exp(log_dt.astype(f32))[:, None]
    ar, ai = a_re.astype(f32), a_im.astype(f32)
    mag = jnp.exp(dt * ar)
    abar_re = mag * jnp.cos(dt * ai)
    abar_im = mag * jnp.sin(dt * ai)
    den = ar * ar + ai * ai
    nr = abar_re - 1.0
    f_re = (nr * ar + abar_im * ai) / den
    f_im = (abar_im * ar - nr * ai) / den
    br, bi = b_re.astype(f32), b_im.astype(f32)
    bb_re = f_re[..., None] * br - f_im[..., None] * bi
    bb_im = f_re[..., None] * bi + f_im[..., None] * br
    bu_re = jnp.einsum('bsgc,gnc->bsgn', uf, bb_re)
    bu_im = jnp.einsum('bsgc,gnc->bsgn', uf, bb_im)
    at_re = jnp.broadcast_to(abar_re, bu_re.shape)
    at_im = jnp.broadcast_to(abar_im, bu_im.shape)

    def combine(e1, e2):
        a1r, a1i, x1r, x1i = e1
        a2r, a2i, x2r, x2i = e2
        return (a1r * a2r - a1i * a2i,
                a1r * a2i + a1i * a2r,
                a2r * x1r - a2i * x1i + x2r,
                a2r * x1i + a2i * x1r + x2i)

    _, _, h_re, h_im = lax.associative_scan(combine, (at_re, at_im, bu_re, bu_im), axis=1)
    y = (jnp.einsum('bsgn,gcn->bsgc', h_re, c_re.astype(f32))
         - jnp.einsum('bsgn,gcn->bsgc', h_im, c_im.astype(f32))
         + d_skip.astype(f32).reshape(SSM_GROUPS, SSM_GROUP) * uf)
    y = jax.nn.gelu(y.reshape(bsz, seq, SSM_WIDTH))
    a, b = jnp.split(y @ w_glu.astype(f32), 2, axis=-1)
    return (a * jax.nn.sigmoid(b)).astype(u.dtype)


def dsa_branch(q, k, v, q_idx, k_idx, w_idx):
    bsz, seq = q.shape[0], q.shape[1]
    n_keys = k.shape[1]
    topk = min(TOPK_MAX, n_keys // 4)
    n_blocks = seq // Q_BLOCK
    rep = N_HEADS // N_KV_HEADS
    key_pos = jnp.arange(n_keys)
    gather = jax.vmap(lambda arr, ii: arr[ii])

    def block(j):
        start = j * Q_BLOCK
        qb = lax.dynamic_slice_in_dim(q, start, Q_BLOCK, axis=1)
        qib = lax.dynamic_slice_in_dim(q_idx, start, Q_BLOCK, axis=1)
        wb = lax.dynamic_slice_in_dim(w_idx, start, Q_BLOCK, axis=1)
        t = start + jnp.arange(Q_BLOCK)
        raw = jax.nn.relu(jnp.einsum('bqhd,bsd->bqhs', qib, k_idx))
        score = jnp.einsum('bqhs,bqh->bqs', raw, wb).astype(jnp.float32)
        causal = key_pos[None, :] <= t[:, None]
        score = jnp.where(causal[None], score, -jnp.inf)
        vals, idx = lax.top_k(score, topk)
        valid = jnp.isfinite(vals)
        ks = gather(k, idx)
        vs = gather(v, idx)
        qg = qb.reshape(bsz, Q_BLOCK, N_KV_HEADS, rep, HEAD_DIM)
        logits = jnp.einsum('bqgrd,bqngd->bqgrn', qg, ks).astype(jnp.float32) * (HEAD_DIM ** -0.5)
        logits = jnp.where(valid[:, :, None, None, :], logits, -jnp.inf)
        probs = jax.nn.softmax(logits, axis=-1).astype(v.dtype)
        o = jnp.einsum('bqgrn,bqngd->bqgrd', probs, vs)
        return o.reshape(bsz, Q_BLOCK, ATTN_WIDTH)

    out = lax.map(block, jnp.arange(n_blocks))
    return out.transpose(1, 0, 2, 3).reshape(bsz, seq, ATTN_WIDTH)


def setup_inputs(seed: int = 0) -> dict:
    key = jax.random.key(seed)
    ks = jax.random.split(key, 32)
    f32 = jnp.float32

    def nrm(k, shape, scale):
        return jax.random.normal(k, shape, f32) * scale

    def gain(k, shape):
        return 1.0 + 0.02 * jax.random.normal(k, shape, f32)

    L = DEPTH
    n = jnp.arange(SSM_STATE, dtype=f32)
    a_re = -0.5 + 0.01 * jax.random.normal(ks[8], (L, SSM_GROUPS, SSM_STATE), f32)
    a_im = math.pi * n + 0.01 * jax.random.normal(ks[9], (L, SSM_GROUPS, SSM_STATE), f32)
    log_dt = jax.random.uniform(ks[10], (L, SSM_GROUPS), f32, math.log(DT_MIN), math.log(DT_MAX))
    b_scale = (2.0 * SSM_GROUP) ** -0.5
    c_scale = (2.0 * SSM_STATE) ** -0.5
    return {
        "x": jax.random.normal(ks[0], (BATCH, SEQ, D_MODEL), f32),
        "p": jax.random.normal(ks[1], (DEPTH, BATCH, SEQ, PLE_DIM), f32),
        "g_mix": gain(ks[2], (L, D_MODEL)),
        "w_in": nrm(ks[3], (L, D_MODEL, IN_WIDTH), D_MODEL ** -0.5),
        "g_q": gain(ks[4], (L, Q_LORA_RANK)),
        "w_uq": nrm(ks[5], (L, Q_LORA_RANK, ATTN_WIDTH), Q_LORA_RANK ** -0.5),
        "w_uq_idx": nrm(ks[6], (L, Q_LORA_RANK, IDX_HEADS * IDX_DIM), Q_LORA_RANK ** -0.5),
        "g_kidx": gain(ks[7], (L, IDX_DIM)),
        "a_re": a_re,
        "a_im": a_im,
        "log_dt": log_dt,
        "b_re": nrm(ks[11], (L, SSM_GROUPS, SSM_STATE, SSM_GROUP), b_scale),
        "b_im": nrm(ks[12], (L, SSM_GROUPS, SSM_STATE, SSM_GROUP), b_scale),
        "c_re": nrm(ks[13], (L, SSM_GROUPS, SSM_GROUP, SSM_STATE), c_scale),
        "c_im": nrm(ks[14], (L, SSM_GROUPS, SSM_GROUP, SSM_STATE), c_scale),
        "d_skip": nrm(ks[15], (L, SSM_WIDTH), 1.0),
        "w_glu": nrm(ks[16], (L, SSM_WIDTH, 2 * SSM_WIDTH), SSM_WIDTH ** -0.5),
        "w_ssm_out": nrm(ks[17], (L, SSM_WIDTH, D_MODEL), SSM_WIDTH ** -0.5),
        "w_attn_out": nrm(ks[18], (L, ATTN_WIDTH, D_MODEL), ATTN_WIDTH ** -0.5),
        "w_o": nrm(ks[19], (L, D_MODEL, D_MODEL), D_MODEL ** -0.5),
        "g_ple": gain(ks[20], (L, D_MODEL)),
        "w_ple_gate": nrm(ks[21], (L, D_MODEL, D_MODEL), D_MODEL ** -0.5),
        "w_ple": nrm(ks[22], (L, PLE_DIM, D_MODEL), PLE_DIM ** -0.5),
        "g_ple_post": gain(ks[23], (L, D_MODEL)),
        "g_final": gain(ks[24], (D_MODEL,)),
    }


def reference(x, p, g_mix, w_in, g_q, w_uq, w_uq_idx, g_kidx, a_re, a_im, log_dt, b_re, b_im,
              c_re, c_im, d_skip, w_glu, w_ssm_out, w_attn_out, w_o, g_ple, w_ple_gate, w_ple,
              g_ple_post, g_final):
    bsz, seq, _ = x.shape
    pts = split_points()
    for i in range(DEPTH):
        h = rms_norm(x, g_mix[i])
        proj = h @ w_in[i]
        (u, z_ssm, c_q, k, v, z_attn, k_idx, w_idx, gate_ssm, gate_attn) = jnp.split(proj, pts, axis=-1)

        y_ssm = s5_branch(u, a_re[i], a_im[i], log_dt[i], b_re[i], b_im[i], c_re[i], c_im[i],
                          d_skip[i], w_glu[i]) * jax.nn.silu(z_ssm)
        o_ssm = y_ssm @ w_ssm_out[i]

        cq = rms_norm(c_q, g_q[i])
        q = (cq @ w_uq[i]).reshape(bsz, seq, N_HEADS, HEAD_DIM)
        q_idx = (cq @ w_uq_idx[i]).reshape(bsz, seq, IDX_HEADS, IDX_DIM)
        k_idx = rms_norm(k_idx, g_kidx[i])
        w_idx = w_idx * ((IDX_HEADS ** -0.5) * (IDX_DIM ** -0.5))
        k = k.reshape(bsz, seq, N_KV_HEADS, HEAD_DIM)
        v = v.reshape(bsz, seq, N_KV_HEADS, HEAD_DIM)
        y_attn = dsa_branch(q, k, v, q_idx, k_idx, w_idx) * jax.nn.silu(z_attn)
        o_attn = y_attn @ w_attn_out[i]

        merged = jax.nn.sigmoid(gate_ssm) * o_ssm + jax.nn.sigmoid(gate_attn) * o_attn
        x = x + merged @ w_o[i]

        e = rms_norm(p[i] @ w_ple[i], g_ple_post[i])
        gate = jax.nn.sigmoid(rms_norm(x, g_ple[i]) @ w_ple_gate[i])
        x = x + gate * e
    return rms_norm(x, g_final)
```

```python
import functools
import math

import numpy as np
import jax
import jax.numpy as jnp
from jax import lax
from jax.experimental import pallas as pl
from jax.experimental.pallas import tpu as pltpu

F32 = jnp.float32
BF16 = jnp.bfloat16
I32 = jnp.int32

EPS = 1e-6
SSM_GROUP = 16
SSM_STATE = 64
N_HEADS = 8
N_KV_HEADS = 2
HEAD_DIM = 128
Q_LORA_RANK = 512
IDX_HEADS = 16
IDX_DIM = 64
TOPK_MAX = 256

V7X_LANES = 128
V7X_VMEM_LIMIT = 56 * 1024 * 1024

S5_CHUNK = 32
DSA_TQ = 256
DSA_TK = 256
SEARCH_ROWS = 64

INT_MIN = -(2 ** 31)
KEY_NEG_INF = -2139095041
NEG_INIT = -1e30

_NT = (((1,), (1,)), ((), ()))


def _cparams(*sem):
    return pltpu.CompilerParams(dimension_semantics=sem, vmem_limit_bytes=V7X_VMEM_LIMIT)


def _rms(x, g):
    ms = jnp.mean(x * x, axis=-1, keepdims=True)
    return x * lax.rsqrt(ms + EPS) * g


def _sigmoid(x):
    return 1.0 / (1.0 + jnp.exp(-x))


def _gelu_tanh(x):
    c = math.sqrt(2.0 / math.pi)
    return 0.5 * x * (1.0 + jnp.tanh(c * (x + 0.044715 * (x * x * x))))


def _proj_kernel(x_ref, g_ref, w_ref, o_ref, h_ref, *, sigmoid):
    @pl.when(pl.program_id(1) == 0)
    def _():
        h_ref[...] = _rms(x_ref[...], g_ref[...]).astype(BF16)

    acc = jnp.dot(h_ref[...], w_ref[...], preferred_element_type=F32)
    o_ref[...] = _sigmoid(acc) if sigmoid else acc


def _proj(x, g, w, *, sigmoid, tm=1024, tn=512):
    s, d = x.shape
    n = w.shape[1]
    return pl.pallas_call(
        functools.partial(_proj_kernel, sigmoid=sigmoid),
        out_shape=jax.ShapeDtypeStruct((s, n), F32),
        grid=(s // tm, n // tn),
        in_specs=[
            pl.BlockSpec((tm, d), lambda i, j: (i, 0)),
            pl.BlockSpec((1, d), lambda i, j: (0, 0)),
            pl.BlockSpec((d, tn), lambda i, j: (0, j)),
        ],
        out_specs=pl.BlockSpec((tm, tn), lambda i, j: (i, j)),
        scratch_shapes=[pltpu.VMEM((tm, d), BF16)],
        compiler_params=_cparams("parallel", "arbitrary"),
        name="proj_sig" if sigmoid else "proj",
    )(x, g, w)


def _proj_idx_kernel(x_ref, g_ref, w_ref, gk_ref, ki_ref, wi_ref, *, w_scale):
    h = _rms(x_ref[...], g_ref[...]).astype(BF16)
    r = jnp.dot(h, w_ref[...], preferred_element_type=F32)
    ki_ref[...] = _rms(r[:, :IDX_DIM], gk_ref[...]).astype(BF16)
    wi_ref[...] = r[:, IDX_DIM:IDX_DIM + IDX_HEADS] * w_scale


def _proj_idx(x, g, w, gk, *, tm=512):
    s, d = x.shape
    return pl.pallas_call(
        functools.partial(_proj_idx_kernel, w_scale=(IDX_HEADS ** -0.5) * (IDX_DIM ** -0.5)),
        out_shape=(jax.ShapeDtypeStruct((s, IDX_DIM), BF16), jax.ShapeDtypeStruct((s, IDX_HEADS), F32)),
        grid=(s // tm,),
        in_specs=[
            pl.BlockSpec((tm, d), lambda i: (i, 0)),
            pl.BlockSpec((1, d), lambda i: (0, 0)),
            pl.BlockSpec((d, V7X_LANES), lambda i: (0, 0)),
            pl.BlockSpec((1, IDX_DIM), lambda i: (0, 0)),
        ],
        out_specs=(pl.BlockSpec((tm, IDX_DIM), lambda i: (i, 0)),
                   pl.BlockSpec((tm, IDX_HEADS), lambda i: (i, 0))),
        compiler_params=_cparams("parallel"),
        name="proj_idx",
    )(x, g, w, gk)


def _qproj_kernel(c_ref, g_ref, w_ref, q_ref, qi_ref, *, q_scale):
    cq = _rms(c_ref[...], g_ref[...]).astype(BF16)
    r = jnp.dot(cq, w_ref[...], preferred_element_type=F32)
    for h in range(N_HEADS):
        q_ref[h] = (r[:, h * HEAD_DIM:(h + 1) * HEAD_DIM] * q_scale).astype(BF16)
    base = N_HEADS * HEAD_DIM
    for h in range(IDX_HEADS):
        qi_ref[h] = r[:, base + h * IDX_DIM: base + (h + 1) * IDX_DIM].astype(BF16)


def _qproj(proj_a, cq_block, g, w, *, tm=512):
    s = proj_a.shape[0]
    return pl.pallas_call(
        functools.partial(_qproj_kernel, q_scale=HEAD_DIM ** -0.5),
        out_shape=(jax.ShapeDtypeStruct((N_HEADS, s, HEAD_DIM), BF16),
                   jax.ShapeDtypeStruct((IDX_HEADS, s, IDX_DIM), BF16)),
        grid=(s // tm,),
        in_specs=[
            pl.BlockSpec((tm, Q_LORA_RANK), lambda i: (i, cq_block)),
            pl.BlockSpec((1, Q_LORA_RANK), lambda i: (0, 0)),
            pl.BlockSpec(w.shape, lambda i: (0, 0)),
        ],
        out_specs=(pl.BlockSpec((N_HEADS, tm, HEAD_DIM), lambda i: (0, i, 0)),
                   pl.BlockSpec((IDX_HEADS, tm, IDX_DIM), lambda i: (0, i, 0))),
        compiler_params=_cparams("parallel"),
        name="qproj",
    )(proj_a, g, w)


def _cmul(ar, ai, br, bi):
    return ar * br - ai * bi, ar * bi + ai * br


def _s5_operators(a_re, a_im, log_dt, b_re, b_im, c_re, c_im, levels):
    hp = lax.Precision.HIGHEST
    t_len = S5_CHUNK
    g_n, n_st = a_re.shape
    dt = jnp.exp(log_dt)[:, None]
    mag = jnp.exp(dt * a_re)
    abr = mag * jnp.cos(dt * a_im)
    abi = mag * jnp.sin(dt * a_im)
    den = a_re * a_re + a_im * a_im
    nr = abr - 1.0
    f_re = (nr * a_re + abi * a_im) / den
    f_im = (abi * a_re - nr * a_im) / den
    bb_re = f_re[..., None] * b_re - f_im[..., None] * b_im
    bb_im = f_re[..., None] * b_im + f_im[..., None] * b_re

    nbits = t_len.bit_length()
    sq = [(abr, abi)]
    for _ in range(nbits - 1):
        sq.append(_cmul(*sq[-1], *sq[-1]))
    j = np.arange(t_len + 1)
    pr = jnp.ones((t_len + 1, g_n, n_st), F32)
    pi = jnp.zeros((t_len + 1, g_n, n_st), F32)
    for b in range(nbits):
        bit = jnp.asarray(((j >> b) & 1).astype(bool))[:, None, None]
        mr, mi = _cmul(pr, pi, sq[b][0][None], sq[b][1][None])
        pr = jnp.where(bit, mr, pr)
        pi = jnp.where(bit, mi, pi)

    cp_re = c_re[None] * pr[:, :, None, :] - c_im[None] * pi[:, :, None, :]
    cp_im = c_re[None] * pi[:, :, None, :] + c_im[None] * pr[:, :, None, :]
    kern = (jnp.einsum('jgcn,gnd->jgcd', cp_re[:t_len], bb_re, precision=hp)
            - jnp.einsum('jgcn,gnd->jgcd', cp_im[:t_len], bb_im, precision=hp))
    lag = np.arange(t_len)[None, :] - np.arange(t_len)[:, None]
    ksel = kern[np.maximum(lag, 0)]
    ksel = jnp.where(jnp.asarray(lag >= 0)[:, :, None, None, None], ksel, 0.0)
    mt = ksel.transpose(2, 0, 4, 1, 3).reshape(g_n, SSM_GROUP * t_len, SSM_GROUP * t_len)

    rev = np.arange(t_len - 1, -1, -1)
    e_re = pr[rev][..., None] * bb_re[None] - pi[rev][..., None] * bb_im[None]
    e_im = pr[rev][..., None] * bb_im[None] + pi[rev][..., None] * bb_re[None]
    e_re = e_re.transpose(1, 0, 3, 2).reshape(g_n, SSM_GROUP * t_len, n_st)
    e_im = e_im.transpose(1, 0, 3, 2).reshape(g_n, SSM_GROUP * t_len, n_st)
    e = jnp.concatenate([e_re, e_im], axis=-1)

    fr = cp_re[1:].transpose(1, 3, 0, 2).reshape(g_n, n_st, SSM_GROUP * t_len)
    fi = cp_im[1:].transpose(1, 3, 0, 2).reshape(g_n, n_st, SSM_GROUP * t_len)
    f = jnp.concatenate([fr, -fi], axis=1)

    lv_r, lv_i = [pr[t_len]], [pi[t_len]]
    for _ in range(levels - 1):
        r2, i2 = _cmul(lv_r[-1], lv_i[-1], lv_r[-1], lv_i[-1])
        lv_r.append(r2)
        lv_i.append(i2)
    ar = jnp.stack(lv_r, axis=1)
    ai = jnp.stack(lv_i, axis=1)
    a1 = jnp.concatenate([ar, ar], axis=-1)
    a2 = jnp.concatenate([-ai, ai], axis=-1)
    return mt.astype(BF16), e.astype(BF16), f.astype(BF16), a1, a2


def _s5_kernel(u_ref, mt_ref, e_ref, f_ref, a1_ref, a2_ref, y_ref):
    u = u_ref[0]
    intra = jnp.dot(u, mt_ref[0], preferred_element_type=F32)
    p = jnp.dot(u, e_ref[0], preferred_element_type=F32)
    row = lax.broadcasted_iota(I32, p.shape, 0)
    n_st = p.shape[1] // 2
    for k in range(a1_ref.shape[1]):
        sh = 1 << k
        a1 = a1_ref[0, k:k + 1, :]
        a2 = a2_ref[0, k:k + 1, :]
        ps = jnp.where(row >= sh, pltpu.roll(p, sh, axis=0), 0.0)
        p = p + a1 * ps + a2 * pltpu.roll(ps, n_st, axis=1)
    carry = jnp.where(row >= 1, pltpu.roll(p, 1, axis=0), 0.0)
    inter = jnp.dot(carry.astype(BF16), f_ref[0], preferred_element_type=F32)
    y_ref[0] = intra + inter


def _s5(u_g, mt, e, f, a1, a2):
    g_n, n_chunks, ct = u_g.shape
    n2 = e.shape[-1]
    levels = a1.shape[1]
    assert n_chunks == 1 << levels
    blk = lambda *shape: pl.BlockSpec((1,) + shape, lambda g: (g, 0, 0))
    return pl.pallas_call(
        _s5_kernel,
        out_shape=jax.ShapeDtypeStruct((g_n, n_chunks, ct), F32),
        grid=(g_n,),
        in_specs=[blk(n_chunks, ct), blk(ct, ct), blk(ct, n2), blk(n2, ct),
                  blk(levels, n2), blk(levels, n2)],
        out_specs=blk(n_chunks, ct),
        compiler_params=_cparams("parallel"),
        name="s5",
    )(u_g, mt, e, f, a1, a2)


def _s5_post_kernel(y_ref, u_ref, z_ref, d_ref, w_ref, o_ref):
    width = y_ref.shape[1]
    yy = _gelu_tanh(y_ref[...] + d_ref[...] * u_ref[...])
    r = jnp.dot(yy.astype(BF16), w_ref[...], preferred_element_type=F32)
    glu = r[:, :width] * _sigmoid(r[:, width:])
    z = z_ref[...]
    o_ref[...] = (glu * (z * _sigmoid(z))).astype(BF16)


def _s5_post(y, proj_a, d_skip, w_glu, *, tm=512):
    s, width = y.shape
    return pl.pallas_call(
        _s5_post_kernel,
        out_shape=jax.ShapeDtypeStruct((s, width), BF16),
        grid=(s // tm,),
        in_specs=[
            pl.BlockSpec((tm, width), lambda i: (i, 0)),
            pl.BlockSpec((tm, width), lambda i: (i, 0)),
            pl.BlockSpec((tm, width), lambda i: (i, 1)),
            pl.BlockSpec((1, width), lambda i: (0, 0)),
            pl.BlockSpec(w_glu.shape, lambda i: (0, 0)),
        ],
        out_specs=pl.BlockSpec((tm, width), lambda i: (i, 0)),
        compiler_params=_cparams("parallel"),
        name="s5_post",
    )(y, proj_a, proj_a, d_skip, w_glu)


def _sort_key(x):
    bits = lax.bitcast_convert_type(x, I32)
    return bits ^ (lax.shift_right_arithmetic(bits, 31) & 0x7FFFFFFF)


def _dsa_kernel(q_ref, qi_ref, w_ref, z_ref, k_ref, v_ref, ki_ref, o_ref,
                keys_ref, wb_ref, tau_ref, m_ref, l_ref, acc_ref, *, topk):
    tq, tk = DSA_TQ, DSA_TK
    rep = N_HEADS // N_KV_HEADS
    i = pl.program_id(0)
    n_kt = i + 1

    for h in range(IDX_HEADS):
        wb_ref[h] = jnp.broadcast_to(w_ref[:, h:h + 1], (tq, V7X_LANES))
    qi = qi_ref[...].reshape(IDX_HEADS * tq, IDX_DIM)

    def score_tile(kt, diag):
        ki = ki_ref[pl.ds(pl.multiple_of(kt * tk, tk), tk), :]
        r = lax.dot_general(qi, ki, _NT, preferred_element_type=F32)
        for half in range(tk // V7X_LANES):
            lanes = slice(half * V7X_LANES, (half + 1) * V7X_LANES)
            acc = jnp.zeros((tq, V7X_LANES), F32)
            for h in range(IDX_HEADS):
                acc = acc + wb_ref[h] * jnp.maximum(r[h * tq:(h + 1) * tq, lanes], 0.0)
            if diag:
                col = lax.broadcasted_iota(I32, acc.shape, 1) + half * V7X_LANES
                rowi = lax.broadcasted_iota(I32, acc.shape, 0)
                acc = jnp.where(col <= rowi, acc, -jnp.inf)
            keys_ref[(tk // V7X_LANES) * kt + half] = _sort_key(acc)

    def score_body(kt, c):
        score_tile(kt, False)
        return c

    lax.fori_loop(0, i, score_body, 0)
    score_tile(i, True)

    for rc in range(tq // SEARCH_ROWS):
        rows = slice(rc * SEARCH_ROWS, (rc + 1) * SEARCH_ROWS)

        def pass_body(p, res, rows=rows):
            cand = res + lax.shift_left(jnp.int32(1), 31 - p)

            def kt_body(kt, cnt):
                a = keys_ref[2 * kt, rows, :]
                b = keys_ref[2 * kt + 1, rows, :]
                return cnt + jnp.where(a >= cand, 1, 0) + jnp.where(b >= cand, 1, 0)

            cnt = lax.fori_loop(0, n_kt, kt_body, jnp.zeros((SEARCH_ROWS, V7X_LANES), I32))
            tot = jnp.sum(cnt.astype(F32), axis=1, keepdims=True)
            return jnp.where(tot >= float(topk), cand, res)

        res = lax.fori_loop(0, 32, pass_body, jnp.full((SEARCH_ROWS, V7X_LANES), INT_MIN, I32))
        tau_ref[rows, :] = jnp.maximum(res, KEY_NEG_INF + 1)

    m_ref[...] = jnp.full(m_ref.shape, NEG_INIT, F32)
    l_ref[...] = jnp.zeros(l_ref.shape, F32)
    acc_ref[...] = jnp.zeros(acc_ref.shape, F32)

    def attn_body(kt, c):
        tau = tau_ref[...]
        neg = jnp.concatenate(
            [jnp.where(keys_ref[2 * kt + half] >= tau, 0.0, -jnp.inf) for half in range(2)], axis=1)
        ks = pl.ds(pl.multiple_of(kt * tk, tk), tk)
        for g in range(N_KV_HEADS):
            hs = slice(g * rep, (g + 1) * rep)
            cols = slice(g * HEAD_DIM, (g + 1) * HEAD_DIM)
            qg = q_ref[hs].reshape(rep * tq, HEAD_DIM)
            s = lax.dot_general(qg, k_ref[ks, cols], _NT, preferred_element_type=F32)
            s = s.reshape(rep, tq, tk) + neg[None]
            m_prev = m_ref[hs]
            m_new = jnp.maximum(m_prev[..., :1], jnp.max(s, axis=2, keepdims=True))
            alpha = jnp.exp(m_prev[..., :1] - m_new)
            p = jnp.exp(s - m_new)
            l_ref[hs] = alpha * l_ref[hs] + jnp.sum(p, axis=2, keepdims=True)
            pv = jnp.dot(p.reshape(rep * tq, tk).astype(BF16), v_ref[ks, cols],
                         preferred_element_type=F32)
            acc_ref[hs] = alpha * acc_ref[hs] + pv.reshape(rep, tq, HEAD_DIM)
            m_ref[hs] = jnp.broadcast_to(m_new, m_prev.shape)
        return c

    lax.fori_loop(0, n_kt, attn_body, 0)

    for h in range(N_HEADS):
        cols = slice(h * HEAD_DIM, (h + 1) * HEAD_DIM)
        z = z_ref[:, cols]
        o_ref[:, cols] = ((acc_ref[h] / l_ref[h]) * (z * _sigmoid(z))).astype(BF16)


def _dsa(q, qi, w_idx, proj_a, z_block, k, v, kidx):
    s = k.shape[0]
    tq = DSA_TQ
    width = N_HEADS * HEAD_DIM
    kvw = N_KV_HEADS * HEAD_DIM
    topk = min(TOPK_MAX, s // 4)
    return pl.pallas_call(
        functools.partial(_dsa_kernel, topk=topk),
        out_shape=jax.ShapeDtypeStruct((s, width), BF16),
        grid=(s // tq,),
        in_specs=[
            pl.BlockSpec((N_HEADS, tq, HEAD_DIM), lambda i: (0, i, 0)),
            pl.BlockSpec((IDX_HEADS, tq, IDX_DIM), lambda i: (0, i, 0)),
            pl.BlockSpec((tq, IDX_HEADS), lambda i: (i, 0)),
            pl.BlockSpec((tq, width), lambda i: (i, z_block)),
            pl.BlockSpec((s, kvw), lambda i: (0, 0)),
            pl.BlockSpec((s, kvw), lambda i: (0, 0)),
            pl.BlockSpec((s, IDX_DIM), lambda i: (0, 0)),
        ],
        out_specs=pl.BlockSpec((tq, width), lambda i: (i, 0)),
        scratch_shapes=[
            pltpu.VMEM((s // V7X_LANES, tq, V7X_LANES), I32),
            pltpu.VMEM((IDX_HEADS, tq, V7X_LANES), F32),
            pltpu.VMEM((tq, V7X_LANES), I32),
            pltpu.VMEM((N_HEADS, tq, V7X_LANES), F32),
            pltpu.VMEM((N_HEADS, tq, V7X_LANES), F32),
            pltpu.VMEM((N_HEADS, tq, HEAD_DIM), F32),
        ],
        compiler_params=_cparams("arbitrary"),
        name="dsa",
    )(q, qi, w_idx, proj_a, k, v, kidx)


def _merge_kernel(ys_ref, ya_ref, ws_ref, wa_ref, gs_ref, ga_ref, o_ref):
    o_s = jnp.dot(ys_ref[...], ws_ref[...], preferred_element_type=F32)
    o_a = jnp.dot(ya_ref[...], wa_ref[...], preferred_element_type=F32)
    o_ref[...] = (gs_ref[...] * o_s + ga_ref[...] * o_a).astype(BF16)


def _merge(y_ssm, y_attn, w_so, w_ao, gates, *, tm=1024, tn=512):
    s, kdim = y_ssm.shape
    d = w_so.shape[1]
    nj = d // tn
    return pl.pallas_call(
        _merge_kernel,
        out_shape=jax.ShapeDtypeStruct((s, d), BF16),
        grid=(s // tm, nj),
        in_specs=[
            pl.BlockSpec((tm, kdim), lambda i, j: (i, 0)),
            pl.BlockSpec((tm, kdim), lambda i, j: (i, 0)),
            pl.BlockSpec((kdim, tn), lambda i, j: (0, j)),
            pl.BlockSpec((kdim, tn), lambda i, j: (0, j)),
            pl.BlockSpec((tm, tn), lambda i, j: (i, j)),
            pl.BlockSpec((tm, tn), lambda i, j: (i, j + nj)),
        ],
        out_specs=pl.BlockSpec((tm, tn), lambda i, j: (i, j)),
        compiler_params=_cparams("parallel", "parallel"),
        name="merge",
    )(y_ssm, y_attn, w_so, w_ao, gates, gates)


def _oproj_kernel(m_ref, w_ref, x_ref, o_ref):
    o_ref[...] = x_ref[...] + jnp.dot(m_ref[...], w_ref[...], preferred_element_type=F32)


def _oproj(merged, w_o, x, *, tm=1024, tn=512):
    s, d = x.shape
    return pl.pallas_call(
        _oproj_kernel,
        out_shape=jax.ShapeDtypeStruct((s, d), F32),
        grid=(s // tm, d // tn),
        in_specs=[
            pl.BlockSpec((tm, d), lambda i, j: (i, 0)),
            pl.BlockSpec((d, tn), lambda i, j: (0, j)),
            pl.BlockSpec((tm, tn), lambda i, j: (i, j)),
        ],
        out_specs=pl.BlockSpec((tm, tn), lambda i, j: (i, j)),
        compiler_params=_cparams("parallel", "parallel"),
        name="oproj",
    )(merged, w_o, x)


def _final_kernel(x_ref, p_ref, wp_ref, wg_ref, gple_ref, gpost_ref, gfin_ref, o_ref):
    x1 = x_ref[...]
    e = _rms(jnp.dot(p_ref[...].astype(BF16), wp_ref[...], preferred_element_type=F32), gpost_ref[...])
    xn = _rms(x1, gple_ref[...]).astype(BF16)
    gate = _sigmoid(jnp.dot(xn, wg_ref[...], preferred_element_type=F32))
    o_ref[...] = _rms(x1 + gate * e, gfin_ref[...])


def _final(x1, p, w_ple, w_gate, g_ple, g_post, g_final, *, tm=256):
    s, d = x1.shape
    pd = p.shape[1]
    vec = pl.BlockSpec((1, d), lambda i: (0, 0))
    return pl.pallas_call(
        _final_kernel,
        out_shape=jax.ShapeDtypeStruct((s, d), F32),
        grid=(s // tm,),
        in_specs=[
            pl.BlockSpec((tm, d), lambda i: (i, 0)),
            pl.BlockSpec((tm, pd), lambda i: (i, 0)),
            pl.BlockSpec((pd, d), lambda i: (0, 0)),
            pl.BlockSpec((d, d), lambda i: (0, 0)),
            vec, vec, vec,
        ],
        out_specs=pl.BlockSpec((tm, d), lambda i: (i, 0)),
        compiler_params=_cparams("parallel"),
        name="final",
    )(x1, p, w_ple, w_gate, g_ple, g_post, g_final)


def _layer(x, p, g_mix, w_in, g_q, w_uq, w_uq_idx, g_kidx, a_re, a_im, log_dt, b_re, b_im,
           c_re, c_im, d_skip, w_glu, w_ssm_out, w_attn_out, w_o, g_ple, w_ple_gate, w_ple,
           g_ple_post):
    s, d = x.shape
    ssm_w = d // 2
    attn_w = N_HEADS * HEAD_DIM
    kvw = N_KV_HEADS * HEAD_DIM
    n_a = 2 * ssm_w + Q_LORA_RANK + 2 * kvw + attn_w
    n_i = IDX_DIM + IDX_HEADS
    w_a = w_in[:, :n_a].astype(BF16)
    w_i = jnp.pad(w_in[:, n_a:n_a + n_i], ((0, 0), (0, V7X_LANES - n_i))).astype(BF16)
    w_g = w_in[:, n_a + n_i:].astype(BF16)
    row = lambda v: v.reshape(1, -1)

    proj_a = _proj(x, row(g_mix), w_a, sigmoid=False)
    gates = _proj(x, row(g_mix), w_g, sigmoid=True)
    kidx, w_idx = _proj_idx(x, row(g_mix), w_i, row(g_kidx))

    n_chunks = s // S5_CHUNK
    n_groups = ssm_w // SSM_GROUP
    mt, e, f, a1, a2 = _s5_operators(a_re, a_im, log_dt, b_re, b_im, c_re, c_im,
                                     levels=n_chunks.bit_length() - 1)
    u_g = (proj_a[:, :ssm_w].reshape(n_chunks, S5_CHUNK, n_groups, SSM_GROUP)
           .transpose(2, 0, 1, 3).reshape(n_groups, n_chunks, S5_CHUNK * SSM_GROUP).astype(BF16))
    y_g = _s5(u_g, mt, e, f, a1, a2)
    y = (y_g.reshape(n_groups, n_chunks, S5_CHUNK, SSM_GROUP)
         .transpose(1, 2, 0, 3).reshape(s, ssm_w))
    y_ssm = _s5_post(y, proj_a, row(d_skip), w_glu.astype(BF16))

    w_q = jnp.concatenate([w_uq, w_uq_idx], axis=1).astype(BF16)
    cq_block = (2 * ssm_w) // Q_LORA_RANK
    q, qi = _qproj(proj_a, cq_block, row(g_q), w_q)
    k_off = 2 * ssm_w + Q_LORA_RANK
    k = proj_a[:, k_off:k_off + kvw].astype(BF16)
    v = proj_a[:, k_off + kvw:k_off + 2 * kvw].astype(BF16)
    z_block = (k_off + 2 * kvw) // attn_w
    y_attn = _dsa(q, qi, w_idx, proj_a, z_block, k, v, kidx)

    merged = _merge(y_ssm, y_attn, w_ssm_out.astype(BF16), w_attn_out.astype(BF16), gates)
    x1 = _oproj(merged, w_o.astype(BF16), x)
    return x1, (p, w_ple.astype(BF16), w_ple_gate.astype(BF16), row(g_ple), row(g_ple_post))


def kernel(x, p, g_mix, w_in, g_q, w_uq, w_uq_idx, g_kidx, a_re, a_im, log_dt, b_re, b_im, c_re, c_im,
           d_skip, w_glu, w_ssm_out, w_attn_out, w_o, g_ple, w_ple_gate, w_ple, g_ple_post, g_final):
    bsz, seq, d = x.shape
    depth = p.shape[0]
    assert bsz == 1 and depth == 1, "kernel is specialised to one sequence and one layer"
    x2 = x.reshape(seq, d)
    x1, (p0, wp, wg, gple, gpost) = _layer(
        x2, p[0, 0], g_mix[0], w_in[0], g_q[0], w_uq[0], w_uq_idx[0], g_kidx[0], a_re[0], a_im[0],
        log_dt[0], b_re[0], b_im[0], c_re[0], c_im[0], d_skip[0], w_glu[0], w_ssm_out[0],
        w_attn_out[0], w_o[0], g_ple[0], w_ple_gate[0], w_ple[0], g_ple_post[0])
    out = _final(x1, p0, wp, wg, gple, gpost, g_final.reshape(1, -1))
    return out.reshape(bsz, seq, d)
```

```python
import functools
import math

import numpy as np
import jax
import jax.numpy as jnp
from jax import lax
from jax.experimental import pallas as pl
from jax.experimental.pallas import tpu as pltpu

F32 = jnp.float32
BF16 = jnp.bfloat16
I32 = jnp.int32

EPS = 1e-6
SSM_GROUP = 16
SSM_STATE = 64
N_HEADS = 8
N_KV_HEADS = 2
HEAD_DIM = 128
Q_LORA_RANK = 512
IDX_HEADS = 16
IDX_DIM = 64
TOPK_MAX = 256

V7X_LANES = 128
V7X_VMEM_LIMIT = 56 * 1024 * 1024

S5_CHUNK = 32
DSA_TQ = 256
DSA_TK = 512
SEARCH_ROWS = 64

INT_MIN = -(2 ** 31)
KEY_NEG_INF = -2139095041
NEG_INIT = -1e30

_NT = (((1,), (1,)), ((), ()))


def _cparams(*sem):
    return pltpu.CompilerParams(dimension_semantics=sem, vmem_limit_bytes=V7X_VMEM_LIMIT)


def _rms(x, g):
    ms = jnp.mean(x * x, axis=-1, keepdims=True)
    return x * lax.rsqrt(ms + EPS) * g


def _sigmoid(x):
    return 1.0 / (1.0 + jnp.exp(-x))


def _gelu_tanh(x):
    c = math.sqrt(2.0 / math.pi)
    return 0.5 * x * (1.0 + jnp.tanh(c * (x + 0.044715 * (x * x * x))))


def _proj_kernel(x_ref, g_ref, w_ref, o_ref, h_ref, *, sigmoid):
    @pl.when(pl.program_id(1) == 0)
    def _():
        h_ref[...] = _rms(x_ref[...], g_ref[...]).astype(BF16)

    acc = jnp.dot(h_ref[...], w_ref[...], preferred_element_type=F32)
    o_ref[...] = _sigmoid(acc) if sigmoid else acc


def _proj(x, g, w, *, sigmoid, tm=1024, tn=512):
    s, d = x.shape
    n = w.shape[1]
    return pl.pallas_call(
        functools.partial(_proj_kernel, sigmoid=sigmoid),
        out_shape=jax.ShapeDtypeStruct((s, n), F32),
        grid=(s // tm, n // tn),
        in_specs=[
            pl.BlockSpec((tm, d), lambda i, j: (i, 0)),
            pl.BlockSpec((1, d), lambda i, j: (0, 0)),
            pl.BlockSpec((d, tn), lambda i, j: (0, j)),
        ],
        out_specs=pl.BlockSpec((tm, tn), lambda i, j: (i, j)),
        scratch_shapes=[pltpu.VMEM((tm, d), BF16)],
        compiler_params=_cparams("parallel", "arbitrary"),
        name="proj_sig" if sigmoid else "proj",
    )(x, g, w)


def _proj_idx_kernel(x_ref, g_ref, w_ref, gk_ref, ki_ref, wi_ref, *, w_scale):
    h = _rms(x_ref[...], g_ref[...]).astype(BF16)
    r = jnp.dot(h, w_ref[...], preferred_element_type=F32)
    ki_ref[...] = _rms(r[:, :IDX_DIM], gk_ref[...]).astype(BF16)
    wi_ref[...] = r[:, IDX_DIM:IDX_DIM + IDX_HEADS] * w_scale


def _proj_idx(x, g, w, gk, *, tm=512):
    s, d = x.shape
    return pl.pallas_call(
        functools.partial(_proj_idx_kernel, w_scale=(IDX_HEADS ** -0.5) * (IDX_DIM ** -0.5)),
        out_shape=(jax.ShapeDtypeStruct((s, IDX_DIM), BF16), jax.ShapeDtypeStruct((s, IDX_HEADS), F32)),
        grid=(s // tm,),
        in_specs=[
            pl.BlockSpec((tm, d), lambda i: (i, 0)),
            pl.BlockSpec((1, d), lambda i: (0, 0)),
            pl.BlockSpec((d, V7X_LANES), lambda i: (0, 0)),
            pl.BlockSpec((1, IDX_DIM), lambda i: (0, 0)),
        ],
        out_specs=(pl.BlockSpec((tm, IDX_DIM), lambda i: (i, 0)),
                   pl.BlockSpec((tm, IDX_HEADS), lambda i: (i, 0))),
        compiler_params=_cparams("parallel"),
        name="proj_idx",
    )(x, g, w, gk)


def _qproj_kernel(c_ref, g_ref, w_ref, q_ref, qi_ref, *, q_scale):
    cq = _rms(c_ref[...], g_ref[...]).astype(BF16)
    r = jnp.dot(cq, w_ref[...], preferred_element_type=F32)
    for h in range(N_HEADS):
        q_ref[h] = (r[:, h * HEAD_DIM:(h + 1) * HEAD_DIM] * q_scale).astype(BF16)
    base = N_HEADS * HEAD_DIM
    for h in range(IDX_HEADS):
        qi_ref[h] = r[:, base + h * IDX_DIM: base + (h + 1) * IDX_DIM].astype(BF16)


def _qproj(proj_a, cq_block, g, w, *, tm=512):
    s = proj_a.shape[0]
    return pl.pallas_call(
        functools.partial(_qproj_kernel, q_scale=HEAD_DIM ** -0.5 * math.log2(math.e)),
        out_shape=(jax.ShapeDtypeStruct((N_HEADS, s, HEAD_DIM), BF16),
                   jax.ShapeDtypeStruct((IDX_HEADS, s, IDX_DIM), BF16)),
        grid=(s // tm,),
        in_specs=[
            pl.BlockSpec((tm, Q_LORA_RANK), lambda i: (i, cq_block)),
            pl.BlockSpec((1, Q_LORA_RANK), lambda i: (0, 0)),
            pl.BlockSpec(w.shape, lambda i: (0, 0)),
        ],
        out_specs=(pl.BlockSpec((N_HEADS, tm, HEAD_DIM), lambda i: (0, i, 0)),
                   pl.BlockSpec((IDX_HEADS, tm, IDX_DIM), lambda i: (0, i, 0))),
        compiler_params=_cparams("parallel"),
        name="qproj",
    )(proj_a, g, w)


def _cmul(ar, ai, br, bi):
    return ar * br - ai * bi, ar * bi + ai * br


def _s5_operators(a_re, a_im, log_dt, b_re, b_im, c_re, c_im, levels):
    hp = lax.Precision.HIGHEST
    t_len = S5_CHUNK
    g_n, n_st = a_re.shape
    dt = jnp.exp(log_dt)[:, None]
    mag = jnp.exp(dt * a_re)
    abr = mag * jnp.cos(dt * a_im)
    abi = mag * jnp.sin(dt * a_im)
    den = a_re * a_re + a_im * a_im
    nr = abr - 1.0
    f_re = (nr * a_re + abi * a_im) / den
    f_im = (abi * a_re - nr * a_im) / den
    bb_re = f_re[..., None] * b_re - f_im[..., None] * b_im
    bb_im = f_re[..., None] * b_im + f_im[..., None] * b_re

    nbits = t_len.bit_length()
    sq = [(abr, abi)]
    for _ in range(nbits - 1):
        sq.append(_cmul(*sq[-1], *sq[-1]))
    j = np.arange(t_len + 1)
    pr = jnp.ones((t_len + 1, g_n, n_st), F32)
    pi = jnp.zeros((t_len + 1, g_n, n_st), F32)
    for b in range(nbits):
        bit = jnp.asarray(((j >> b) & 1).astype(bool))[:, None, None]
        mr, mi = _cmul(pr, pi, sq[b][0][None], sq[b][1][None])
        pr = jnp.where(bit, mr, pr)
        pi = jnp.where(bit, mi, pi)

    cp_re = c_re[None] * pr[:, :, None, :] - c_im[None] * pi[:, :, None, :]
    cp_im = c_re[None] * pi[:, :, None, :] + c_im[None] * pr[:, :, None, :]
    kern = (jnp.einsum('jgcn,gnd->jgcd', cp_re[:t_len], bb_re, precision=hp)
            - jnp.einsum('jgcn,gnd->jgcd', cp_im[:t_len], bb_im, precision=hp))
    lag = np.arange(t_len)[None, :] - np.arange(t_len)[:, None]
    ksel = kern[np.maximum(lag, 0)]
    ksel = jnp.where(jnp.asarray(lag >= 0)[:, :, None, None, None], ksel, 0.0)
    mt = ksel.transpose(2, 0, 4, 1, 3).reshape(g_n, SSM_GROUP * t_len, SSM_GROUP * t_len)

    rev = np.arange(t_len - 1, -1, -1)
    e_re = pr[rev][..., None] * bb_re[None] - pi[rev][..., None] * bb_im[None]
    e_im = pr[rev][..., None] * bb_im[None] + pi[rev][..., None] * bb_re[None]
    e_re = e_re.transpose(1, 0, 3, 2).reshape(g_n, SSM_GROUP * t_len, n_st)
    e_im = e_im.transpose(1, 0, 3, 2).reshape(g_n, SSM_GROUP * t_len, n_st)
    e = jnp.concatenate([e_re, e_im], axis=-1)

    fr = cp_re[1:].transpose(1, 3, 0, 2).reshape(g_n, n_st, SSM_GROUP * t_len)
    fi = cp_im[1:].transpose(1, 3, 0, 2).reshape(g_n, n_st, SSM_GROUP * t_len)
    f = jnp.concatenate([fr, -fi], axis=1)

    lv_r, lv_i = [pr[t_len]], [pi[t_len]]
    for _ in range(levels - 1):
        r2, i2 = _cmul(lv_r[-1], lv_i[-1], lv_r[-1], lv_i[-1])
        lv_r.append(r2)
        lv_i.append(i2)
    ar = jnp.stack(lv_r, axis=1)
    ai = jnp.stack(lv_i, axis=1)
    a1 = jnp.concatenate([ar, ar], axis=-1)
    a2 = jnp.concatenate([-ai, ai], axis=-1)
    return mt.astype(BF16), e.astype(BF16), f.astype(BF16), a1, a2


def _s5_kernel(u_ref, mt_ref, e_ref, f_ref, a1_ref, a2_ref, y_ref):
    u = u_ref[0]
    intra = jnp.dot(u, mt_ref[0], preferred_element_type=F32)
    p = jnp.dot(u, e_ref[0], preferred_element_type=F32)
    row = lax.broadcasted_iota(I32, p.shape, 0)
    n_st = p.shape[1] // 2
    for k in range(a1_ref.shape[1]):
        sh = 1 << k
        a1 = a1_ref[0, k:k + 1, :]
        a2 = a2_ref[0, k:k + 1, :]
        ps = jnp.where(row >= sh, pltpu.roll(p, sh, axis=0), 0.0)
        p = p + a1 * ps + a2 * pltpu.roll(ps, n_st, axis=1)
    carry = jnp.where(row >= 1, pltpu.roll(p, 1, axis=0), 0.0)
    inter = jnp.dot(carry.astype(BF16), f_ref[0], preferred_element_type=F32)
    y_ref[0] = intra + inter


def _s5(u_g, mt, e, f, a1, a2):
    g_n, n_chunks, ct = u_g.shape
    n2 = e.shape[-1]
    levels = a1.shape[1]
    assert n_chunks == 1 << levels
    blk = lambda *shape: pl.BlockSpec((1,) + shape, lambda g: (g, 0, 0))
    return pl.pallas_call(
        _s5_kernel,
        out_shape=jax.ShapeDtypeStruct((g_n, n_chunks, ct), F32),
        grid=(g_n,),
        in_specs=[blk(n_chunks, ct), blk(ct, ct), blk(ct, n2), blk(n2, ct),
                  blk(levels, n2), blk(levels, n2)],
        out_specs=blk(n_chunks, ct),
        compiler_params=_cparams("parallel"),
        name="s5",
    )(u_g, mt, e, f, a1, a2)


def _s5_post_kernel(y_ref, u_ref, z_ref, d_ref, w_ref, o_ref):
    width = y_ref.shape[1]
    yy = _gelu_tanh(y_ref[...] + d_ref[...] * u_ref[...])
    r = jnp.dot(yy.astype(BF16), w_ref[...], preferred_element_type=F32)
    glu = r[:, :width] * _sigmoid(r[:, width:])
    z = z_ref[...]
    o_ref[...] = (glu * (z * _sigmoid(z))).astype(BF16)


def _s5_post(y, proj_a, d_skip, w_glu, *, tm=512):
    s, width = y.shape
    return pl.pallas_call(
        _s5_post_kernel,
        out_shape=jax.ShapeDtypeStruct((s, width), BF16),
        grid=(s // tm,),
        in_specs=[
            pl.BlockSpec((tm, width), lambda i: (i, 0)),
            pl.BlockSpec((tm, width), lambda i: (i, 0)),
            pl.BlockSpec((tm, width), lambda i: (i, 1)),
            pl.BlockSpec((1, width), lambda i: (0, 0)),
            pl.BlockSpec(w_glu.shape, lambda i: (0, 0)),
        ],
        out_specs=pl.BlockSpec((tm, width), lambda i: (i, 0)),
        compiler_params=_cparams("parallel"),
        name="s5_post",
    )(y, proj_a, proj_a, d_skip, w_glu)


def _key_to_float(key):
    bits = key ^ (lax.shift_right_arithmetic(key, 31) & 0x7FFFFFFF)
    return lax.bitcast_convert_type(bits, F32)


def _dsa_kernel(q_ref, qi_ref, w_ref, z_ref, k_ref, v_ref, ki_ref, o_ref,
                sc_ref, wb_ref, res_ref, cand_ref, m_ref, l_ref, acc_ref, *, topk):
    tq = DSA_TQ
    halves = tq // V7X_LANES
    cpt = DSA_TK // V7X_LANES
    rep = N_HEADS // N_KV_HEADS
    i = pl.program_id(0)
    n_valid = halves * (i + 1)
    n_it = (n_valid + cpt - 1) // cpt

    for h in range(IDX_HEADS):
        wb_ref[h] = jnp.broadcast_to(w_ref[:, h:h + 1], (tq, V7X_LANES))
    qi = qi_ref[...].reshape(IDX_HEADS * tq, IDX_DIM)

    def score_tile(kt, diag):
        ki = ki_ref[pl.ds(pl.multiple_of(kt * tq, tq), tq), :]
        r = lax.dot_general(qi, ki, _NT, preferred_element_type=F32)
        for half in range(halves):
            lanes = slice(half * V7X_LANES, (half + 1) * V7X_LANES)
            acc = jnp.zeros((tq, V7X_LANES), F32)
            for h in range(IDX_HEADS):
                acc = acc + wb_ref[h] * jnp.maximum(r[h * tq:(h + 1) * tq, lanes], 0.0)
            if diag:
                col = lax.broadcasted_iota(I32, acc.shape, 1) + half * V7X_LANES
                rowi = lax.broadcasted_iota(I32, acc.shape, 0)
                acc = jnp.where(col <= rowi, acc, -jnp.inf)
            sc_ref[halves * kt + half] = acc

    def score_body(kt, c):
        score_tile(kt, False)
        return c

    lax.fori_loop(0, i, score_body, 0)
    score_tile(i, True)

    for j in range(cpt - halves):
        @pl.when(n_valid + j < n_it * cpt)
        def _():
            sc_ref[n_valid + j] = jnp.full((tq, V7X_LANES), -jnp.inf, F32)

    res_ref[...] = jnp.full(res_ref.shape, INT_MIN, I32)

    def pass_body(p, c):
        bit = lax.shift_left(jnp.int32(1), 31 - p)
        cand_ref[...] = _key_to_float(res_ref[...] + bit)
        cnts = []
        for rc in range(tq // SEARCH_ROWS):
            rows = slice(rc * SEARCH_ROWS, (rc + 1) * SEARCH_ROWS)
            cand = cand_ref[rows, :]

            def it_body(it, cnt, rows=rows, cand=cand):
                for j in range(cpt):
                    cnt = cnt + jnp.where(sc_ref[cpt * it + j, rows, :] >= cand, 1.0, 0.0)
                return cnt

            cnts.append(lax.fori_loop(0, n_it, it_body, jnp.zeros((SEARCH_ROWS, V7X_LANES), F32)))
        tot = jnp.sum(jnp.concatenate(cnts, axis=0), axis=1, keepdims=True)
        res = res_ref[...]
        res_ref[...] = jnp.where(tot >= float(topk), res + bit, res)
        return c

    lax.fori_loop(0, 32, pass_body, 0)
    cand_ref[...] = _key_to_float(jnp.maximum(res_ref[...], KEY_NEG_INF + 1))

    m_ref[...] = jnp.full(m_ref.shape, NEG_INIT, F32)
    l_ref[...] = jnp.zeros(l_ref.shape, F32)
    acc_ref[...] = jnp.zeros(acc_ref.shape, F32)
    ones = jnp.ones((DSA_TK, HEAD_DIM), BF16)

    def attn_body(it, c):
        tau = cand_ref[...]
        neg = jnp.concatenate(
            [jnp.where(sc_ref[cpt * it + j] >= tau, 0.0, -jnp.inf) for j in range(cpt)], axis=1)
        ks = pl.ds(pl.multiple_of(it * DSA_TK, DSA_TK), DSA_TK)
        for g in range(N_KV_HEADS):
            hs = slice(g * rep, (g + 1) * rep)
            cols = slice(g * HEAD_DIM, (g + 1) * HEAD_DIM)
            qg = q_ref[hs].reshape(rep * tq, HEAD_DIM)
            s = lax.dot_general(qg, k_ref[ks, cols], _NT, preferred_element_type=F32)
            s = s.reshape(rep, tq, DSA_TK) + neg[None]
            m_prev = m_ref[hs]
            m_new = jnp.maximum(m_prev, jnp.max(s, axis=2, keepdims=True))
            alpha = jnp.exp2(m_prev - m_new)
            p = jnp.concatenate(
                [jnp.exp2(s[..., j * V7X_LANES:(j + 1) * V7X_LANES] - m_new) for j in range(cpt)], axis=2)
            v_aug = jnp.concatenate([v_ref[ks, cols], ones], axis=1)
            pv = jnp.dot(p.reshape(rep * tq, DSA_TK).astype(BF16), v_aug,
                         preferred_element_type=F32).reshape(rep, tq, 2 * HEAD_DIM)
            acc_ref[hs] = alpha * acc_ref[hs] + pv[..., :HEAD_DIM]
            l_ref[hs] = alpha * l_ref[hs] + pv[..., HEAD_DIM:]
            m_ref[hs] = m_new
        return c

    lax.fori_loop(0, n_it, attn_body, 0)

    for h in range(N_HEADS):
        cols = slice(h * HEAD_DIM, (h + 1) * HEAD_DIM)
        z = z_ref[:, cols]
        o_ref[:, cols] = ((acc_ref[h] / l_ref[h]) * (z * _sigmoid(z))).astype(BF16)


def _dsa(q, qi, w_idx, proj_a, z_block, k, v, kidx):
    s = k.shape[0]
    tq = DSA_TQ
    width = N_HEADS * HEAD_DIM
    kvw = N_KV_HEADS * HEAD_DIM
    topk = min(TOPK_MAX, s // 4)
    return pl.pallas_call(
        functools.partial(_dsa_kernel, topk=topk),
        out_shape=jax.ShapeDtypeStruct((s, width), BF16),
        grid=(s // tq,),
        in_specs=[
            pl.BlockSpec((N_HEADS, tq, HEAD_DIM), lambda i: (0, i, 0)),
            pl.BlockSpec((IDX_HEADS, tq, IDX_DIM), lambda i: (0, i, 0)),
            pl.BlockSpec((tq, IDX_HEADS), lambda i: (i, 0)),
            pl.BlockSpec((tq, width), lambda i: (i, z_block)),
            pl.BlockSpec((s, kvw), lambda i: (0, 0)),
            pl.BlockSpec((s, kvw), lambda i: (0, 0)),
            pl.BlockSpec((s, IDX_DIM), lambda i: (0, 0)),
        ],
        out_specs=pl.BlockSpec((tq, width), lambda i: (i, 0)),
        scratch_shapes=[
            pltpu.VMEM((s // V7X_LANES, tq, V7X_LANES), F32),
            pltpu.VMEM((IDX_HEADS, tq, V7X_LANES), F32),
            pltpu.VMEM((tq, V7X_LANES), I32),
            pltpu.VMEM((tq, V7X_LANES), F32),
            pltpu.VMEM((N_HEADS, tq, V7X_LANES), F32),
            pltpu.VMEM((N_HEADS, tq, V7X_LANES), F32),
            pltpu.VMEM((N_HEADS, tq, HEAD_DIM), F32),
        ],
        compiler_params=_cparams("arbitrary"),
        name="dsa",
    )(q, qi, w_idx, proj_a, k, v, kidx)


def _merge_kernel(ys_ref, ya_ref, ws_ref, wa_ref, gs_ref, ga_ref, o_ref):
    o_s = jnp.dot(ys_ref[...], ws_ref[...], preferred_element_type=F32)
    o_a = jnp.dot(ya_ref[...], wa_ref[...], preferred_element_type=F32)
    o_ref[...] = (gs_ref[...] * o_s + ga_ref[...] * o_a).astype(BF16)


def _merge(y_ssm, y_attn, w_so, w_ao, gates, *, tm=1024, tn=512):
    s, kdim = y_ssm.shape
    d = w_so.shape[1]
    nj = d // tn
    return pl.pallas_call(
        _merge_kernel,
        out_shape=jax.ShapeDtypeStruct((s, d), BF16),
        grid=(s // tm, nj),
        in_specs=[
            pl.BlockSpec((tm, kdim), lambda i, j: (i, 0)),
            pl.BlockSpec((tm, kdim), lambda i, j: (i, 0)),
            pl.BlockSpec((kdim, tn), lambda i, j: (0, j)),
            pl.BlockSpec((kdim, tn), lambda i, j: (0, j)),
            pl.BlockSpec((tm, tn), lambda i, j: (i, j)),
            pl.BlockSpec((tm, tn), lambda i, j: (i, j + nj)),
        ],
        out_specs=pl.BlockSpec((tm, tn), lambda i, j: (i, j)),
        compiler_params=_cparams("parallel", "parallel"),
        name="merge",
    )(y_ssm, y_attn, w_so, w_ao, gates, gates)


def _oproj_kernel(m_ref, w_ref, x_ref, o_ref):
    o_ref[...] = x_ref[...] + jnp.dot(m_ref[...], w_ref[...], preferred_element_type=F32)


def _oproj(merged, w_o, x, *, tm=1024, tn=512):
    s, d = x.shape
    return pl.pallas_call(
        _oproj_kernel,
        out_shape=jax.ShapeDtypeStruct((s, d), F32),
        grid=(s // tm, d // tn),
        in_specs=[
            pl.BlockSpec((tm, d), lambda i, j: (i, 0)),
            pl.BlockSpec((d, tn), lambda i, j: (0, j)),
            pl.BlockSpec((tm, tn), lambda i, j: (i, j)),
        ],
        out_specs=pl.BlockSpec((tm, tn), lambda i, j: (i, j)),
        compiler_params=_cparams("parallel", "parallel"),
        name="oproj",
    )(merged, w_o, x)


def _final_kernel(x_ref, p_ref, wp_ref, wg_ref, gple_ref, gpost_ref, gfin_ref, o_ref):
    x1 = x_ref[...]
    e = _rms(jnp.dot(p_ref[...].astype(BF16), wp_ref[...], preferred_element_type=F32), gpost_ref[...])
    xn = _rms(x1, gple_ref[...]).astype(BF16)
    gate = _sigmoid(jnp.dot(xn, wg_ref[...], preferred_element_type=F32))
    o_ref[...] = _rms(x1 + gate * e, gfin_ref[...])


def _final(x1, p, w_ple, w_gate, g_ple, g_post, g_final, *, tm=256):
    s, d = x1.shape
    pd = p.shape[1]
    vec = pl.BlockSpec((1, d), lambda i: (0, 0))
    return pl.pallas_call(
        _final_kernel,
        out_shape=jax.ShapeDtypeStruct((s, d), F32),
        grid=(s // tm,),
        in_specs=[
            pl.BlockSpec((tm, d), lambda i: (i, 0)),
            pl.BlockSpec((tm, pd), lambda i: (i, 0)),
            pl.BlockSpec((pd, d), lambda i: (0, 0)),
            pl.BlockSpec((d, d), lambda i: (0, 0)),
            vec, vec, vec,
        ],
        out_specs=pl.BlockSpec((tm, d), lambda i: (i, 0)),
        compiler_params=_cparams("parallel"),
        name="final",
    )(x1, p, w_ple, w_gate, g_ple, g_post, g_final)


def _layer(x, p, g_mix, w_in, g_q, w_uq, w_uq_idx, g_kidx, a_re, a_im, log_dt, b_re, b_im,
           c_re, c_im, d_skip, w_glu, w_ssm_out, w_attn_out, w_o, g_ple, w_ple_gate, w_ple,
           g_ple_post):
    s, d = x.shape
    ssm_w = d // 2
    attn_w = N_HEADS * HEAD_DIM
    kvw = N_KV_HEADS * HEAD_DIM
    n_a = 2 * ssm_w + Q_LORA_RANK + 2 * kvw + attn_w
    n_i = IDX_DIM + IDX_HEADS
    w_a = w_in[:, :n_a].astype(BF16)
    w_i = jnp.pad(w_in[:, n_a:n_a + n_i], ((0, 0), (0, V7X_LANES - n_i))).astype(BF16)
    w_g = w_in[:, n_a + n_i:].astype(BF16)
    row = lambda v: v.reshape(1, -1)

    proj_a = _proj(x, row(g_mix), w_a, sigmoid=False)
    gates = _proj(x, row(g_mix), w_g, sigmoid=True)
    kidx, w_idx = _proj_idx(x, row(g_mix), w_i, row(g_kidx))

    n_chunks = s // S5_CHUNK
    n_groups = ssm_w // SSM_GROUP
    mt, e, f, a1, a2 = _s5_operators(a_re, a_im, log_dt, b_re, b_im, c_re, c_im,
                                     levels=n_chunks.bit_length() - 1)
    u_g = (proj_a[:, :ssm_w].reshape(n_chunks, S5_CHUNK, n_groups, SSM_GROUP)
           .transpose(2, 0, 1, 3).reshape(n_groups, n_chunks, S5_CHUNK * SSM_GROUP).astype(BF16))
    y_g = _s5(u_g, mt, e, f, a1, a2)
    y = (y_g.reshape(n_groups, n_chunks, S5_CHUNK, SSM_GROUP)
         .transpose(1, 2, 0, 3).reshape(s, ssm_w))
    y_ssm = _s5_post(y, proj_a, row(d_skip), w_glu.astype(BF16))

    w_q = jnp.concatenate([w_uq, w_uq_idx], axis=1).astype(BF16)
    cq_block = (2 * ssm_w) // Q_LORA_RANK
    q, qi = _qproj(proj_a, cq_block, row(g_q), w_q)
    k_off = 2 * ssm_w + Q_LORA_RANK
    k = proj_a[:, k_off:k_off + kvw].astype(BF16)
    v = proj_a[:, k_off + kvw:k_off + 2 * kvw].astype(BF16)
    z_block = (k_off + 2 * kvw) // attn_w
    y_attn = _dsa(q, qi, w_idx, proj_a, z_block, k, v, kidx)

    merged = _merge(y_ssm, y_attn, w_ssm_out.astype(BF16), w_attn_out.astype(BF16), gates)
    x1 = _oproj(merged, w_o.astype(BF16), x)
    return x1, (p, w_ple.astype(BF16), w_ple_gate.astype(BF16), row(g_ple), row(g_ple_post))


def kernel(x, p, g_mix, w_in, g_q, w_uq, w_uq_idx, g_kidx, a_re, a_im, log_dt, b_re, b_im, c_re, c_im,
           d_skip, w_glu, w_ssm_out, w_attn_out, w_o, g_ple, w_ple_gate, w_ple, g_ple_post, g_final):
    bsz, seq, d = x.shape
    depth = p.shape[0]
    assert bsz == 1 and depth == 1, "kernel is specialised to one sequence and one layer"
    x2 = x.reshape(seq, d)
    x1, (p0, wp, wg, gple, gpost) = _layer(
        x2, p[0, 0], g_mix[0], w_in[0], g_q[0], w_uq[0], w_uq_idx[0], g_kidx[0], a_re[0], a_im[0],
        log_dt[0], b_re[0], b_im[0], c_re[0], c_im[0], d_skip[0], w_glu[0], w_ssm_out[0],
        w_attn_out[0], w_o[0], g_ple[0], w_ple_gate[0], w_ple[0], g_ple_post[0])
    out = _final(x1, p0, wp, wg, gple, gpost, g_final.reshape(1, -1))
    return out.reshape(bsz, seq, d)
```

```python
import functools
import math

import numpy as np
import jax
import jax.numpy as jnp
from jax import lax
from jax.experimental import pallas as pl
from jax.experimental.pallas import tpu as pltpu

F32 = jnp.float32
BF16 = jnp.bfloat16
I32 = jnp.int32

EPS = 1e-6
SSM_GROUP = 16
SSM_STATE = 64
N_HEADS = 8
N_KV_HEADS = 2
HEAD_DIM = 128
Q_LORA_RANK = 512
IDX_HEADS = 16
IDX_DIM = 64
TOPK_MAX = 256

V7X_LANES = 128
V7X_VMEM_LIMIT = 56 * 1024 * 1024

S5_CHUNK = 32
DSA_TQ = 256
DSA_TK = 512
SEARCH_ROWS = 64

INT_MIN = -(2 ** 31)
KEY_NEG_INF = -2139095041
NEG_INIT = -1e30

_NT = (((1,), (1,)), ((), ()))


def _cparams(*sem):
    return pltpu.CompilerParams(dimension_semantics=sem, vmem_limit_bytes=V7X_VMEM_LIMIT)


def _rms(x, g):
    ms = jnp.mean(x * x, axis=-1, keepdims=True)
    return x * lax.rsqrt(ms + EPS) * g


def _sigmoid(x):
    return 1.0 / (1.0 + jnp.exp(-x))


def _gelu_tanh(x):
    c = math.sqrt(2.0 / math.pi)
    return 0.5 * x * (1.0 + jnp.tanh(c * (x + 0.044715 * (x * x * x))))


def _proj_kernel(x_ref, g_ref, w_ref, *refs, gates):
    o_refs, h_ref = refs[:-1], refs[-1]

    @pl.when(pl.program_id(1) == 0)
    def _():
        h_ref[...] = _rms(x_ref[...], g_ref[...]).astype(BF16)

    acc = jnp.dot(h_ref[...], w_ref[...].astype(BF16), preferred_element_type=F32)
    if gates:
        o_refs[0][...] = _sigmoid(acc)
    else:
        o_refs[0][...] = acc
        o_refs[1][...] = acc.astype(BF16)


def _proj(x, g, w, n, *, gates, tm=1024, tn=512):
    s, d = x.shape
    tile = pl.BlockSpec((tm, tn), lambda i, j: (i, j))
    if gates:
        out_shape, out_specs = jax.ShapeDtypeStruct((s, n), F32), tile
    else:
        out_shape = (jax.ShapeDtypeStruct((s, n), F32), jax.ShapeDtypeStruct((s, n), BF16))
        out_specs = (tile, tile)
    return pl.pallas_call(
        functools.partial(_proj_kernel, gates=gates),
        out_shape=out_shape,
        grid=(s // tm, n // tn),
        in_specs=[
            pl.BlockSpec((tm, d), lambda i, j: (i, 0)),
            pl.BlockSpec((1, d), lambda i, j: (0, 0)),
            pl.BlockSpec((d, tn), lambda i, j: (0, j)),
        ],
        out_specs=out_specs,
        scratch_shapes=[pltpu.VMEM((tm, d), BF16)],
        compiler_params=_cparams("parallel", "arbitrary"),
        name="proj_gates" if gates else "proj",
    )(x, g, w)


def _proj_idx_kernel(x_ref, g_ref, w_ref, gk_ref, ki_ref, wi_ref, *, w_scale):
    h = _rms(x_ref[...], g_ref[...]).astype(BF16)
    r = jnp.dot(h, w_ref[...], preferred_element_type=F32)
    ki_ref[...] = _rms(r[:, :IDX_DIM], gk_ref[...]).astype(BF16)
    wi_ref[...] = r[:, IDX_DIM:IDX_DIM + IDX_HEADS] * w_scale


def _proj_idx(x, g, w, gk, *, tm=512):
    s, d = x.shape
    return pl.pallas_call(
        functools.partial(_proj_idx_kernel, w_scale=(IDX_HEADS ** -0.5) * (IDX_DIM ** -0.5)),
        out_shape=(jax.ShapeDtypeStruct((s, IDX_DIM), BF16), jax.ShapeDtypeStruct((s, IDX_HEADS), F32)),
        grid=(s // tm,),
        in_specs=[
            pl.BlockSpec((tm, d), lambda i: (i, 0)),
            pl.BlockSpec((1, d), lambda i: (0, 0)),
            pl.BlockSpec((d, V7X_LANES), lambda i: (0, 0)),
            pl.BlockSpec((1, IDX_DIM), lambda i: (0, 0)),
        ],
        out_specs=(pl.BlockSpec((tm, IDX_DIM), lambda i: (i, 0)),
                   pl.BlockSpec((tm, IDX_HEADS), lambda i: (i, 0))),
        compiler_params=_cparams("parallel"),
        name="proj_idx",
    )(x, g, w, gk)


def _qproj_kernel(c_ref, g_ref, w_ref, q_ref, qi_ref, *, q_scale):
    cq = _rms(c_ref[...], g_ref[...]).astype(BF16)
    r = jnp.dot(cq, w_ref[...], preferred_element_type=F32)
    for h in range(N_HEADS):
        q_ref[h] = (r[:, h * HEAD_DIM:(h + 1) * HEAD_DIM] * q_scale).astype(BF16)
    base = N_HEADS * HEAD_DIM
    for h in range(IDX_HEADS):
        qi_ref[h] = r[:, base + h * IDX_DIM: base + (h + 1) * IDX_DIM].astype(BF16)


def _qproj(proj_a, cq_block, g, w, *, tm=512):
    s = proj_a.shape[0]
    return pl.pallas_call(
        functools.partial(_qproj_kernel, q_scale=HEAD_DIM ** -0.5 * math.log2(math.e)),
        out_shape=(jax.ShapeDtypeStruct((N_HEADS, s, HEAD_DIM), BF16),
                   jax.ShapeDtypeStruct((IDX_HEADS, s, IDX_DIM), BF16)),
        grid=(s // tm,),
        in_specs=[
            pl.BlockSpec((tm, Q_LORA_RANK), lambda i: (i, cq_block)),
            pl.BlockSpec((1, Q_LORA_RANK), lambda i: (0, 0)),
            pl.BlockSpec(w.shape, lambda i: (0, 0)),
        ],
        out_specs=(pl.BlockSpec((N_HEADS, tm, HEAD_DIM), lambda i: (0, i, 0)),
                   pl.BlockSpec((IDX_HEADS, tm, IDX_DIM), lambda i: (0, i, 0))),
        compiler_params=_cparams("parallel"),
        name="qproj",
    )(proj_a, g, w)


def _cmul(ar, ai, br, bi):
    return ar * br - ai * bi, ar * bi + ai * br


def _s5_operators(a_re, a_im, log_dt, b_re, b_im, c_re, c_im, levels):
    t_len = S5_CHUNK
    g_n, n_st = a_re.shape
    dt = jnp.exp(log_dt)[:, None]
    mag = jnp.exp(dt * a_re)
    abr = mag * jnp.cos(dt * a_im)
    abi = mag * jnp.sin(dt * a_im)
    den = a_re * a_re + a_im * a_im
    nr = abr - 1.0
    f_re = (nr * a_re + abi * a_im) / den
    f_im = (abi * a_re - nr * a_im) / den
    bt_re = b_re.transpose(0, 2, 1)
    bt_im = b_im.transpose(0, 2, 1)
    bbt_re = f_re[:, None, :] * bt_re - f_im[:, None, :] * bt_im
    bbt_im = f_re[:, None, :] * bt_im + f_im[:, None, :] * bt_re

    nbits = t_len.bit_length()
    sq = [(abr, abi)]
    for _ in range(nbits - 1):
        sq.append(_cmul(*sq[-1], *sq[-1]))
    j = np.arange(t_len + 1)
    pr = jnp.ones((g_n, t_len + 1, n_st), F32)
    pi = jnp.zeros((g_n, t_len + 1, n_st), F32)
    for b in range(nbits):
        bit = jnp.asarray(((j >> b) & 1).astype(bool))[None, :, None]
        mr, mi = _cmul(pr, pi, sq[b][0][:, None, :], sq[b][1][:, None, :])
        pr = jnp.where(bit, mr, pr)
        pi = jnp.where(bit, mi, pi)

    prj, pij = pr[:, :, None, :], pi[:, :, None, :]
    cp_re = c_re[:, None] * prj - c_im[:, None] * pij
    cp_im = c_re[:, None] * pij + c_im[:, None] * prj
    cp = jnp.concatenate([cp_re, -cp_im], axis=-1).reshape(g_n, (t_len + 1) * SSM_GROUP, 2 * n_st)

    prr, pir = prj[:, t_len - 1::-1], pij[:, t_len - 1::-1]
    e_re = prr * bbt_re[:, None] - pir * bbt_im[:, None]
    e_im = prr * bbt_im[:, None] + pir * bbt_re[:, None]
    e = jnp.concatenate([e_re, e_im], axis=-1).reshape(g_n, t_len * SSM_GROUP, 2 * n_st)

    lv_r, lv_i = [pr[:, t_len]], [pi[:, t_len]]
    for _ in range(levels - 1):
        r2, i2 = _cmul(lv_r[-1], lv_i[-1], lv_r[-1], lv_i[-1])
        lv_r.append(r2)
        lv_i.append(i2)
    ar = jnp.stack(lv_r, axis=1)
    ai = jnp.stack(lv_i, axis=1)
    a1 = jnp.concatenate([ar, ar], axis=-1)
    a2 = jnp.concatenate([-ai, ai], axis=-1)
    return cp, jnp.concatenate([bbt_re, bbt_im], axis=-1), e.astype(BF16), a1, a2


def _s5_kernel(u_ref, cp_ref, bbt_ref, e_ref, a1_ref, a2_ref, y_ref, mt_ref):
    ct = u_ref.shape[2]
    cp = cp_ref[0]
    kt = lax.dot_general(bbt_ref[0], cp[:ct], _NT, precision=lax.Precision.HIGHEST,
                         preferred_element_type=F32)
    lane = lax.broadcasted_iota(I32, kt.shape, 1)
    for s in range(ct // SSM_GROUP):
        sh = s * SSM_GROUP
        blk = kt if s == 0 else jnp.where(lane >= sh, pltpu.roll(kt, sh, axis=1), 0.0)
        mt_ref[sh:sh + SSM_GROUP, :] = blk.astype(BF16)

    u = u_ref[0]
    intra = jnp.dot(u, mt_ref[...], preferred_element_type=F32)
    p = jnp.dot(u, e_ref[0], preferred_element_type=F32)
    row = lax.broadcasted_iota(I32, p.shape, 0)
    n_st = p.shape[1] // 2
    for k in range(a1_ref.shape[1]):
        sh = 1 << k
        a1 = a1_ref[0, k:k + 1, :]
        a2 = a2_ref[0, k:k + 1, :]
        ps = jnp.where(row >= sh, pltpu.roll(p, sh, axis=0), 0.0)
        p = p + a1 * ps + a2 * pltpu.roll(ps, n_st, axis=1)
    carry = jnp.where(row >= 1, pltpu.roll(p, 1, axis=0), 0.0)
    ft = cp[SSM_GROUP:].astype(BF16)
    inter = lax.dot_general(carry.astype(BF16), ft, _NT, preferred_element_type=F32)
    y_ref[0] = intra + inter


def _s5(u_g, cp, bbt, e, a1, a2):
    g_n, n_chunks, ct = u_g.shape
    n2 = e.shape[-1]
    levels = a1.shape[1]
    assert n_chunks == 1 << levels
    blk = lambda *shape: pl.BlockSpec((1,) + shape, lambda g: (g, 0, 0))
    return pl.pallas_call(
        _s5_kernel,
        out_shape=jax.ShapeDtypeStruct((g_n, n_chunks, ct), F32),
        grid=(g_n,),
        in_specs=[blk(n_chunks, ct), blk(ct + SSM_GROUP, n2), blk(SSM_GROUP, n2), blk(ct, n2),
                  blk(levels, n2), blk(levels, n2)],
        out_specs=blk(n_chunks, ct),
        scratch_shapes=[pltpu.VMEM((ct, ct), BF16)],
        compiler_params=_cparams("parallel"),
        name="s5",
    )(u_g, cp, bbt, e, a1, a2)


def _s5_post_kernel(y_ref, u_ref, z_ref, d_ref, w_ref, o_ref):
    width = y_ref.shape[1]
    yy = _gelu_tanh(y_ref[...] + d_ref[...] * u_ref[...])
    r = jnp.dot(yy.astype(BF16), w_ref[...], preferred_element_type=F32)
    glu = r[:, :width] * _sigmoid(r[:, width:])
    z = z_ref[...]
    o_ref[...] = (glu * (z * _sigmoid(z))).astype(BF16)


def _s5_post(y, proj_a, d_skip, w_glu, *, tm=512):
    s, width = y.shape
    return pl.pallas_call(
        _s5_post_kernel,
        out_shape=jax.ShapeDtypeStruct((s, width), BF16),
        grid=(s // tm,),
        in_specs=[
            pl.BlockSpec((tm, width), lambda i: (i, 0)),
            pl.BlockSpec((tm, width), lambda i: (i, 0)),
            pl.BlockSpec((tm, width), lambda i: (i, 1)),
            pl.BlockSpec((1, width), lambda i: (0, 0)),
            pl.BlockSpec(w_glu.shape, lambda i: (0, 0)),
        ],
        out_specs=pl.BlockSpec((tm, width), lambda i: (i, 0)),
        compiler_params=_cparams("parallel"),
        name="s5_post",
    )(y, proj_a, proj_a, d_skip, w_glu)


def _key_to_float(key):
    bits = key ^ (lax.shift_right_arithmetic(key, 31) & 0x7FFFFFFF)
    return lax.bitcast_convert_type(bits, F32)


def _dsa_kernel(q_ref, qi_ref, w_ref, z_ref, k_ref, v_ref, ki_ref, o_ref,
                sc_ref, wb_ref, res_ref, cand_ref, cnt_ref, m_ref, l_ref, acc_ref, *, topk):
    tq = DSA_TQ
    halves = tq // V7X_LANES
    cpt = DSA_TK // V7X_LANES
    rep = N_HEADS // N_KV_HEADS
    i = pl.program_id(0)
    n_valid = halves * (i + 1)
    n_it = (n_valid + cpt - 1) // cpt

    for h in range(IDX_HEADS):
        wb_ref[h] = jnp.broadcast_to(w_ref[:, h:h + 1], (tq, V7X_LANES))
    qi = qi_ref[...].reshape(IDX_HEADS * tq, IDX_DIM)

    def score_tile(kt, diag):
        ki = ki_ref[pl.ds(pl.multiple_of(kt * tq, tq), tq), :]
        r = lax.dot_general(qi, ki, _NT, preferred_element_type=F32)
        for half in range(halves):
            lanes = slice(half * V7X_LANES, (half + 1) * V7X_LANES)
            acc = jnp.zeros((tq, V7X_LANES), F32)
            for h in range(IDX_HEADS):
                acc = acc + wb_ref[h] * jnp.maximum(r[h * tq:(h + 1) * tq, lanes], 0.0)
            if diag:
                col = lax.broadcasted_iota(I32, acc.shape, 1) + half * V7X_LANES
                rowi = lax.broadcasted_iota(I32, acc.shape, 0)
                acc = jnp.where(col <= rowi, acc, -jnp.inf)
            sc_ref[halves * kt + half] = acc

    def score_body(kt, c):
        score_tile(kt, False)
        return c

    lax.fori_loop(0, i, score_body, 0)
    score_tile(i, True)

    for j in range(cpt - halves):
        @pl.when(n_valid + j < n_it * cpt)
        def _():
            sc_ref[n_valid + j] = jnp.full((tq, V7X_LANES), -jnp.inf, F32)

    res_ref[...] = jnp.full(res_ref.shape, INT_MIN, I32)

    def pass_body(p, c):
        bit = lax.shift_left(jnp.int32(1), 31 - p)
        cand_ref[...] = _key_to_float(res_ref[...] + bit)
        cnts = []
        for rc in range(tq // SEARCH_ROWS):
            rows = slice(rc * SEARCH_ROWS, (rc + 1) * SEARCH_ROWS)
            cand = cand_ref[rows, :]

            def it_body(it, cnt, rows=rows, cand=cand):
                for j in range(cpt):
                    cnt = cnt + jnp.where(sc_ref[cpt * it + j, rows, :] >= cand, 1.0, 0.0)
                return cnt

            cnts.append(lax.fori_loop(0, n_it, it_body, jnp.zeros((SEARCH_ROWS, V7X_LANES), F32)))
        tot = jnp.sum(jnp.concatenate(cnts, axis=0), axis=1, keepdims=True)
        ok = tot >= float(topk)
        res = res_ref[...]
        res_ref[...] = jnp.where(ok, res + bit, res)
        cnt_ref[...] = jnp.where(ok, tot, cnt_ref[...])
        return c

    cnt_ref[...] = (jnp.full(cnt_ref.shape, DSA_TK, I32) * n_it).astype(F32)
    lax.fori_loop(0, 32, pass_body, 0)
    res = res_ref[...]
    tau = _key_to_float(jnp.maximum(res, KEY_NEG_INF + 1))
    cand_ref[...] = tau

    excess = jnp.where(res > KEY_NEG_INF, cnt_ref[...] - float(topk), 0.0).astype(I32)
    lane = lax.broadcasted_iota(I32, (tq, V7X_LANES), 1)
    n_chunks = n_it * cpt

    def drop_body(r, c):
        active = excess > r

        def min_body(ch, mv):
            sc = sc_ref[ch]
            return jnp.minimum(mv, jnp.where(sc >= tau, sc, jnp.inf))

        mv = lax.fori_loop(0, n_chunks, min_body, jnp.full((tq, V7X_LANES), jnp.inf, F32))
        mval = jnp.min(mv, axis=1, keepdims=True)

        def idx_body(ch, ix):
            col = (lane + ch * V7X_LANES).astype(F32)
            return jnp.maximum(ix, jnp.where(sc_ref[ch] == mval, col, -1.0))

        ix = lax.fori_loop(0, n_chunks, idx_body, jnp.full((tq, V7X_LANES), -1.0, F32))
        idx = jnp.max(ix, axis=1, keepdims=True)

        def drop_chunk(ch, c2):
            col = (lane + ch * V7X_LANES).astype(F32)
            sc = sc_ref[ch]
            sc_ref[ch] = jnp.where(active, jnp.where(col == idx, -jnp.inf, sc), sc)
            return c2

        lax.fori_loop(0, n_chunks, drop_chunk, 0)
        return c

    lax.fori_loop(0, jnp.max(excess), drop_body, 0)

    m_ref[...] = jnp.full(m_ref.shape, NEG_INIT, F32)
    l_ref[...] = jnp.zeros(l_ref.shape, F32)
    acc_ref[...] = jnp.zeros(acc_ref.shape, F32)
    ones = jnp.ones((DSA_TK, HEAD_DIM), BF16)

    def attn_body(it, c):
        tau = cand_ref[...]
        neg = jnp.concatenate(
            [jnp.where(sc_ref[cpt * it + j] >= tau, 0.0, -jnp.inf) for j in range(cpt)], axis=1)
        ks = pl.ds(pl.multiple_of(it * DSA_TK, DSA_TK), DSA_TK)
        for g in range(N_KV_HEADS):
            hs = slice(g * rep, (g + 1) * rep)
            cols = slice(g * HEAD_DIM, (g + 1) * HEAD_DIM)
            qg = q_ref[hs].reshape(rep * tq, HEAD_DIM)
            s = lax.dot_general(qg, k_ref[ks, cols], _NT, preferred_element_type=F32)
            s = s.reshape(rep, tq, DSA_TK) + neg[None]
            m_prev = m_ref[hs]
            m_new = jnp.maximum(m_prev, jnp.max(s, axis=2, keepdims=True))
            alpha = jnp.exp2(m_prev - m_new)
            p = jnp.concatenate(
                [jnp.exp2(s[..., j * V7X_LANES:(j + 1) * V7X_LANES] - m_new) for j in range(cpt)], axis=2)
            v_aug = jnp.concatenate([v_ref[ks, cols], ones], axis=1)
            pv = jnp.dot(p.reshape(rep * tq, DSA_TK).astype(BF16), v_aug,
                         preferred_element_type=F32).reshape(rep, tq, 2 * HEAD_DIM)
            acc_ref[hs] = alpha * acc_ref[hs] + pv[..., :HEAD_DIM]
            l_ref[hs] = alpha * l_ref[hs] + pv[..., HEAD_DIM:]
            m_ref[hs] = m_new
        return c

    lax.fori_loop(0, n_it, attn_body, 0)

    for h in range(N_HEADS):
        cols = slice(h * HEAD_DIM, (h + 1) * HEAD_DIM)
        z = z_ref[:, cols]
        o_ref[:, cols] = ((acc_ref[h] / l_ref[h]) * (z * _sigmoid(z))).astype(BF16)


def _dsa(q, qi, w_idx, proj_a, z_block, proj_b, k_block, kidx):
    s = kidx.shape[0]
    tq = DSA_TQ
    width = N_HEADS * HEAD_DIM
    kvw = N_KV_HEADS * HEAD_DIM
    topk = min(TOPK_MAX, s // 4)
    return pl.pallas_call(
        functools.partial(_dsa_kernel, topk=topk),
        out_shape=jax.ShapeDtypeStruct((s, width), BF16),
        grid=(s // tq,),
        in_specs=[
            pl.BlockSpec((N_HEADS, tq, HEAD_DIM), lambda i: (0, i, 0)),
            pl.BlockSpec((IDX_HEADS, tq, IDX_DIM), lambda i: (0, i, 0)),
            pl.BlockSpec((tq, IDX_HEADS), lambda i: (i, 0)),
            pl.BlockSpec((tq, width), lambda i: (i, z_block)),
            pl.BlockSpec((s, kvw), lambda i: (0, k_block)),
            pl.BlockSpec((s, kvw), lambda i: (0, k_block + 1)),
            pl.BlockSpec((s, IDX_DIM), lambda i: (0, 0)),
        ],
        out_specs=pl.BlockSpec((tq, width), lambda i: (i, 0)),
        scratch_shapes=[
            pltpu.VMEM((s // V7X_LANES, tq, V7X_LANES), F32),
            pltpu.VMEM((IDX_HEADS, tq, V7X_LANES), F32),
            pltpu.VMEM((tq, V7X_LANES), I32),
            pltpu.VMEM((tq, V7X_LANES), F32),
            pltpu.VMEM((tq, V7X_LANES), F32),
            pltpu.VMEM((N_HEADS, tq, V7X_LANES), F32),
            pltpu.VMEM((N_HEADS, tq, V7X_LANES), F32),
            pltpu.VMEM((N_HEADS, tq, HEAD_DIM), F32),
        ],
        compiler_params=_cparams("arbitrary"),
        name="dsa",
    )(q, qi, w_idx, proj_a, proj_b, proj_b, kidx)


def _merge_kernel(ys_ref, ya_ref, ws_ref, wa_ref, gs_ref, ga_ref, o_ref):
    o_s = jnp.dot(ys_ref[...], ws_ref[...], preferred_element_type=F32)
    o_a = jnp.dot(ya_ref[...], wa_ref[...], preferred_element_type=F32)
    o_ref[...] = (gs_ref[...] * o_s + ga_ref[...] * o_a).astype(BF16)


def _merge(y_ssm, y_attn, w_so, w_ao, gates, *, tm=1024, tn=512):
    s, kdim = y_ssm.shape
    d = w_so.shape[1]
    nj = d // tn
    return pl.pallas_call(
        _merge_kernel,
        out_shape=jax.ShapeDtypeStruct((s, d), BF16),
        grid=(s // tm, nj),
        in_specs=[
            pl.BlockSpec((tm, kdim), lambda i, j: (i, 0)),
            pl.BlockSpec((tm, kdim), lambda i, j: (i, 0)),
            pl.BlockSpec((kdim, tn), lambda i, j: (0, j)),
            pl.BlockSpec((kdim, tn), lambda i, j: (0, j)),
            pl.BlockSpec((tm, tn), lambda i, j: (i, j)),
            pl.BlockSpec((tm, tn), lambda i, j: (i, j + nj)),
        ],
        out_specs=pl.BlockSpec((tm, tn), lambda i, j: (i, j)),
        compiler_params=_cparams("parallel", "parallel"),
        name="merge",
    )(y_ssm, y_attn, w_so, w_ao, gates, gates)


def _oproj_kernel(m_ref, w_ref, x_ref, o_ref):
    o_ref[...] = x_ref[...] + jnp.dot(m_ref[...], w_ref[...], preferred_element_type=F32)


def _oproj(merged, w_o, x, *, tm=1024, tn=512):
    s, d = x.shape
    return pl.pallas_call(
        _oproj_kernel,
        out_shape=jax.ShapeDtypeStruct((s, d), F32),
        grid=(s // tm, d // tn),
        in_specs=[
            pl.BlockSpec((tm, d), lambda i, j: (i, 0)),
            pl.BlockSpec((d, tn), lambda i, j: (0, j)),
            pl.BlockSpec((tm, tn), lambda i, j: (i, j)),
        ],
        out_specs=pl.BlockSpec((tm, tn), lambda i, j: (i, j)),
        compiler_params=_cparams("parallel", "parallel"),
        name="oproj",
    )(merged, w_o, x)


def _final_kernel(x_ref, p_ref, wp_ref, wg_ref, gple_ref, gpost_ref, gfin_ref, o_ref):
    x1 = x_ref[...]
    e = _rms(jnp.dot(p_ref[...].astype(BF16), wp_ref[...], preferred_element_type=F32), gpost_ref[...])
    xn = _rms(x1, gple_ref[...]).astype(BF16)
    gate = _sigmoid(jnp.dot(xn, wg_ref[...], preferred_element_type=F32))
    o_ref[...] = _rms(x1 + gate * e, gfin_ref[...])


def _final(x1, p, w_ple, w_gate, g_ple, g_post, g_final, *, tm=256):
    s, d = x1.shape
    pd = p.shape[1]
    vec = pl.BlockSpec((1, d), lambda i: (0, 0))
    return pl.pallas_call(
        _final_kernel,
        out_shape=jax.ShapeDtypeStruct((s, d), F32),
        grid=(s // tm,),
        in_specs=[
            pl.BlockSpec((tm, d), lambda i: (i, 0)),
            pl.BlockSpec((tm, pd), lambda i: (i, 0)),
            pl.BlockSpec((pd, d), lambda i: (0, 0)),
            pl.BlockSpec((d, d), lambda i: (0, 0)),
            vec, vec, vec,
        ],
        out_specs=pl.BlockSpec((tm, d), lambda i: (i, 0)),
        compiler_params=_cparams("parallel"),
        name="final",
    )(x1, p, w_ple, w_gate, g_ple, g_post, g_final)


def _layer(x, p, g_mix, w_in, g_q, w_uq, w_uq_idx, g_kidx, a_re, a_im, log_dt, b_re, b_im,
           c_re, c_im, d_skip, w_glu, w_ssm_out, w_attn_out, w_o, g_ple, w_ple_gate, w_ple,
           g_ple_post):
    s, d = x.shape
    ssm_w = d // 2
    attn_w = N_HEADS * HEAD_DIM
    kvw = N_KV_HEADS * HEAD_DIM
    n_a = 2 * ssm_w + Q_LORA_RANK + 2 * kvw + attn_w
    n_i = IDX_DIM + IDX_HEADS
    w_i = jnp.pad(w_in[:, n_a:n_a + n_i], ((0, 0), (0, V7X_LANES - n_i))).astype(BF16)
    w_g = w_in[:, n_a + n_i:]
    row = lambda v: v.reshape(1, -1)

    proj_a, proj_b = _proj(x, row(g_mix), w_in, n_a, gates=False)
    gates = _proj(x, row(g_mix), w_g, w_g.shape[1], gates=True)
    kidx, w_idx = _proj_idx(x, row(g_mix), w_i, row(g_kidx))

    n_chunks = s // S5_CHUNK
    n_groups = ssm_w // SSM_GROUP
    cp, bbt, e, a1, a2 = _s5_operators(a_re, a_im, log_dt, b_re, b_im, c_re, c_im,
                                       levels=n_chunks.bit_length() - 1)
    u_g = (proj_b[:, :ssm_w].reshape(n_chunks, S5_CHUNK, n_groups, SSM_GROUP)
           .transpose(2, 0, 1, 3).reshape(n_groups, n_chunks, S5_CHUNK * SSM_GROUP))
    y_g = _s5(u_g, cp, bbt, e, a1, a2)
    y = (y_g.reshape(n_groups, n_chunks, S5_CHUNK, SSM_GROUP)
         .transpose(1, 2, 0, 3).reshape(s, ssm_w))
    y_ssm = _s5_post(y, proj_a, row(d_skip), w_glu.astype(BF16))

    w_q = jnp.concatenate([w_uq, w_uq_idx], axis=1).astype(BF16)
    cq_block = (2 * ssm_w) // Q_LORA_RANK
    q, qi = _qproj(proj_a, cq_block, row(g_q), w_q)
    k_off = 2 * ssm_w + Q_LORA_RANK
    z_block = (k_off + 2 * kvw) // attn_w
    y_attn = _dsa(q, qi, w_idx, proj_a, z_block, proj_b, k_off // kvw, kidx)

    merged = _merge(y_ssm, y_attn, w_ssm_out.astype(BF16), w_attn_out.astype(BF16), gates)
    x1 = _oproj(merged, w_o.astype(BF16), x)
    return x1, (p, w_ple.astype(BF16), w_ple_gate.astype(BF16), row(g_ple), row(g_ple_post))


def kernel(x, p, g_mix, w_in, g_q, w_uq, w_uq_idx, g_kidx, a_re, a_im, log_dt, b_re, b_im, c_re, c_im,
           d_skip, w_glu, w_ssm_out, w_attn_out, w_o, g_ple, w_ple_gate, w_ple, g_ple_post, g_final):
    bsz, seq, d = x.shape
    depth = p.shape[0]
    assert bsz == 1 and depth == 1, "kernel is specialised to one sequence and one layer"
    x2 = x.reshape(seq, d)
    x1, (p0, wp, wg, gple, gpost) = _layer(
        x2, p[0, 0], g_mix[0], w_in[0], g_q[0], w_uq[0], w_uq_idx[0], g_kidx[0], a_re[0], a_im[0],
        log_dt[0], b_re[0], b_im[0], c_re[0], c_im[0], d_skip[0], w_glu[0], w_ssm_out[0],
        w_attn_out[0], w_o[0], g_ple[0], w_ple_gate[0], w_ple[0], g_ple_post[0])
    out = _final(x1, p0, wp, wg, gple, gpost, g_final.reshape(1, -1))
    return out.reshape(bsz, seq, d)
```

```python
import functools
import math

import numpy as np
import jax
import jax.numpy as jnp
from jax import lax
from jax.experimental import pallas as pl
from jax.experimental.pallas import tpu as pltpu

F32 = jnp.float32
BF16 = jnp.bfloat16
I32 = jnp.int32

EPS = 1e-6
SSM_GROUP = 16
SSM_STATE = 64
N_HEADS = 8
N_KV_HEADS = 2
HEAD_DIM = 128
Q_LORA_RANK = 512
IDX_HEADS = 16
IDX_DIM = 64
TOPK_MAX = 256

V7X_LANES = 128
V7X_VMEM_LIMIT = 56 * 1024 * 1024

S5_CHUNK = 32
DSA_TQ = 256
DSA_TK = 512
SEARCH_ROWS = 128

INT_MIN = -(2 ** 31)
KEY_NEG_INF = -2139095041
NEG_INIT = -1e30

_NT = (((1,), (1,)), ((), ()))


def _cparams(*sem):
    return pltpu.CompilerParams(dimension_semantics=sem, vmem_limit_bytes=V7X_VMEM_LIMIT)


def _rms(x, g):
    ms = jnp.mean(x * x, axis=-1, keepdims=True)
    return x * lax.rsqrt(ms + EPS) * g


def _sigmoid(x):
    return 1.0 / (1.0 + jnp.exp(-x))


def _gelu_tanh(x):
    c = math.sqrt(2.0 / math.pi)
    return 0.5 * x * (1.0 + jnp.tanh(c * (x + 0.044715 * (x * x * x))))


def _proj_kernel(x_ref, g_ref, wa_ref, wg_ref, wi_ref, gk_ref, ob_ref, cq_ref, ki_ref, widx_ref, h_ref,
                 *, n_a_tiles, cq_tile, w_scale):
    j = pl.program_id(1)

    @pl.when(j == 0)
    def _():
        h = _rms(x_ref[...], g_ref[...]).astype(BF16)
        h_ref[...] = h
        r = jnp.dot(h, wi_ref[...], preferred_element_type=F32)
        ki_ref[...] = _rms(r[:, :IDX_DIM], gk_ref[...]).astype(BF16)
        widx_ref[...] = r[:, IDX_DIM:IDX_DIM + IDX_HEADS] * w_scale

    @pl.when(j < n_a_tiles)
    def _():
        acc = jnp.dot(h_ref[...], wa_ref[...].astype(BF16), preferred_element_type=F32)
        ob_ref[...] = acc.astype(BF16)

        @pl.when(j == cq_tile)
        def _():
            cq_ref[...] = acc

    @pl.when(j >= n_a_tiles)
    def _():
        acc = jnp.dot(h_ref[...], wg_ref[...].astype(BF16), preferred_element_type=F32)
        ob_ref[...] = _sigmoid(acc).astype(BF16)


def _proj(x, g, w_in, n_a, cq_col, w_g, w_i, gk, *, tm=1024, tn=Q_LORA_RANK):
    s, d = x.shape
    n_g = w_g.shape[1]
    n_a_tiles = n_a // tn
    return pl.pallas_call(
        functools.partial(_proj_kernel, n_a_tiles=n_a_tiles, cq_tile=cq_col // tn,
                          w_scale=(IDX_HEADS ** -0.5) * (IDX_DIM ** -0.5)),
        out_shape=(jax.ShapeDtypeStruct((s, n_a + n_g), BF16),
                   jax.ShapeDtypeStruct((s, tn), F32),
                   jax.ShapeDtypeStruct((s, IDX_DIM), BF16),
                   jax.ShapeDtypeStruct((s, IDX_HEADS), F32)),
        grid=(s // tm, (n_a + n_g) // tn),
        in_specs=[
            pl.BlockSpec((tm, d), lambda i, j: (i, 0)),
            pl.BlockSpec((1, d), lambda i, j: (0, 0)),
            pl.BlockSpec((d, tn), lambda i, j: (0, jnp.minimum(j, n_a_tiles - 1))),
            pl.BlockSpec((d, tn), lambda i, j: (0, jnp.maximum(j - n_a_tiles, 0))),
            pl.BlockSpec((d, V7X_LANES), lambda i, j: (0, 0)),
            pl.BlockSpec((1, IDX_DIM), lambda i, j: (0, 0)),
        ],
        out_specs=(pl.BlockSpec((tm, tn), lambda i, j: (i, j)),
                   pl.BlockSpec((tm, tn), lambda i, j: (i, 0)),
                   pl.BlockSpec((tm, IDX_DIM), lambda i, j: (i, 0)),
                   pl.BlockSpec((tm, IDX_HEADS), lambda i, j: (i, 0))),
        scratch_shapes=[pltpu.VMEM((tm, d), BF16)],
        compiler_params=_cparams("parallel", "arbitrary"),
        name="proj",
    )(x, g, w_in, w_g, w_i, gk)


def _qproj_kernel(c_ref, g_ref, w_ref, q_ref, qi_ref, *, q_scale):
    cq = _rms(c_ref[...], g_ref[...]).astype(BF16)
    r = jnp.dot(cq, w_ref[...], preferred_element_type=F32)
    for h in range(N_HEADS):
        q_ref[h] = (r[:, h * HEAD_DIM:(h + 1) * HEAD_DIM] * q_scale).astype(BF16)
    base = N_HEADS * HEAD_DIM
    for h in range(IDX_HEADS):
        qi_ref[h] = r[:, base + h * IDX_DIM: base + (h + 1) * IDX_DIM].astype(BF16)


def _qproj(proj_a, cq_block, g, w, *, tm=512):
    s = proj_a.shape[0]
    return pl.pallas_call(
        functools.partial(_qproj_kernel, q_scale=HEAD_DIM ** -0.5 * math.log2(math.e)),
        out_shape=(jax.ShapeDtypeStruct((N_HEADS, s, HEAD_DIM), BF16),
                   jax.ShapeDtypeStruct((IDX_HEADS, s, IDX_DIM), BF16)),
        grid=(s // tm,),
        in_specs=[
            pl.BlockSpec((tm, Q_LORA_RANK), lambda i: (i, cq_block)),
            pl.BlockSpec((1, Q_LORA_RANK), lambda i: (0, 0)),
            pl.BlockSpec(w.shape, lambda i: (0, 0)),
        ],
        out_specs=(pl.BlockSpec((N_HEADS, tm, HEAD_DIM), lambda i: (0, i, 0)),
                   pl.BlockSpec((IDX_HEADS, tm, IDX_DIM), lambda i: (0, i, 0))),
        compiler_params=_cparams("parallel"),
        name="qproj",
    )(proj_a, g, w)


def _cmul(ar, ai, br, bi):
    return ar * br - ai * bi, ar * bi + ai * br


def _s5_operators(a_re, a_im, log_dt, b_re, b_im, c_re, c_im, levels):
    t_len = S5_CHUNK
    g_n, n_st = a_re.shape
    dt = jnp.exp(log_dt)[:, None]
    mag = jnp.exp(dt * a_re)
    abr = mag * jnp.cos(dt * a_im)
    abi = mag * jnp.sin(dt * a_im)
    den = a_re * a_re + a_im * a_im
    nr = abr - 1.0
    f_re = (nr * a_re + abi * a_im) / den
    f_im = (abi * a_re - nr * a_im) / den
    bt_re = b_re.transpose(0, 2, 1)
    bt_im = b_im.transpose(0, 2, 1)
    bbt_re = f_re[:, None, :] * bt_re - f_im[:, None, :] * bt_im
    bbt_im = f_re[:, None, :] * bt_im + f_im[:, None, :] * bt_re

    nbits = t_len.bit_length()
    sq = [(abr, abi)]
    for _ in range(nbits - 1):
        sq.append(_cmul(*sq[-1], *sq[-1]))
    j = np.arange(t_len + 1)
    pr = jnp.ones((g_n, t_len + 1, n_st), F32)
    pi = jnp.zeros((g_n, t_len + 1, n_st), F32)
    for b in range(nbits):
        bit = jnp.asarray(((j >> b) & 1).astype(bool))[None, :, None]
        mr, mi = _cmul(pr, pi, sq[b][0][:, None, :], sq[b][1][:, None, :])
        pr = jnp.where(bit, mr, pr)
        pi = jnp.where(bit, mi, pi)

    prj, pij = pr[:, :, None, :], pi[:, :, None, :]
    cp_re = c_re[:, None] * prj - c_im[:, None] * pij
    cp_im = c_re[:, None] * pij + c_im[:, None] * prj
    cp = jnp.concatenate([cp_re, -cp_im], axis=-1).reshape(g_n, (t_len + 1) * SSM_GROUP, 2 * n_st)

    prr, pir = prj[:, t_len - 1::-1], pij[:, t_len - 1::-1]
    e_re = prr * bbt_re[:, None] - pir * bbt_im[:, None]
    e_im = prr * bbt_im[:, None] + pir * bbt_re[:, None]
    e = jnp.concatenate([e_re, e_im], axis=-1).reshape(g_n, t_len * SSM_GROUP, 2 * n_st)

    lv_r, lv_i = [pr[:, t_len]], [pi[:, t_len]]
    for _ in range(levels - 1):
        r2, i2 = _cmul(lv_r[-1], lv_i[-1], lv_r[-1], lv_i[-1])
        lv_r.append(r2)
        lv_i.append(i2)
    ar = jnp.stack(lv_r, axis=1)
    ai = jnp.stack(lv_i, axis=1)
    a1 = jnp.concatenate([ar, ar], axis=-1)
    a2 = jnp.concatenate([-ai, ai], axis=-1)
    return cp, jnp.concatenate([bbt_re, bbt_im], axis=-1), e.astype(BF16), a1, a2


def _s5_kernel(u_ref, cp_ref, bbt_ref, e_ref, a1_ref, a2_ref, y_ref, mt_ref):
    ct = u_ref.shape[2]
    cp = cp_ref[0]
    kt = lax.dot_general(bbt_ref[0], cp[:ct], _NT, precision=lax.Precision.HIGHEST,
                         preferred_element_type=F32)
    lane = lax.broadcasted_iota(I32, kt.shape, 1)
    for s in range(ct // SSM_GROUP):
        sh = s * SSM_GROUP
        blk = kt if s == 0 else jnp.where(lane >= sh, pltpu.roll(kt, sh, axis=1), 0.0)
        mt_ref[sh:sh + SSM_GROUP, :] = blk.astype(BF16)

    u = u_ref[0]
    intra = jnp.dot(u, mt_ref[...], preferred_element_type=F32)
    p = jnp.dot(u, e_ref[0], preferred_element_type=F32)
    row = lax.broadcasted_iota(I32, p.shape, 0)
    n_st = p.shape[1] // 2
    for k in range(a1_ref.shape[1]):
        sh = 1 << k
        a1 = a1_ref[0, k:k + 1, :]
        a2 = a2_ref[0, k:k + 1, :]
        ps = jnp.where(row >= sh, pltpu.roll(p, sh, axis=0), 0.0)
        p = p + a1 * ps + a2 * pltpu.roll(ps, n_st, axis=1)
    carry = jnp.where(row >= 1, pltpu.roll(p, 1, axis=0), 0.0)
    ft = cp[SSM_GROUP:].astype(BF16)
    inter = lax.dot_general(carry.astype(BF16), ft, _NT, preferred_element_type=F32)
    y_ref[0] = intra + inter


def _s5(u_g, cp, bbt, e, a1, a2):
    g_n, n_chunks, ct = u_g.shape
    n2 = e.shape[-1]
    levels = a1.shape[1]
    assert n_chunks == 1 << levels
    blk = lambda *shape: pl.BlockSpec((1,) + shape, lambda g: (g, 0, 0))
    return pl.pallas_call(
        _s5_kernel,
        out_shape=jax.ShapeDtypeStruct((g_n, n_chunks, ct), F32),
        grid=(g_n,),
        in_specs=[blk(n_chunks, ct), blk(ct + SSM_GROUP, n2), blk(SSM_GROUP, n2), blk(ct, n2),
                  blk(levels, n2), blk(levels, n2)],
        out_specs=blk(n_chunks, ct),
        scratch_shapes=[pltpu.VMEM((ct, ct), BF16)],
        compiler_params=_cparams("parallel"),
        name="s5",
    )(u_g, cp, bbt, e, a1, a2)


def _s5_post_kernel(y_ref, u_ref, z_ref, d_ref, w_ref, o_ref):
    width = y_ref.shape[1]
    yy = _gelu_tanh(y_ref[...] + d_ref[...] * u_ref[...].astype(F32))
    r = jnp.dot(yy.astype(BF16), w_ref[...], preferred_element_type=F32)
    glu = r[:, :width] * _sigmoid(r[:, width:])
    z = z_ref[...].astype(F32)
    o_ref[...] = (glu * (z * _sigmoid(z))).astype(BF16)


def _s5_post(y, proj_a, d_skip, w_glu, *, tm=512):
    s, width = y.shape
    return pl.pallas_call(
        _s5_post_kernel,
        out_shape=jax.ShapeDtypeStruct((s, width), BF16),
        grid=(s // tm,),
        in_specs=[
            pl.BlockSpec((tm, width), lambda i: (i, 0)),
            pl.BlockSpec((tm, width), lambda i: (i, 0)),
            pl.BlockSpec((tm, width), lambda i: (i, 1)),
            pl.BlockSpec((1, width), lambda i: (0, 0)),
            pl.BlockSpec(w_glu.shape, lambda i: (0, 0)),
        ],
        out_specs=pl.BlockSpec((tm, width), lambda i: (i, 0)),
        compiler_params=_cparams("parallel"),
        name="s5_post",
    )(y, proj_a, proj_a, d_skip, w_glu)


def _key_to_float(key):
    bits = key ^ (lax.shift_right_arithmetic(key, 31) & 0x7FFFFFFF)
    return lax.bitcast_convert_type(bits, F32)


def _dsa_kernel(q_ref, qi_ref, w_ref, z_ref, k_ref, v_ref, ki_ref, o_ref,
                sc_ref, wb_ref, res_ref, cand_ref, cnt_ref, m_ref, l_ref, acc_ref, *, topk):
    tq = DSA_TQ
    halves = tq // V7X_LANES
    cpt = DSA_TK // V7X_LANES
    rep = N_HEADS // N_KV_HEADS
    i = pl.program_id(0)
    n_valid = halves * (i + 1)
    n_it = (n_valid + cpt - 1) // cpt

    for h in range(IDX_HEADS):
        wb_ref[h] = jnp.broadcast_to(w_ref[:, h:h + 1], (tq, V7X_LANES))
    qi = qi_ref[...].reshape(IDX_HEADS * tq, IDX_DIM)

    def score_tile(kt, diag):
        ki = ki_ref[pl.ds(pl.multiple_of(kt * tq, tq), tq), :]
        r = lax.dot_general(qi, ki, _NT, preferred_element_type=F32)
        for half in range(halves):
            lanes = slice(half * V7X_LANES, (half + 1) * V7X_LANES)
            acc = jnp.zeros((tq, V7X_LANES), F32)
            for h in range(IDX_HEADS):
                acc = acc + wb_ref[h] * jnp.maximum(r[h * tq:(h + 1) * tq, lanes], 0.0)
            if diag:
                col = lax.broadcasted_iota(I32, acc.shape, 1) + half * V7X_LANES
                rowi = lax.broadcasted_iota(I32, acc.shape, 0)
                acc = jnp.where(col <= rowi, acc, -jnp.inf)
            sc_ref[halves * kt + half] = acc

    def score_body(kt, c):
        score_tile(kt, False)
        return c

    lax.fori_loop(0, i, score_body, 0)
    score_tile(i, True)

    for j in range(cpt - halves):
        @pl.when(n_valid + j < n_it * cpt)
        def _():
            sc_ref[n_valid + j] = jnp.full((tq, V7X_LANES), -jnp.inf, F32)

    res_ref[...] = jnp.full(res_ref.shape, INT_MIN, I32)

    def pass_body(p, c):
        bit = lax.shift_left(jnp.int32(1), 31 - p)
        cand_ref[...] = _key_to_float(res_ref[...] + bit)
        cnts = []
        for rc in range(tq // SEARCH_ROWS):
            rows = slice(rc * SEARCH_ROWS, (rc + 1) * SEARCH_ROWS)
            cand = cand_ref[rows, :]

            def it_body(it, cnt, rows=rows, cand=cand):
                for j in range(cpt):
                    cnt = cnt + jnp.where(sc_ref[cpt * it + j, rows, :] >= cand, 1.0, 0.0)
                return cnt

            cnts.append(lax.fori_loop(0, n_it, it_body, jnp.zeros((SEARCH_ROWS, V7X_LANES), F32)))
        tot = jnp.sum(jnp.concatenate(cnts, axis=0), axis=1, keepdims=True)
        ok = tot >= float(topk)
        res = res_ref[...]
        res_ref[...] = jnp.where(ok, res + bit, res)
        cnt_ref[...] = jnp.where(ok, tot, cnt_ref[...])
        return c

    cnt_ref[...] = (jnp.full(cnt_ref.shape, DSA_TK, I32) * n_it).astype(F32)
    lax.fori_loop(0, 32, pass_body, 0)
    res = res_ref[...]
    tau = _key_to_float(jnp.maximum(res, KEY_NEG_INF + 1))
    cand_ref[...] = tau

    excess = jnp.where(res > KEY_NEG_INF, cnt_ref[...] - float(topk), 0.0).astype(I32)
    lane = lax.broadcasted_iota(I32, (tq, V7X_LANES), 1)
    n_chunks = n_it * cpt

    def drop_body(r, c):
        active = excess > r

        def min_body(ch, mv):
            sc = sc_ref[ch]
            return jnp.minimum(mv, jnp.where(sc >= tau, sc, jnp.inf))

        mv = lax.fori_loop(0, n_chunks, min_body, jnp.full((tq, V7X_LANES), jnp.inf, F32))
        mval = jnp.min(mv, axis=1, keepdims=True)

        def idx_body(ch, ix):
            col = (lane + ch * V7X_LANES).astype(F32)
            return jnp.maximum(ix, jnp.where(sc_ref[ch] == mval, col, -1.0))

        ix = lax.fori_loop(0, n_chunks, idx_body, jnp.full((tq, V7X_LANES), -1.0, F32))
        idx = jnp.max(ix, axis=1, keepdims=True)

        def drop_chunk(ch, c2):
            col = (lane + ch * V7X_LANES).astype(F32)
            sc = sc_ref[ch]
            sc_ref[ch] = jnp.where(active, jnp.where(col == idx, -jnp.inf, sc), sc)
            return c2

        lax.fori_loop(0, n_chunks, drop_chunk, 0)
        return c

    lax.fori_loop(0, jnp.max(excess), drop_body, 0)

    m_ref[...] = jnp.full(m_ref.shape, NEG_INIT, F32)
    l_ref[...] = jnp.zeros(l_ref.shape, F32)
    acc_ref[...] = jnp.zeros(acc_ref.shape, F32)
    ones = jnp.ones((DSA_TK, HEAD_DIM), BF16)

    def attn_body(it, c):
        tau = cand_ref[...]
        neg = jnp.concatenate(
            [jnp.where(sc_ref[cpt * it + j] >= tau, 0.0, -jnp.inf) for j in range(cpt)], axis=1)
        ks = pl.ds(pl.multiple_of(it * DSA_TK, DSA_TK), DSA_TK)
        for g in range(N_KV_HEADS):
            hs = slice(g * rep, (g + 1) * rep)
            cols = slice(g * HEAD_DIM, (g + 1) * HEAD_DIM)
            qg = q_ref[hs].reshape(rep * tq, HEAD_DIM)
            s = lax.dot_general(qg, k_ref[ks, cols], _NT, preferred_element_type=F32)
            s = s.reshape(rep, tq, DSA_TK) + neg[None]
            m_prev = m_ref[hs]
            m_new = jnp.maximum(m_prev, jnp.max(s, axis=2, keepdims=True))
            alpha = jnp.exp2(m_prev - m_new)
            p = jnp.concatenate(
                [jnp.exp2(s[..., j * V7X_LANES:(j + 1) * V7X_LANES] - m_new) for j in range(cpt)], axis=2)
            v_aug = jnp.concatenate([v_ref[ks, cols], ones], axis=1)
            pv = jnp.dot(p.reshape(rep * tq, DSA_TK).astype(BF16), v_aug,
                         preferred_element_type=F32).reshape(rep, tq, 2 * HEAD_DIM)
            acc_ref[hs] = alpha * acc_ref[hs] + pv[..., :HEAD_DIM]
            l_ref[hs] = alpha * l_ref[hs] + pv[..., HEAD_DIM:]
            m_ref[hs] = m_new
        return c

    lax.fori_loop(0, n_it, attn_body, 0)

    for h in range(N_HEADS):
        cols = slice(h * HEAD_DIM, (h + 1) * HEAD_DIM)
        z = z_ref[:, cols].astype(F32)
        o_ref[:, cols] = ((acc_ref[h] / l_ref[h]) * (z * _sigmoid(z))).astype(BF16)


def _dsa(q, qi, w_idx, proj, z_block, k_block, kidx):
    s = kidx.shape[0]
    tq = DSA_TQ
    width = N_HEADS * HEAD_DIM
    kvw = N_KV_HEADS * HEAD_DIM
    topk = min(TOPK_MAX, s // 4)
    return pl.pallas_call(
        functools.partial(_dsa_kernel, topk=topk),
        out_shape=jax.ShapeDtypeStruct((s, width), BF16),
        grid=(s // tq,),
        in_specs=[
            pl.BlockSpec((N_HEADS, tq, HEAD_DIM), lambda i: (0, i, 0)),
            pl.BlockSpec((IDX_HEADS, tq, IDX_DIM), lambda i: (0, i, 0)),
            pl.BlockSpec((tq, IDX_HEADS), lambda i: (i, 0)),
            pl.BlockSpec((tq, width), lambda i: (i, z_block)),
            pl.BlockSpec((s, kvw), lambda i: (0, k_block)),
            pl.BlockSpec((s, kvw), lambda i: (0, k_block + 1)),
            pl.BlockSpec((s, IDX_DIM), lambda i: (0, 0)),
        ],
        out_specs=pl.BlockSpec((tq, width), lambda i: (i, 0)),
        scratch_shapes=[
            pltpu.VMEM((s // V7X_LANES, tq, V7X_LANES), F32),
            pltpu.VMEM((IDX_HEADS, tq, V7X_LANES), F32),
            pltpu.VMEM((tq, V7X_LANES), I32),
            pltpu.VMEM((tq, V7X_LANES), F32),
            pltpu.VMEM((tq, V7X_LANES), F32),
            pltpu.VMEM((N_HEADS, tq, V7X_LANES), F32),
            pltpu.VMEM((N_HEADS, tq, V7X_LANES), F32),
            pltpu.VMEM((N_HEADS, tq, HEAD_DIM), F32),
        ],
        compiler_params=_cparams("arbitrary"),
        name="dsa",
    )(q, qi, w_idx, proj, proj, proj, kidx)


def _merge_kernel(ys_ref, ya_ref, ws_ref, wa_ref, gs_ref, ga_ref, o_ref):
    o_s = jnp.dot(ys_ref[...], ws_ref[...], preferred_element_type=F32)
    o_a = jnp.dot(ya_ref[...], wa_ref[...], preferred_element_type=F32)
    o_ref[...] = (gs_ref[...].astype(F32) * o_s + ga_ref[...].astype(F32) * o_a).astype(BF16)


def _merge(y_ssm, y_attn, w_so, w_ao, proj, gate_col, *, tm=1024, tn=512):
    s, kdim = y_ssm.shape
    d = w_so.shape[1]
    nj = d // tn
    g0 = gate_col // tn
    return pl.pallas_call(
        _merge_kernel,
        out_shape=jax.ShapeDtypeStruct((s, d), BF16),
        grid=(s // tm, nj),
        in_specs=[
            pl.BlockSpec((tm, kdim), lambda i, j: (i, 0)),
            pl.BlockSpec((tm, kdim), lambda i, j: (i, 0)),
            pl.BlockSpec((kdim, tn), lambda i, j: (0, j)),
            pl.BlockSpec((kdim, tn), lambda i, j: (0, j)),
            pl.BlockSpec((tm, tn), lambda i, j: (i, g0 + j)),
            pl.BlockSpec((tm, tn), lambda i, j: (i, g0 + nj + j)),
        ],
        out_specs=pl.BlockSpec((tm, tn), lambda i, j: (i, j)),
        compiler_params=_cparams("parallel", "parallel"),
        name="merge",
    )(y_ssm, y_attn, w_so, w_ao, proj, proj)


def _oproj_kernel(m_ref, w_ref, x_ref, o_ref):
    o_ref[...] = x_ref[...] + jnp.dot(m_ref[...], w_ref[...], preferred_element_type=F32)


def _oproj(merged, w_o, x, *, tm=1024, tn=512):
    s, d = x.shape
    return pl.pallas_call(
        _oproj_kernel,
        out_shape=jax.ShapeDtypeStruct((s, d), F32),
        grid=(s // tm, d // tn),
        in_specs=[
            pl.BlockSpec((tm, d), lambda i, j: (i, 0)),
            pl.BlockSpec((d, tn), lambda i, j: (0, j)),
            pl.BlockSpec((tm, tn), lambda i, j: (i, j)),
        ],
        out_specs=pl.BlockSpec((tm, tn), lambda i, j: (i, j)),
        compiler_params=_cparams("parallel", "parallel"),
        name="oproj",
    )(merged, w_o, x)


def _final_kernel(x_ref, p_ref, wp_ref, wg_ref, gple_ref, gpost_ref, gfin_ref, o_ref):
    x1 = x_ref[...]
    e = _rms(jnp.dot(p_ref[...].astype(BF16), wp_ref[...], preferred_element_type=F32), gpost_ref[...])
    xn = _rms(x1, gple_ref[...]).astype(BF16)
    gate = _sigmoid(jnp.dot(xn, wg_ref[...], preferred_element_type=F32))
    o_ref[...] = _rms(x1 + gate * e, gfin_ref[...])


def _final(x1, p, w_ple, w_gate, g_ple, g_post, g_final, *, tm=256):
    s, d = x1.shape
    pd = p.shape[1]
    vec = pl.BlockSpec((1, d), lambda i: (0, 0))
    return pl.pallas_call(
        _final_kernel,
        out_shape=jax.ShapeDtypeStruct((s, d), F32),
        grid=(s // tm,),
        in_specs=[
            pl.BlockSpec((tm, d), lambda i: (i, 0)),
            pl.BlockSpec((tm, pd), lambda i: (i, 0)),
            pl.BlockSpec((pd, d), lambda i: (0, 0)),
            pl.BlockSpec((d, d), lambda i: (0, 0)),
            vec, vec, vec,
        ],
        out_specs=pl.BlockSpec((tm, d), lambda i: (i, 0)),
        compiler_params=_cparams("parallel"),
        name="final",
    )(x1, p, w_ple, w_gate, g_ple, g_post, g_final)


def _layer(x, p, g_mix, w_in, g_q, w_uq, w_uq_idx, g_kidx, a_re, a_im, log_dt, b_re, b_im,
           c_re, c_im, d_skip, w_glu, w_ssm_out, w_attn_out, w_o, g_ple, w_ple_gate, w_ple,
           g_ple_post):
    s, d = x.shape
    ssm_w = d // 2
    attn_w = N_HEADS * HEAD_DIM
    kvw = N_KV_HEADS * HEAD_DIM
    n_a = 2 * ssm_w + Q_LORA_RANK + 2 * kvw + attn_w
    n_i = IDX_DIM + IDX_HEADS
    w_i = jnp.pad(w_in[:, n_a:n_a + n_i], ((0, 0), (0, V7X_LANES - n_i))).astype(BF16)
    w_g = w_in[:, n_a + n_i:]
    row = lambda v: v.reshape(1, -1)

    proj, c_q, kidx, w_idx = _proj(x, row(g_mix), w_in, n_a, 2 * ssm_w, w_g, w_i, row(g_kidx))

    n_chunks = s // S5_CHUNK
    n_groups = ssm_w // SSM_GROUP
    cp, bbt, e, a1, a2 = _s5_operators(a_re, a_im, log_dt, b_re, b_im, c_re, c_im,
                                       levels=n_chunks.bit_length() - 1)
    u_g = (proj[:, :ssm_w].reshape(n_chunks, S5_CHUNK, n_groups, SSM_GROUP)
           .transpose(2, 0, 1, 3).reshape(n_groups, n_chunks, S5_CHUNK * SSM_GROUP))
    y_g = _s5(u_g, cp, bbt, e, a1, a2)
    y = (y_g.reshape(n_groups, n_chunks, S5_CHUNK, SSM_GROUP)
         .transpose(1, 2, 0, 3).reshape(s, ssm_w))
    y_ssm = _s5_post(y, proj, row(d_skip), w_glu.astype(BF16))

    w_q = jnp.concatenate([w_uq, w_uq_idx], axis=1).astype(BF16)
    q, qi = _qproj(c_q, 0, row(g_q), w_q)
    k_off = 2 * ssm_w + Q_LORA_RANK
    z_block = (k_off + 2 * kvw) // attn_w
    y_attn = _dsa(q, qi, w_idx, proj, z_block, k_off // kvw, kidx)

    merged = _merge(y_ssm, y_attn, w_ssm_out.astype(BF16), w_attn_out.astype(BF16), proj, n_a)
    x1 = _oproj(merged, w_o.astype(BF16), x)
    return x1, (p, w_ple.astype(BF16), w_ple_gate.astype(BF16), row(g_ple), row(g_ple_post))


def kernel(x, p, g_mix, w_in, g_q, w_uq, w_uq_idx, g_kidx, a_re, a_im, log_dt, b_re, b_im, c_re, c_im,
           d_skip, w_glu, w_ssm_out, w_attn_out, w_o, g_ple, w_ple_gate, w_ple, g_ple_post, g_final):
    bsz, seq, d = x.shape
    depth = p.shape[0]
    assert bsz == 1 and depth == 1, "kernel is specialised to one sequence and one layer"
    x2 = x.reshape(seq, d)
    x1, (p0, wp, wg, gple, gpost) = _layer(
        x2, p[0, 0], g_mix[0], w_in[0], g_q[0], w_uq[0], w_uq_idx[0], g_kidx[0], a_re[0], a_im[0],
        log_dt[0], b_re[0], b_im[0], c_re[0], c_im[0], d_skip[0], w_glu[0], w_ssm_out[0],
        w_attn_out[0], w_o[0], g_ple[0], w_ple_gate[0], w_ple[0], g_ple_post[0])
    out = _final(x1, p0, wp, wg, gple, gpost, g_final.reshape(1, -1))
    return out.reshape(bsz, seq, d)
```

```python
import functools
import math

import numpy as np
import jax
import jax.numpy as jnp
from jax import lax
from jax.experimental import pallas as pl
from jax.experimental.pallas import tpu as pltpu

F32 = jnp.float32
BF16 = jnp.bfloat16
I32 = jnp.int32

EPS = 1e-6
SSM_GROUP = 16
SSM_STATE = 64
N_HEADS = 8
N_KV_HEADS = 2
HEAD_DIM = 128
Q_LORA_RANK = 512
IDX_HEADS = 16
IDX_DIM = 64
TOPK_MAX = 256

V7X_LANES = 128
V7X_VMEM_LIMIT = 56 * 1024 * 1024

S5_CHUNK = 32
DSA_TQ = 256
DSA_TK = 512
SEARCH_ROWS = 128

INT_MIN = -(2 ** 31)
KEY_NEG_INF = -2139095041
NEG_INIT = -1e30

_NT = (((1,), (1,)), ((), ()))


def _cparams(*sem):
    return pltpu.CompilerParams(dimension_semantics=sem, vmem_limit_bytes=V7X_VMEM_LIMIT)


def _rms(x, g):
    ms = jnp.mean(x * x, axis=-1, keepdims=True)
    return x * lax.rsqrt(ms + EPS) * g


def _sigmoid(x):
    return 1.0 / (1.0 + jnp.exp(-x))


def _gelu_tanh(x):
    c = math.sqrt(2.0 / math.pi)
    return 0.5 * x * (1.0 + jnp.tanh(c * (x + 0.044715 * (x * x * x))))


def _proj_kernel(x_ref, g_ref, wa_ref, wg_ref, wi_ref, gk_ref, ob_ref, cq_ref, ki_ref, widx_ref, h_ref,
                 *, n_a_tiles, cq_tile, w_scale):
    j = pl.program_id(1)

    @pl.when(j == 0)
    def _():
        h = _rms(x_ref[...], g_ref[...]).astype(BF16)
        h_ref[...] = h
        r = jnp.dot(h, wi_ref[...], preferred_element_type=F32)
        ki_ref[...] = _rms(r[:, :IDX_DIM], gk_ref[...]).astype(BF16)
        widx_ref[...] = r[:, IDX_DIM:IDX_DIM + IDX_HEADS] * w_scale

    @pl.when(j < n_a_tiles)
    def _():
        acc = jnp.dot(h_ref[...], wa_ref[...].astype(BF16), preferred_element_type=F32)
        ob_ref[...] = acc.astype(BF16)

        @pl.when(j == cq_tile)
        def _():
            cq_ref[...] = acc

    @pl.when(j >= n_a_tiles)
    def _():
        acc = jnp.dot(h_ref[...], wg_ref[...].astype(BF16), preferred_element_type=F32)
        ob_ref[...] = _sigmoid(acc).astype(BF16)


def _proj(x, g, w_in, n_a, cq_col, w_g, w_i, gk, *, tm=1024, tn=Q_LORA_RANK):
    s, d = x.shape
    n_g = w_g.shape[1]
    n_a_tiles = n_a // tn
    return pl.pallas_call(
        functools.partial(_proj_kernel, n_a_tiles=n_a_tiles, cq_tile=cq_col // tn,
                          w_scale=(IDX_HEADS ** -0.5) * (IDX_DIM ** -0.5)),
        out_shape=(jax.ShapeDtypeStruct((s, n_a + n_g), BF16),
                   jax.ShapeDtypeStruct((s, tn), F32),
                   jax.ShapeDtypeStruct((s, IDX_DIM), BF16),
                   jax.ShapeDtypeStruct((s, IDX_HEADS), F32)),
        grid=(s // tm, (n_a + n_g) // tn),
        in_specs=[
            pl.BlockSpec((tm, d), lambda i, j: (i, 0)),
            pl.BlockSpec((1, d), lambda i, j: (0, 0)),
            pl.BlockSpec((d, tn), lambda i, j: (0, jnp.minimum(j, n_a_tiles - 1))),
            pl.BlockSpec((d, tn), lambda i, j: (0, jnp.maximum(j - n_a_tiles, 0))),
            pl.BlockSpec((d, V7X_LANES), lambda i, j: (0, 0)),
            pl.BlockSpec((1, IDX_DIM), lambda i, j: (0, 0)),
        ],
        out_specs=(pl.BlockSpec((tm, tn), lambda i, j: (i, j)),
                   pl.BlockSpec((tm, tn), lambda i, j: (i, 0)),
                   pl.BlockSpec((tm, IDX_DIM), lambda i, j: (i, 0)),
                   pl.BlockSpec((tm, IDX_HEADS), lambda i, j: (i, 0))),
        scratch_shapes=[pltpu.VMEM((tm, d), BF16)],
        compiler_params=_cparams("parallel", "arbitrary"),
        name="proj",
    )(x, g, w_in, w_g, w_i, gk)


def _qproj_kernel(c_ref, g_ref, w_ref, q_ref, qi_ref, *, q_scale):
    cq = _rms(c_ref[...], g_ref[...]).astype(BF16)
    r = jnp.dot(cq, w_ref[...], preferred_element_type=F32)
    for h in range(N_HEADS):
        q_ref[h] = (r[:, h * HEAD_DIM:(h + 1) * HEAD_DIM] * q_scale).astype(BF16)
    base = N_HEADS * HEAD_DIM
    for h in range(IDX_HEADS):
        qi_ref[h] = r[:, base + h * IDX_DIM: base + (h + 1) * IDX_DIM].astype(BF16)


def _qproj(proj_a, cq_block, g, w, *, tm=512):
    s = proj_a.shape[0]
    return pl.pallas_call(
        functools.partial(_qproj_kernel, q_scale=HEAD_DIM ** -0.5 * math.log2(math.e)),
        out_shape=(jax.ShapeDtypeStruct((N_HEADS, s, HEAD_DIM), BF16),
                   jax.ShapeDtypeStruct((IDX_HEADS, s, IDX_DIM), BF16)),
        grid=(s // tm,),
        in_specs=[
            pl.BlockSpec((tm, Q_LORA_RANK), lambda i: (i, cq_block)),
            pl.BlockSpec((1, Q_LORA_RANK), lambda i: (0, 0)),
            pl.BlockSpec(w.shape, lambda i: (0, 0)),
        ],
        out_specs=(pl.BlockSpec((N_HEADS, tm, HEAD_DIM), lambda i: (0, i, 0)),
                   pl.BlockSpec((IDX_HEADS, tm, IDX_DIM), lambda i: (0, i, 0))),
        compiler_params=_cparams("parallel"),
        name="qproj",
    )(proj_a, g, w)


def _cmul(ar, ai, br, bi):
    return ar * br - ai * bi, ar * bi + ai * br


def _s5_operators(a_re, a_im, log_dt, b_re, b_im, c_re, c_im, levels):
    t_len = S5_CHUNK
    g_n, n_st = a_re.shape
    dt = jnp.exp(log_dt)[:, None]
    mag = jnp.exp(dt * a_re)
    abr = mag * jnp.cos(dt * a_im)
    abi = mag * jnp.sin(dt * a_im)
    den = a_re * a_re + a_im * a_im
    nr = abr - 1.0
    f_re = (nr * a_re + abi * a_im) / den
    f_im = (abi * a_re - nr * a_im) / den
    bt_re = b_re.transpose(0, 2, 1)
    bt_im = b_im.transpose(0, 2, 1)
    bbt_re = f_re[:, None, :] * bt_re - f_im[:, None, :] * bt_im
    bbt_im = f_re[:, None, :] * bt_im + f_im[:, None, :] * bt_re

    nbits = t_len.bit_length()
    sq = [(abr, abi)]
    for _ in range(nbits - 1):
        sq.append(_cmul(*sq[-1], *sq[-1]))
    j = np.arange(t_len + 1)
    pr = jnp.ones((g_n, t_len + 1, n_st), F32)
    pi = jnp.zeros((g_n, t_len + 1, n_st), F32)
    for b in range(nbits):
        bit = jnp.asarray(((j >> b) & 1).astype(bool))[None, :, None]
        mr, mi = _cmul(pr, pi, sq[b][0][:, None, :], sq[b][1][:, None, :])
        pr = jnp.where(bit, mr, pr)
        pi = jnp.where(bit, mi, pi)

    prj, pij = pr[:, :, None, :], pi[:, :, None, :]
    cp_re = c_re[:, None] * prj - c_im[:, None] * pij
    cp_im = c_re[:, None] * pij + c_im[:, None] * prj
    cp = jnp.concatenate([cp_re, -cp_im], axis=-1).reshape(g_n, (t_len + 1) * SSM_GROUP, 2 * n_st)

    prr, pir = prj[:, t_len - 1::-1], pij[:, t_len - 1::-1]
    e_re = prr * bbt_re[:, None] - pir * bbt_im[:, None]
    e_im = prr * bbt_im[:, None] + pir * bbt_re[:, None]
    e = jnp.concatenate([e_re, e_im], axis=-1).reshape(g_n, t_len * SSM_GROUP, 2 * n_st)

    lv_r, lv_i = [pr[:, t_len]], [pi[:, t_len]]
    for _ in range(levels - 1):
        r2, i2 = _cmul(lv_r[-1], lv_i[-1], lv_r[-1], lv_i[-1])
        lv_r.append(r2)
        lv_i.append(i2)
    ar = jnp.stack(lv_r, axis=1)
    ai = jnp.stack(lv_i, axis=1)
    a1 = jnp.concatenate([ar, ar], axis=-1)
    a2 = jnp.concatenate([-ai, ai], axis=-1)
    return cp, jnp.concatenate([bbt_re, bbt_im], axis=-1), e.astype(BF16), a1, a2


def _s5_kernel(*refs, t_len):
    gpt = V7X_LANES // SSM_GROUP
    u_refs = refs[:t_len]
    cp_ref, bbt_ref, e_ref, a1_ref, a2_ref, y_ref, rep_ref, master_ref, ebd_ref, fbd_ref = refs[t_len:]
    ct = t_len * SSM_GROUP
    n2 = cp_ref.shape[2]
    pair = 2 * V7X_LANES

    @pl.when(pl.program_id(0) == 0)
    def _():
        ebd_ref[...] = jnp.zeros(ebd_ref.shape, BF16)
        fbd_ref[...] = jnp.zeros(fbd_ref.shape, BF16)
        r = lax.broadcasted_iota(I32, rep_ref.shape, 0)
        q = lax.broadcasted_iota(I32, rep_ref.shape, 1)
        same_lag = (r // SSM_GROUP) == (q // V7X_LANES)
        same_ch = (r % SSM_GROUP) == (q % SSM_GROUP)
        rep_ref[...] = jnp.where(same_lag, jnp.where(same_ch, 1.0, 0.0), 0.0).astype(BF16)

    kt8 = jnp.concatenate(
        [lax.dot_general(bbt_ref[k], cp_ref[k, :ct, :], _NT, precision=lax.Precision.HIGHEST,
                         preferred_element_type=F32) for k in range(gpt)], axis=0)
    dall = jnp.dot(kt8.astype(BF16), rep_ref[...], preferred_element_type=F32)
    rg = lax.broadcasted_iota(I32, dall.shape, 0) // SSM_GROUP
    lg = (lax.broadcasted_iota(I32, dall.shape, 1) % V7X_LANES) // SSM_GROUP
    dall = jnp.where(rg == lg, dall, 0.0).astype(BF16)
    for b in range(t_len - 1):
        r0 = (t_len - 2 - b) * V7X_LANES
        master_ref[b * V7X_LANES:(b + 1) * V7X_LANES, :] = dall[:, r0:r0 + pair]
    master_ref[(t_len - 1) * V7X_LANES:, :] = jnp.concatenate(
        [jnp.zeros((V7X_LANES, V7X_LANES), BF16), dall[:, :V7X_LANES]], axis=1)
    for s in range(t_len):
        for k in range(gpt):
            rows = slice(s * V7X_LANES + k * SSM_GROUP, s * V7X_LANES + (k + 1) * SSM_GROUP)
            cols = slice(k * n2, (k + 1) * n2)
            ebd_ref[rows, cols] = e_ref[k, s * SSM_GROUP:(s + 1) * SSM_GROUP, :]
            fbd_ref[rows, cols] = cp_ref[k, (s + 1) * SSM_GROUP:(s + 2) * SSM_GROUP, :].astype(BF16)

    u = jnp.concatenate([r[...] for r in u_refs], axis=1)
    hend = jnp.dot(u, ebd_ref[...], preferred_element_type=F32)
    row = lax.broadcasted_iota(I32, (hend.shape[0], n2), 0)
    carries = []
    for k in range(gpt):
        p = hend[:, k * n2:(k + 1) * n2]
        for lv in range(a1_ref.shape[1]):
            sh = 1 << lv
            a1 = a1_ref[k, lv:lv + 1, :]
            a2 = a2_ref[k, lv:lv + 1, :]
            ps = jnp.where(row >= sh, pltpu.roll(p, sh, axis=0), 0.0)
            p = p + a1 * ps + a2 * pltpu.roll(ps, n2 // 2, axis=1)
        carries.append(jnp.where(row >= 1, pltpu.roll(p, 1, axis=0), 0.0).astype(BF16))
    carry = jnp.concatenate(carries, axis=1)

    for tp in range(t_len // 2):
        kdim = (tp + 1) * pair
        y2 = jnp.dot(u[:, :kdim], master_ref[(t_len - 2 - 2 * tp) * V7X_LANES:, :],
                     preferred_element_type=F32)
        y2 = y2 + lax.dot_general(carry, fbd_ref[tp * pair:(tp + 1) * pair, :], _NT,
                                  preferred_element_type=F32)
        y_ref[2 * tp] = y2[:, :V7X_LANES]
        y_ref[2 * tp + 1] = y2[:, V7X_LANES:]


def _s5(proj, ssm_w, cp, bbt, e, a1, a2):
    s, width = proj.shape
    t_len = S5_CHUNK
    n_chunks = s // t_len
    gpt = V7X_LANES // SSM_GROUP
    ct = t_len * SSM_GROUP
    n2 = e.shape[-1]
    levels = a1.shape[1]
    assert n_chunks == 1 << levels and n2 == V7X_LANES
    tiles = width // V7X_LANES
    u2 = proj.reshape(n_chunks, t_len * width)
    u_specs = [pl.BlockSpec((n_chunks, V7X_LANES), lambda g, t=t: (0, t * tiles + g)) for t in range(t_len)]
    grp = lambda rows: pl.BlockSpec((gpt, rows, n2), lambda g: (g, 0, 0))
    return pl.pallas_call(
        functools.partial(_s5_kernel, t_len=t_len),
        out_shape=jax.ShapeDtypeStruct((t_len, n_chunks, ssm_w), F32),
        grid=(ssm_w // V7X_LANES,),
        in_specs=u_specs + [grp(ct + SSM_GROUP), grp(SSM_GROUP), grp(ct), grp(levels), grp(levels)],
        out_specs=pl.BlockSpec((t_len, n_chunks, V7X_LANES), lambda g: (0, 0, g)),
        scratch_shapes=[
            pltpu.VMEM((ct, t_len * V7X_LANES), BF16),
            pltpu.VMEM((t_len * V7X_LANES, 2 * V7X_LANES), BF16),
            pltpu.VMEM((t_len * V7X_LANES, gpt * n2), BF16),
            pltpu.VMEM((t_len * V7X_LANES, gpt * n2), BF16),
        ],
        compiler_params=_cparams("arbitrary"),
        name="s5",
    )(*([u2] * t_len), cp, bbt, e, a1, a2)


def _s5_post_kernel(y_ref, u_ref, z_ref, d_ref, w_ref, o_ref):
    width = y_ref.shape[1]
    yy = _gelu_tanh(y_ref[...] + d_ref[...] * u_ref[...].astype(F32))
    r = jnp.dot(yy.astype(BF16), w_ref[...], preferred_element_type=F32)
    glu = r[:, :width] * _sigmoid(r[:, width:])
    z = z_ref[...].astype(F32)
    o_ref[...] = (glu * (z * _sigmoid(z))).astype(BF16)


def _s5_post(y, proj_a, d_skip, w_glu, *, tm=512):
    s, width = y.shape
    return pl.pallas_call(
        _s5_post_kernel,
        out_shape=jax.ShapeDtypeStruct((s, width), BF16),
        grid=(s // tm,),
        in_specs=[
            pl.BlockSpec((tm, width), lambda i: (i, 0)),
            pl.BlockSpec((tm, width), lambda i: (i, 0)),
            pl.BlockSpec((tm, width), lambda i: (i, 1)),
            pl.BlockSpec((1, width), lambda i: (0, 0)),
            pl.BlockSpec(w_glu.shape, lambda i: (0, 0)),
        ],
        out_specs=pl.BlockSpec((tm, width), lambda i: (i, 0)),
        compiler_params=_cparams("parallel"),
        name="s5_post",
    )(y, proj_a, proj_a, d_skip, w_glu)


def _key_to_float(key):
    bits = key ^ (lax.shift_right_arithmetic(key, 31) & 0x7FFFFFFF)
    return lax.bitcast_convert_type(bits, F32)


def _dsa_kernel(q_ref, qi_ref, w_ref, z_ref, k_ref, v_ref, ki_ref, o_ref,
                sc_ref, wb_ref, res_ref, cand_ref, cnt_ref, m_ref, l_ref, acc_ref, *, topk):
    tq = DSA_TQ
    halves = tq // V7X_LANES
    cpt = DSA_TK // V7X_LANES
    rep = N_HEADS // N_KV_HEADS
    i = pl.program_id(0)
    n_valid = halves * (i + 1)
    n_it = (n_valid + cpt - 1) // cpt

    for h in range(IDX_HEADS):
        wb_ref[h] = jnp.broadcast_to(w_ref[:, h:h + 1], (tq, V7X_LANES))
    qi = qi_ref[...].reshape(IDX_HEADS * tq, IDX_DIM)

    def score_tile(kt, diag):
        ki = ki_ref[pl.ds(pl.multiple_of(kt * tq, tq), tq), :]
        r = lax.dot_general(qi, ki, _NT, preferred_element_type=F32)
        for half in range(halves):
            lanes = slice(half * V7X_LANES, (half + 1) * V7X_LANES)
            acc = jnp.zeros((tq, V7X_LANES), F32)
            for h in range(IDX_HEADS):
                acc = acc + wb_ref[h] * jnp.maximum(r[h * tq:(h + 1) * tq, lanes], 0.0)
            if diag:
                col = lax.broadcasted_iota(I32, acc.shape, 1) + half * V7X_LANES
                rowi = lax.broadcasted_iota(I32, acc.shape, 0)
                acc = jnp.where(col <= rowi, acc, -jnp.inf)
            sc_ref[halves * kt + half] = acc

    def score_body(kt, c):
        score_tile(kt, False)
        return c

    lax.fori_loop(0, i, score_body, 0)
    score_tile(i, True)

    for j in range(cpt - halves):
        @pl.when(n_valid + j < n_it * cpt)
        def _():
            sc_ref[n_valid + j] = jnp.full((tq, V7X_LANES), -jnp.inf, F32)

    res_ref[...] = jnp.full(res_ref.shape, INT_MIN, I32)

    def pass_body(p, c):
        bit = lax.shift_left(jnp.int32(1), 31 - p)
        cand_ref[...] = _key_to_float(res_ref[...] + bit)
        cnts = []
        for rc in range(tq // SEARCH_ROWS):
            rows = slice(rc * SEARCH_ROWS, (rc + 1) * SEARCH_ROWS)
            cand = cand_ref[rows, :]

            def it_body(it, cnt, rows=rows, cand=cand):
                for j in range(cpt):
                    cnt = cnt + jnp.where(sc_ref[cpt * it + j, rows, :] >= cand, 1.0, 0.0)
                return cnt

            cnts.append(lax.fori_loop(0, n_it, it_body, jnp.zeros((SEARCH_ROWS, V7X_LANES), F32)))
        tot = jnp.sum(jnp.concatenate(cnts, axis=0), axis=1, keepdims=True)
        ok = tot >= float(topk)
        res = res_ref[...]
        res_ref[...] = jnp.where(ok, res + bit, res)
        cnt_ref[...] = jnp.where(ok, tot, cnt_ref[...])
        return c

    cnt_ref[...] = (jnp.full(cnt_ref.shape, DSA_TK, I32) * n_it).astype(F32)
    lax.fori_loop(0, 32, pass_body, 0)
    res = res_ref[...]
    tau = _key_to_float(jnp.maximum(res, KEY_NEG_INF + 1))
    cand_ref[...] = tau

    excess = jnp.where(res > KEY_NEG_INF, cnt_ref[...] - float(topk), 0.0).astype(I32)
    lane = lax.broadcasted_iota(I32, (tq, V7X_LANES), 1)
    n_chunks = n_it * cpt

    def drop_body(r, c):
        active = excess > r

        def min_body(ch, mv):
            sc = sc_ref[ch]
            return jnp.minimum(mv, jnp.where(sc >= tau, sc, jnp.inf))

        mv = lax.fori_loop(0, n_chunks, min_body, jnp.full((tq, V7X_LANES), jnp.inf, F32))
        mval = jnp.min(mv, axis=1, keepdims=True)

        def idx_body(ch, ix):
            col = (lane + ch * V7X_LANES).astype(F32)
            return jnp.maximum(ix, jnp.where(sc_ref[ch] == mval, col, -1.0))

        ix = lax.fori_loop(0, n_chunks, idx_body, jnp.full((tq, V7X_LANES), -1.0, F32))
        idx = jnp.max(ix, axis=1, keepdims=True)

        def drop_chunk(ch, c2):
            col = (lane + ch * V7X_LANES).astype(F32)
            sc = sc_ref[ch]
            sc_ref[ch] = jnp.where(active, jnp.where(col == idx, -jnp.inf, sc), sc)
            return c2

        lax.fori_loop(0, n_chunks, drop_chunk, 0)
        return c

    lax.fori_loop(0, jnp.max(excess), drop_body, 0)

    m_ref[...] = jnp.full(m_ref.shape, NEG_INIT, F32)
    l_ref[...] = jnp.zeros(l_ref.shape, F32)
    acc_ref[...] = jnp.zeros(acc_ref.shape, F32)
    ones = jnp.ones((DSA_TK, HEAD_DIM), BF16)

    def attn_body(it, c):
        tau = cand_ref[...]
        neg = jnp.concatenate(
            [jnp.where(sc_ref[cpt * it + j] >= tau, 0.0, -jnp.inf) for j in range(cpt)], axis=1)
        ks = pl.ds(pl.multiple_of(it * DSA_TK, DSA_TK), DSA_TK)
        for g in range(N_KV_HEADS):
            hs = slice(g * rep, (g + 1) * rep)
            cols = slice(g * HEAD_DIM, (g + 1) * HEAD_DIM)
            qg = q_ref[hs].reshape(rep * tq, HEAD_DIM)
            s = lax.dot_general(qg, k_ref[ks, cols], _NT, preferred_element_type=F32)
            s = s.reshape(rep, tq, DSA_TK) + neg[None]
            m_prev = m_ref[hs]
            m_new = jnp.maximum(m_prev, jnp.max(s, axis=2, keepdims=True))
            alpha = jnp.exp2(m_prev - m_new)
            p = jnp.concatenate(
                [jnp.exp2(s[..., j * V7X_LANES:(j + 1) * V7X_LANES] - m_new) for j in range(cpt)], axis=2)
            v_aug = jnp.concatenate([v_ref[ks, cols], ones], axis=1)
            pv = jnp.dot(p.reshape(rep * tq, DSA_TK).astype(BF16), v_aug,
                         preferred_element_type=F32).reshape(rep, tq, 2 * HEAD_DIM)
            acc_ref[hs] = alpha * acc_ref[hs] + pv[..., :HEAD_DIM]
            l_ref[hs] = alpha * l_ref[hs] + pv[..., HEAD_DIM:]
            m_ref[hs] = m_new
        return c

    lax.fori_loop(0, n_it, attn_body, 0)

    for h in range(N_HEADS):
        cols = slice(h * HEAD_DIM, (h + 1) * HEAD_DIM)
        z = z_ref[:, cols].astype(F32)
        o_ref[:, cols] = ((acc_ref[h] / l_ref[h]) * (z * _sigmoid(z))).astype(BF16)


def _dsa(q, qi, w_idx, proj, z_block, k_block, kidx):
    s = kidx.shape[0]
    tq = DSA_TQ
    width = N_HEADS * HEAD_DIM
    kvw = N_KV_HEADS * HEAD_DIM
    topk = min(TOPK_MAX, s // 4)
    return pl.pallas_call(
        functools.partial(_dsa_kernel, topk=topk),
        out_shape=jax.ShapeDtypeStruct((s, width), BF16),
        grid=(s // tq,),
        in_specs=[
            pl.BlockSpec((N_HEADS, tq, HEAD_DIM), lambda i: (0, i, 0)),
            pl.BlockSpec((IDX_HEADS, tq, IDX_DIM), lambda i: (0, i, 0)),
            pl.BlockSpec((tq, IDX_HEADS), lambda i: (i, 0)),
            pl.BlockSpec((tq, width), lambda i: (i, z_block)),
            pl.BlockSpec((s, kvw), lambda i: (0, k_block)),
            pl.BlockSpec((s, kvw), lambda i: (0, k_block + 1)),
            pl.BlockSpec((s, IDX_DIM), lambda i: (0, 0)),
        ],
        out_specs=pl.BlockSpec((tq, width), lambda i: (i, 0)),
        scratch_shapes=[
            pltpu.VMEM((s // V7X_LANES, tq, V7X_LANES), F32),
            pltpu.VMEM((IDX_HEADS, tq, V7X_LANES), F32),
            pltpu.VMEM((tq, V7X_LANES), I32),
            pltpu.VMEM((tq, V7X_LANES), F32),
            pltpu.VMEM((tq, V7X_LANES), F32),
            pltpu.VMEM((N_HEADS, tq, V7X_LANES), F32),
            pltpu.VMEM((N_HEADS, tq, V7X_LANES), F32),
            pltpu.VMEM((N_HEADS, tq, HEAD_DIM), F32),
        ],
        compiler_params=_cparams("arbitrary"),
        name="dsa",
    )(q, qi, w_idx, proj, proj, proj, kidx)


def _merge_kernel(ys_ref, ya_ref, ws_ref, wa_ref, gs_ref, ga_ref, o_ref):
    o_s = jnp.dot(ys_ref[...], ws_ref[...], preferred_element_type=F32)
    o_a = jnp.dot(ya_ref[...], wa_ref[...], preferred_element_type=F32)
    o_ref[...] = (gs_ref[...].astype(F32) * o_s + ga_ref[...].astype(F32) * o_a).astype(BF16)


def _merge(y_ssm, y_attn, w_so, w_ao, proj, gate_col, *, tm=1024, tn=512):
    s, kdim = y_ssm.shape
    d = w_so.shape[1]
    nj = d // tn
    g0 = gate_col // tn
    return pl.pallas_call(
        _merge_kernel,
        out_shape=jax.ShapeDtypeStruct((s, d), BF16),
        grid=(s // tm, nj),
        in_specs=[
            pl.BlockSpec((tm, kdim), lambda i, j: (i, 0)),
            pl.BlockSpec((tm, kdim), lambda i, j: (i, 0)),
            pl.BlockSpec((kdim, tn), lambda i, j: (0, j)),
            pl.BlockSpec((kdim, tn), lambda i, j: (0, j)),
            pl.BlockSpec((tm, tn), lambda i, j: (i, g0 + j)),
            pl.BlockSpec((tm, tn), lambda i, j: (i, g0 + nj + j)),
        ],
        out_specs=pl.BlockSpec((tm, tn), lambda i, j: (i, j)),
        compiler_params=_cparams("parallel", "parallel"),
        name="merge",
    )(y_ssm, y_attn, w_so, w_ao, proj, proj)


def _oproj_kernel(m_ref, w_ref, x_ref, o_ref):
    o_ref[...] = x_ref[...] + jnp.dot(m_ref[...], w_ref[...], preferred_element_type=F32)


def _oproj(merged, w_o, x, *, tm=1024, tn=512):
    s, d = x.shape
    return pl.pallas_call(
        _oproj_kernel,
        out_shape=jax.ShapeDtypeStruct((s, d), F32),
        grid=(s // tm, d // tn),
        in_specs=[
            pl.BlockSpec((tm, d), lambda i, j: (i, 0)),
            pl.BlockSpec((d, tn), lambda i, j: (0, j)),
            pl.BlockSpec((tm, tn), lambda i, j: (i, j)),
        ],
        out_specs=pl.BlockSpec((tm, tn), lambda i, j: (i, j)),
        compiler_params=_cparams("parallel", "parallel"),
        name="oproj",
    )(merged, w_o, x)


def _final_kernel(x_ref, p_ref, wp_ref, wg_ref, gple_ref, gpost_ref, gfin_ref, o_ref):
    x1 = x_ref[...]
    e = _rms(jnp.dot(p_ref[...].astype(BF16), wp_ref[...], preferred_element_type=F32), gpost_ref[...])
    xn = _rms(x1, gple_ref[...]).astype(BF16)
    gate = _sigmoid(jnp.dot(xn, wg_ref[...], preferred_element_type=F32))
    o_ref[...] = _rms(x1 + gate * e, gfin_ref[...])


def _final(x1, p, w_ple, w_gate, g_ple, g_post, g_final, *, tm=256):
    s, d = x1.shape
    pd = p.shape[1]
    vec = pl.BlockSpec((1, d), lambda i: (0, 0))
    return pl.pallas_call(
        _final_kernel,
        out_shape=jax.ShapeDtypeStruct((s, d), F32),
        grid=(s // tm,),
        in_specs=[
            pl.BlockSpec((tm, d), lambda i: (i, 0)),
            pl.BlockSpec((tm, pd), lambda i: (i, 0)),
            pl.BlockSpec((pd, d), lambda i: (0, 0)),
            pl.BlockSpec((d, d), lambda i: (0, 0)),
            vec, vec, vec,
        ],
        out_specs=pl.BlockSpec((tm, d), lambda i: (i, 0)),
        compiler_params=_cparams("parallel"),
        name="final",
    )(x1, p, w_ple, w_gate, g_ple, g_post, g_final)


def _layer(x, p, g_mix, w_in, g_q, w_uq, w_uq_idx, g_kidx, a_re, a_im, log_dt, b_re, b_im,
           c_re, c_im, d_skip, w_glu, w_ssm_out, w_attn_out, w_o, g_ple, w_ple_gate, w_ple,
           g_ple_post):
    s, d = x.shape
    ssm_w = d // 2
    attn_w = N_HEADS * HEAD_DIM
    kvw = N_KV_HEADS * HEAD_DIM
    n_a = 2 * ssm_w + Q_LORA_RANK + 2 * kvw + attn_w
    n_i = IDX_DIM + IDX_HEADS
    w_i = jnp.pad(w_in[:, n_a:n_a + n_i], ((0, 0), (0, V7X_LANES - n_i))).astype(BF16)
    w_g = w_in[:, n_a + n_i:]
    row = lambda v: v.reshape(1, -1)

    proj, c_q, kidx, w_idx = _proj(x, row(g_mix), w_in, n_a, 2 * ssm_w, w_g, w_i, row(g_kidx))

    n_chunks = s // S5_CHUNK
    cp, bbt, e, a1, a2 = _s5_operators(a_re, a_im, log_dt, b_re, b_im, c_re, c_im,
                                       levels=n_chunks.bit_length() - 1)
    y = _s5(proj, ssm_w, cp, bbt, e, a1, a2).transpose(1, 0, 2).reshape(s, ssm_w)
    y_ssm = _s5_post(y, proj, row(d_skip), w_glu.astype(BF16))

    w_q = jnp.concatenate([w_uq, w_uq_idx], axis=1).astype(BF16)
    q, qi = _qproj(c_q, 0, row(g_q), w_q)
    k_off = 2 * ssm_w + Q_LORA_RANK
    z_block = (k_off + 2 * kvw) // attn_w
    y_attn = _dsa(q, qi, w_idx, proj, z_block, k_off // kvw, kidx)

    merged = _merge(y_ssm, y_attn, w_ssm_out.astype(BF16), w_attn_out.astype(BF16), proj, n_a)
    x1 = _oproj(merged, w_o.astype(BF16), x)
    return x1, (p, w_ple.astype(BF16), w_ple_gate.astype(BF16), row(g_ple), row(g_ple_post))


def kernel(x, p, g_mix, w_in, g_q, w_uq, w_uq_idx, g_kidx, a_re, a_im, log_dt, b_re, b_im, c_re, c_im,
           d_skip, w_glu, w_ssm_out, w_attn_out, w_o, g_ple, w_ple_gate, w_ple, g_ple_post, g_final):
    bsz, seq, d = x.shape
    depth = p.shape[0]
    assert bsz == 1 and depth == 1, "kernel is specialised to one sequence and one layer"
    x2 = x.reshape(seq, d)
    x1, (p0, wp, wg, gple, gpost) = _layer(
        x2, p[0, 0], g_mix[0], w_in[0], g_q[0], w_uq[0], w_uq_idx[0], g_kidx[0], a_re[0], a_im[0],
        log_dt[0], b_re[0], b_im[0], c_re[0], c_im[0], d_skip[0], w_glu[0], w_ssm_out[0],
        w_attn_out[0], w_o[0], g_ple[0], w_ple_gate[0], w_ple[0], g_ple_post[0])
    out = _final(x1, p0, wp, wg, gple, gpost, g_final.reshape(1, -1))
    return out.reshape(bsz, seq, d)
```

```python
import functools
import math

import numpy as np
import jax
import jax.numpy as jnp
from jax import lax
from jax.experimental import pallas as pl
from jax.experimental.pallas import tpu as pltpu

F32 = jnp.float32
BF16 = jnp.bfloat16
I32 = jnp.int32

EPS = 1e-6
SSM_GROUP = 16
SSM_STATE = 64
N_HEADS = 8
N_KV_HEADS = 2
HEAD_DIM = 128
Q_LORA_RANK = 512
IDX_HEADS = 16
IDX_DIM = 64
TOPK_MAX = 256

V7X_LANES = 128
V7X_VMEM_LIMIT = 56 * 1024 * 1024

S5_CHUNK = 32
DSA_TQ = 256
DSA_TK = 512
SEARCH_ROWS = 128

INT_MIN = -(2 ** 31)
KEY_NEG_INF = -2139095041
NEG_INIT = -1e30

_NT = (((1,), (1,)), ((), ()))


def _cparams(*sem):
    return pltpu.CompilerParams(dimension_semantics=sem, vmem_limit_bytes=V7X_VMEM_LIMIT)


def _rms(x, g):
    ms = jnp.mean(x * x, axis=-1, keepdims=True)
    return x * lax.rsqrt(ms + EPS) * g


def _sigmoid(x):
    return 1.0 / (1.0 + jnp.exp(-x))


def _gelu_tanh(x):
    c = math.sqrt(2.0 / math.pi)
    return 0.5 * x * (1.0 + jnp.tanh(c * (x + 0.044715 * (x * x * x))))


def _proj_kernel(x_ref, g_ref, wa_ref, wg_ref, wi_ref, gk_ref, ob_ref, cq_ref, ki_ref, widx_ref, h_ref,
                 *, n_a_tiles, cq_tile, w_scale):
    j = pl.program_id(1)

    @pl.when(j == 0)
    def _():
        h = _rms(x_ref[...], g_ref[...]).astype(BF16)
        h_ref[...] = h
        r = jnp.dot(h, wi_ref[...], preferred_element_type=F32)
        ki_ref[...] = _rms(r[:, :IDX_DIM], gk_ref[...]).astype(BF16)
        widx_ref[...] = r[:, IDX_DIM:IDX_DIM + IDX_HEADS] * w_scale

    @pl.when(j < n_a_tiles)
    def _():
        acc = jnp.dot(h_ref[...], wa_ref[...].astype(BF16), preferred_element_type=F32)
        ob_ref[...] = acc.astype(BF16)

        @pl.when(j == cq_tile)
        def _():
            cq_ref[...] = acc

    @pl.when(j >= n_a_tiles)
    def _():
        acc = jnp.dot(h_ref[...], wg_ref[...].astype(BF16), preferred_element_type=F32)
        ob_ref[...] = _sigmoid(acc).astype(BF16)


def _proj(x, g, w_in, n_a, cq_col, w_g, w_i, gk, *, tm=1024, tn=Q_LORA_RANK):
    s, d = x.shape
    n_g = w_g.shape[1]
    n_a_tiles = n_a // tn
    return pl.pallas_call(
        functools.partial(_proj_kernel, n_a_tiles=n_a_tiles, cq_tile=cq_col // tn,
                          w_scale=(IDX_HEADS ** -0.5) * (IDX_DIM ** -0.5)),
        out_shape=(jax.ShapeDtypeStruct((s, n_a + n_g), BF16),
                   jax.ShapeDtypeStruct((s, tn), F32),
                   jax.ShapeDtypeStruct((s, IDX_DIM), BF16),
                   jax.ShapeDtypeStruct((s, IDX_HEADS), F32)),
        grid=(s // tm, (n_a + n_g) // tn),
        in_specs=[
            pl.BlockSpec((tm, d), lambda i, j: (i, 0)),
            pl.BlockSpec((1, d), lambda i, j: (0, 0)),
            pl.BlockSpec((d, tn), lambda i, j: (0, jnp.minimum(j, n_a_tiles - 1))),
            pl.BlockSpec((d, tn), lambda i, j: (0, jnp.maximum(j - n_a_tiles, 0))),
            pl.BlockSpec((d, V7X_LANES), lambda i, j: (0, 0)),
            pl.BlockSpec((1, IDX_DIM), lambda i, j: (0, 0)),
        ],
        out_specs=(pl.BlockSpec((tm, tn), lambda i, j: (i, j)),
                   pl.BlockSpec((tm, tn), lambda i, j: (i, 0)),
                   pl.BlockSpec((tm, IDX_DIM), lambda i, j: (i, 0)),
                   pl.BlockSpec((tm, IDX_HEADS), lambda i, j: (i, 0))),
        scratch_shapes=[pltpu.VMEM((tm, d), BF16)],
        compiler_params=_cparams("parallel", "arbitrary"),
        name="proj",
    )(x, g, w_in, w_g, w_i, gk)


def _qproj_kernel(c_ref, g_ref, w_ref, q_ref, qi_ref, *, q_scale):
    cq = _rms(c_ref[...], g_ref[...]).astype(BF16)
    r = jnp.dot(cq, w_ref[...], preferred_element_type=F32)
    for h in range(N_HEADS):
        q_ref[h] = (r[:, h * HEAD_DIM:(h + 1) * HEAD_DIM] * q_scale).astype(BF16)
    base = N_HEADS * HEAD_DIM
    for h in range(IDX_HEADS):
        qi_ref[h] = r[:, base + h * IDX_DIM: base + (h + 1) * IDX_DIM].astype(BF16)


def _qproj(proj_a, cq_block, g, w, *, tm=512):
    s = proj_a.shape[0]
    return pl.pallas_call(
        functools.partial(_qproj_kernel, q_scale=HEAD_DIM ** -0.5 * math.log2(math.e)),
        out_shape=(jax.ShapeDtypeStruct((N_HEADS, s, HEAD_DIM), BF16),
                   jax.ShapeDtypeStruct((IDX_HEADS, s, IDX_DIM), BF16)),
        grid=(s // tm,),
        in_specs=[
            pl.BlockSpec((tm, Q_LORA_RANK), lambda i: (i, cq_block)),
            pl.BlockSpec((1, Q_LORA_RANK), lambda i: (0, 0)),
            pl.BlockSpec(w.shape, lambda i: (0, 0)),
        ],
        out_specs=(pl.BlockSpec((N_HEADS, tm, HEAD_DIM), lambda i: (0, i, 0)),
                   pl.BlockSpec((IDX_HEADS, tm, IDX_DIM), lambda i: (0, i, 0))),
        compiler_params=_cparams("parallel"),
        name="qproj",
    )(proj_a, g, w)


def _cmul(ar, ai, br, bi):
    return ar * br - ai * bi, ar * bi + ai * br


def _s5_operators(a_re, a_im, log_dt, b_re, b_im, c_re, c_im, levels):
    t_len = S5_CHUNK
    g_n, n_st = a_re.shape
    dt = jnp.exp(log_dt)[:, None]
    mag = jnp.exp(dt * a_re)
    abr = mag * jnp.cos(dt * a_im)
    abi = mag * jnp.sin(dt * a_im)
    den = a_re * a_re + a_im * a_im
    nr = abr - 1.0
    f_re = (nr * a_re + abi * a_im) / den
    f_im = (abi * a_re - nr * a_im) / den
    bt_re = b_re.transpose(0, 2, 1)
    bt_im = b_im.transpose(0, 2, 1)
    bbt_re = f_re[:, None, :] * bt_re - f_im[:, None, :] * bt_im
    bbt_im = f_re[:, None, :] * bt_im + f_im[:, None, :] * bt_re

    nbits = t_len.bit_length()
    sq = [(abr, abi)]
    for _ in range(nbits - 1):
        sq.append(_cmul(*sq[-1], *sq[-1]))
    j = np.arange(t_len + 1)
    pr = jnp.ones((g_n, t_len + 1, n_st), F32)
    pi = jnp.zeros((g_n, t_len + 1, n_st), F32)
    for b in range(nbits):
        bit = jnp.asarray(((j >> b) & 1).astype(bool))[None, :, None]
        mr, mi = _cmul(pr, pi, sq[b][0][:, None, :], sq[b][1][:, None, :])
        pr = jnp.where(bit, mr, pr)
        pi = jnp.where(bit, mi, pi)

    prj, pij = pr[:, :, None, :], pi[:, :, None, :]
    cp_re = c_re[:, None] * prj - c_im[:, None] * pij
    cp_im = c_re[:, None] * pij + c_im[:, None] * prj
    cp = jnp.concatenate([cp_re, -cp_im], axis=-1).reshape(g_n, (t_len + 1) * SSM_GROUP, 2 * n_st)

    prr, pir = prj[:, t_len - 1::-1], pij[:, t_len - 1::-1]
    e_re = prr * bbt_re[:, None] - pir * bbt_im[:, None]
    e_im = prr * bbt_im[:, None] + pir * bbt_re[:, None]
    e = jnp.concatenate([e_re, e_im], axis=-1).reshape(g_n, t_len * SSM_GROUP, 2 * n_st)

    lv_r, lv_i = [pr[:, t_len]], [pi[:, t_len]]
    for _ in range(levels - 1):
        r2, i2 = _cmul(lv_r[-1], lv_i[-1], lv_r[-1], lv_i[-1])
        lv_r.append(r2)
        lv_i.append(i2)
    ar = jnp.stack(lv_r, axis=1)
    ai = jnp.stack(lv_i, axis=1)
    a1 = jnp.concatenate([ar, ar], axis=-1)
    a2 = jnp.concatenate([-ai, ai], axis=-1)
    return cp, jnp.concatenate([bbt_re, bbt_im], axis=-1), e.astype(BF16), a1, a2


def _s5_kernel(*refs, t_len):
    gpt = V7X_LANES // SSM_GROUP
    (u_ref, cp_ref, bbt_ref, e_ref, a1_ref, a2_ref, y_ref,
     rep_ref, master_ref, ebd_ref, fbd_ref, tm_ref) = refs
    n_chunks = u_ref.shape[0] // t_len
    ct = t_len * SSM_GROUP
    n2 = cp_ref.shape[2]
    pair = 2 * V7X_LANES

    @pl.when(pl.program_id(0) == 0)
    def _():
        ebd_ref[...] = jnp.zeros(ebd_ref.shape, BF16)
        fbd_ref[...] = jnp.zeros(fbd_ref.shape, BF16)
        r = lax.broadcasted_iota(I32, rep_ref.shape, 0)
        q = lax.broadcasted_iota(I32, rep_ref.shape, 1)
        same_lag = (r // SSM_GROUP) == (q // V7X_LANES)
        same_ch = (r % SSM_GROUP) == (q % SSM_GROUP)
        rep_ref[...] = jnp.where(same_lag, jnp.where(same_ch, 1.0, 0.0), 0.0).astype(BF16)

    kt8 = jnp.concatenate(
        [lax.dot_general(bbt_ref[k], cp_ref[k, :ct, :], _NT, precision=lax.Precision.HIGHEST,
                         preferred_element_type=F32) for k in range(gpt)], axis=0)
    dall = jnp.dot(kt8.astype(BF16), rep_ref[...], preferred_element_type=F32)
    rg = lax.broadcasted_iota(I32, dall.shape, 0) // SSM_GROUP
    lg = (lax.broadcasted_iota(I32, dall.shape, 1) % V7X_LANES) // SSM_GROUP
    dall = jnp.where(rg == lg, dall, 0.0).astype(BF16)
    for b in range(t_len - 1):
        r0 = (t_len - 2 - b) * V7X_LANES
        master_ref[b * V7X_LANES:(b + 1) * V7X_LANES, :] = dall[:, r0:r0 + pair]
    master_ref[(t_len - 1) * V7X_LANES:, :] = jnp.concatenate(
        [jnp.zeros((V7X_LANES, V7X_LANES), BF16), dall[:, :V7X_LANES]], axis=1)
    for s in range(t_len):
        for k in range(gpt):
            rows = slice(s * V7X_LANES + k * SSM_GROUP, s * V7X_LANES + (k + 1) * SSM_GROUP)
            cols = slice(k * n2, (k + 1) * n2)
            ebd_ref[rows, cols] = e_ref[k, s * SSM_GROUP:(s + 1) * SSM_GROUP, :]
            fbd_ref[rows, cols] = cp_ref[k, (s + 1) * SSM_GROUP:(s + 2) * SSM_GROUP, :].astype(BF16)

    tm_ref[...] = u_ref[...].astype(F32)
    u = jnp.concatenate([tm_ref[pl.ds(s, n_chunks, stride=t_len), :].astype(BF16) for s in range(t_len)],
                        axis=1)
    hend = jnp.dot(u, ebd_ref[...], preferred_element_type=F32)
    row = lax.broadcasted_iota(I32, (hend.shape[0], n2), 0)
    carries = []
    for k in range(gpt):
        p = hend[:, k * n2:(k + 1) * n2]
        for lv in range(a1_ref.shape[1]):
            sh = 1 << lv
            a1 = a1_ref[k, lv:lv + 1, :]
            a2 = a2_ref[k, lv:lv + 1, :]
            ps = jnp.where(row >= sh, pltpu.roll(p, sh, axis=0), 0.0)
            p = p + a1 * ps + a2 * pltpu.roll(ps, n2 // 2, axis=1)
        carries.append(jnp.where(row >= 1, pltpu.roll(p, 1, axis=0), 0.0).astype(BF16))
    carry = jnp.concatenate(carries, axis=1)

    for tp in range(t_len // 2):
        kdim = (tp + 1) * pair
        y2 = jnp.dot(u[:, :kdim], master_ref[(t_len - 2 - 2 * tp) * V7X_LANES:, :],
                     preferred_element_type=F32)
        y2 = y2 + lax.dot_general(carry, fbd_ref[tp * pair:(tp + 1) * pair, :], _NT,
                                  preferred_element_type=F32)
        tm_ref[(2 * tp) * n_chunks:(2 * tp + 1) * n_chunks, :] = y2[:, :V7X_LANES]
        tm_ref[(2 * tp + 1) * n_chunks:(2 * tp + 2) * n_chunks, :] = y2[:, V7X_LANES:]

    for c in range(n_chunks):
        y_ref[c * t_len:(c + 1) * t_len, :] = tm_ref[pl.ds(c, t_len, stride=n_chunks), :]


def _s5(proj, ssm_w, cp, bbt, e, a1, a2):
    s, width = proj.shape
    t_len = S5_CHUNK
    n_chunks = s // t_len
    gpt = V7X_LANES // SSM_GROUP
    ct = t_len * SSM_GROUP
    n2 = e.shape[-1]
    levels = a1.shape[1]
    assert n_chunks == 1 << levels and n2 == V7X_LANES
    grp = lambda rows: pl.BlockSpec((gpt, rows, n2), lambda g: (g, 0, 0))
    col = pl.BlockSpec((s, V7X_LANES), lambda g: (0, g))
    return pl.pallas_call(
        functools.partial(_s5_kernel, t_len=t_len),
        out_shape=jax.ShapeDtypeStruct((s, ssm_w), F32),
        grid=(ssm_w // V7X_LANES,),
        in_specs=[col, grp(ct + SSM_GROUP), grp(SSM_GROUP), grp(ct), grp(levels), grp(levels)],
        out_specs=col,
        scratch_shapes=[
            pltpu.VMEM((ct, t_len * V7X_LANES), BF16),
            pltpu.VMEM((t_len * V7X_LANES, 2 * V7X_LANES), BF16),
            pltpu.VMEM((t_len * V7X_LANES, gpt * n2), BF16),
            pltpu.VMEM((t_len * V7X_LANES, gpt * n2), BF16),
            pltpu.VMEM((s, V7X_LANES), F32),
        ],
        compiler_params=_cparams("arbitrary"),
        name="s5",
    )(proj, cp, bbt, e, a1, a2)


def _s5_post_kernel(y_ref, u_ref, z_ref, d_ref, w_ref, o_ref):
    width = y_ref.shape[1]
    yy = _gelu_tanh(y_ref[...] + d_ref[...] * u_ref[...].astype(F32))
    r = jnp.dot(yy.astype(BF16), w_ref[...], preferred_element_type=F32)
    glu = r[:, :width] * _sigmoid(r[:, width:])
    z = z_ref[...].astype(F32)
    o_ref[...] = (glu * (z * _sigmoid(z))).astype(BF16)


def _s5_post(y, proj_a, d_skip, w_glu, *, tm=512):
    s, width = y.shape
    return pl.pallas_call(
        _s5_post_kernel,
        out_shape=jax.ShapeDtypeStruct((s, width), BF16),
        grid=(s // tm,),
        in_specs=[
            pl.BlockSpec((tm, width), lambda i: (i, 0)),
            pl.BlockSpec((tm, width), lambda i: (i, 0)),
            pl.BlockSpec((tm, width), lambda i: (i, 1)),
            pl.BlockSpec((1, width), lambda i: (0, 0)),
            pl.BlockSpec(w_glu.shape, lambda i: (0, 0)),
        ],
        out_specs=pl.BlockSpec((tm, width), lambda i: (i, 0)),
        compiler_params=_cparams("parallel"),
        name="s5_post",
    )(y, proj_a, proj_a, d_skip, w_glu)


def _key_to_float(key):
    bits = key ^ (lax.shift_right_arithmetic(key, 31) & 0x7FFFFFFF)
    return lax.bitcast_convert_type(bits, F32)


def _dsa_kernel(q_ref, qi_ref, w_ref, z_ref, k_ref, v_ref, ki_ref, o_ref,
                sc_ref, wb_ref, res_ref, cand_ref, cnt_ref, m_ref, l_ref, acc_ref, *, topk):
    tq = DSA_TQ
    halves = tq // V7X_LANES
    cpt = DSA_TK // V7X_LANES
    rep = N_HEADS // N_KV_HEADS
    i = pl.program_id(0)
    n_valid = halves * (i + 1)
    n_it = (n_valid + cpt - 1) // cpt

    for h in range(IDX_HEADS):
        wb_ref[h] = jnp.broadcast_to(w_ref[:, h:h + 1], (tq, V7X_LANES))
    qi = qi_ref[...].reshape(IDX_HEADS * tq, IDX_DIM)

    def score_tile(kt, diag):
        ki = ki_ref[pl.ds(pl.multiple_of(kt * tq, tq), tq), :]
        r = lax.dot_general(qi, ki, _NT, preferred_element_type=F32)
        for half in range(halves):
            lanes = slice(half * V7X_LANES, (half + 1) * V7X_LANES)
            acc = jnp.zeros((tq, V7X_LANES), F32)
            for h in range(IDX_HEADS):
                acc = acc + wb_ref[h] * jnp.maximum(r[h * tq:(h + 1) * tq, lanes], 0.0)
            if diag:
                col = lax.broadcasted_iota(I32, acc.shape, 1) + half * V7X_LANES
                rowi = lax.broadcasted_iota(I32, acc.shape, 0)
                acc = jnp.where(col <= rowi, acc, -jnp.inf)
            sc_ref[halves * kt + half] = acc

    def score_body(kt, c):
        score_tile(kt, False)
        return c

    lax.fori_loop(0, i, score_body, 0)
    score_tile(i, True)

    for j in range(cpt - halves):
        @pl.when(n_valid + j < n_it * cpt)
        def _():
            sc_ref[n_valid + j] = jnp.full((tq, V7X_LANES), -jnp.inf, F32)

    res_ref[...] = jnp.full(res_ref.shape, INT_MIN, I32)

    def pass_body(p, c):
        bit = lax.shift_left(jnp.int32(1), 31 - p)
        cand_ref[...] = _key_to_float(res_ref[...] + bit)
        cnts = []
        for rc in range(tq // SEARCH_ROWS):
            rows = slice(rc * SEARCH_ROWS, (rc + 1) * SEARCH_ROWS)
            cand = cand_ref[rows, :]

            def it_body(it, cnt, rows=rows, cand=cand):
                for j in range(cpt):
                    cnt = cnt + jnp.where(sc_ref[cpt * it + j, rows, :] >= cand, 1.0, 0.0)
                return cnt

            cnts.append(lax.fori_loop(0, n_it, it_body, jnp.zeros((SEARCH_ROWS, V7X_LANES), F32)))
        tot = jnp.sum(jnp.concatenate(cnts, axis=0), axis=1, keepdims=True)
        ok = tot >= float(topk)
        res = res_ref[...]
        res_ref[...] = jnp.where(ok, res + bit, res)
        cnt_ref[...] = jnp.where(ok, tot, cnt_ref[...])
        return c

    cnt_ref[...] = (jnp.full(cnt_ref.shape, DSA_TK, I32) * n_it).astype(F32)
    lax.fori_loop(0, 32, pass_body, 0)
    res = res_ref[...]
    tau = _key_to_float(jnp.maximum(res, KEY_NEG_INF + 1))
    cand_ref[...] = tau

    excess = jnp.where(res > KEY_NEG_INF, cnt_ref[...] - float(topk), 0.0).astype(I32)
    lane = lax.broadcasted_iota(I32, (tq, V7X_LANES), 1)
    n_chunks = n_it * cpt

    def drop_body(r, c):
        active = excess > r

        def min_body(ch, mv):
            sc = sc_ref[ch]
            return jnp.minimum(mv, jnp.where(sc >= tau, sc, jnp.inf))

        mv = lax.fori_loop(0, n_chunks, min_body, jnp.full((tq, V7X_LANES), jnp.inf, F32))
        mval = jnp.min(mv, axis=1, keepdims=True)

        def idx_body(ch, ix):
            col = (lane + ch * V7X_LANES).astype(F32)
            return jnp.maximum(ix, jnp.where(sc_ref[ch] == mval, col, -1.0))

        ix = lax.fori_loop(0, n_chunks, idx_body, jnp.full((tq, V7X_LANES), -1.0, F32))
        idx = jnp.max(ix, axis=1, keepdims=True)

        def drop_chunk(ch, c2):
            col = (lane + ch * V7X_LANES).astype(F32)
            sc = sc_ref[ch]
            sc_ref[ch] = jnp.where(active, jnp.where(col == idx, -jnp.inf, sc), sc)
            return c2

        lax.fori_loop(0, n_chunks, drop_chunk, 0)
        return c

    lax.fori_loop(0, jnp.max(excess), drop_body, 0)

    m_ref[...] = jnp.full(m_ref.shape, NEG_INIT, F32)
    l_ref[...] = jnp.zeros(l_ref.shape, F32)
    acc_ref[...] = jnp.zeros(acc_ref.shape, F32)
    ones = jnp.ones((DSA_TK, HEAD_DIM), BF16)

    def attn_body(it, c):
        tau = cand_ref[...]
        neg = jnp.concatenate(
            [jnp.where(sc_ref[cpt * it + j] >= tau, 0.0, -jnp.inf) for j in range(cpt)], axis=1)
        ks = pl.ds(pl.multiple_of(it * DSA_TK, DSA_TK), DSA_TK)
        for g in range(N_KV_HEADS):
            hs = slice(g * rep, (g + 1) * rep)
            cols = slice(g * HEAD_DIM, (g + 1) * HEAD_DIM)
            qg = q_ref[hs].reshape(rep * tq, HEAD_DIM)
            s = lax.dot_general(qg, k_ref[ks, cols], _NT, preferred_element_type=F32)
            s = s.reshape(rep, tq, DSA_TK) + neg[None]
            m_prev = m_ref[hs]
            m_new = jnp.maximum(m_prev, jnp.max(s, axis=2, keepdims=True))
            alpha = jnp.exp2(m_prev - m_new)
            p = jnp.concatenate(
                [jnp.exp2(s[..., j * V7X_LANES:(j + 1) * V7X_LANES] - m_new) for j in range(cpt)], axis=2)
            v_aug = jnp.concatenate([v_ref[ks, cols], ones], axis=1)
            pv = jnp.dot(p.reshape(rep * tq, DSA_TK).astype(BF16), v_aug,
                         preferred_element_type=F32).reshape(rep, tq, 2 * HEAD_DIM)
            acc_ref[hs] = alpha * acc_ref[hs] + pv[..., :HEAD_DIM]
            l_ref[hs] = alpha * l_ref[hs] + pv[..., HEAD_DIM:]
            m_ref[hs] = m_new
        return c

    lax.fori_loop(0, n_it, attn_body, 0)

    for h in range(N_HEADS):
        cols = slice(h * HEAD_DIM, (h + 1) * HEAD_DIM)
        z = z_ref[:, cols].astype(F32)
        o_ref[:, cols] = ((acc_ref[h] / l_ref[h]) * (z * _sigmoid(z))).astype(BF16)


def _dsa(q, qi, w_idx, proj, z_block, k_block, kidx):
    s = kidx.shape[0]
    tq = DSA_TQ
    width = N_HEADS * HEAD_DIM
    kvw = N_KV_HEADS * HEAD_DIM
    topk = min(TOPK_MAX, s // 4)
    return pl.pallas_call(
        functools.partial(_dsa_kernel, topk=topk),
        out_shape=jax.ShapeDtypeStruct((s, width), BF16),
        grid=(s // tq,),
        in_specs=[
            pl.BlockSpec((N_HEADS, tq, HEAD_DIM), lambda i: (0, i, 0)),
            pl.BlockSpec((IDX_HEADS, tq, IDX_DIM), lambda i: (0, i, 0)),
            pl.BlockSpec((tq, IDX_HEADS), lambda i: (i, 0)),
            pl.BlockSpec((tq, width), lambda i: (i, z_block)),
            pl.BlockSpec((s, kvw), lambda i: (0, k_block)),
            pl.BlockSpec((s, kvw), lambda i: (0, k_block + 1)),
            pl.BlockSpec((s, IDX_DIM), lambda i: (0, 0)),
        ],
        out_specs=pl.BlockSpec((tq, width), lambda i: (i, 0)),
        scratch_shapes=[
            pltpu.VMEM((s // V7X_LANES, tq, V7X_LANES), F32),
            pltpu.VMEM((IDX_HEADS, tq, V7X_LANES), F32),
            pltpu.VMEM((tq, V7X_LANES), I32),
            pltpu.VMEM((tq, V7X_LANES), F32),
            pltpu.VMEM((tq, V7X_LANES), F32),
            pltpu.VMEM((N_HEADS, tq, V7X_LANES), F32),
            pltpu.VMEM((N_HEADS, tq, V7X_LANES), F32),
            pltpu.VMEM((N_HEADS, tq, HEAD_DIM), F32),
        ],
        compiler_params=_cparams("arbitrary"),
        name="dsa",
    )(q, qi, w_idx, proj, proj, proj, kidx)


def _merge_kernel(ys_ref, ya_ref, ws_ref, wa_ref, gs_ref, ga_ref, o_ref):
    o_s = jnp.dot(ys_ref[...], ws_ref[...], preferred_element_type=F32)
    o_a = jnp.dot(ya_ref[...], wa_ref[...], preferred_element_type=F32)
    o_ref[...] = (gs_ref[...].astype(F32) * o_s + ga_ref[...].astype(F32) * o_a).astype(BF16)


def _merge(y_ssm, y_attn, w_so, w_ao, proj, gate_col, *, tm=1024, tn=512):
    s, kdim = y_ssm.shape
    d = w_so.shape[1]
    nj = d // tn
    g0 = gate_col // tn
    return pl.pallas_call(
        _merge_kernel,
        out_shape=jax.ShapeDtypeStruct((s, d), BF16),
        grid=(s // tm, nj),
        in_specs=[
            pl.BlockSpec((tm, kdim), lambda i, j: (i, 0)),
            pl.BlockSpec((tm, kdim), lambda i, j: (i, 0)),
            pl.BlockSpec((kdim, tn), lambda i, j: (0, j)),
            pl.BlockSpec((kdim, tn), lambda i, j: (0, j)),
            pl.BlockSpec((tm, tn), lambda i, j: (i, g0 + j)),
            pl.BlockSpec((tm, tn), lambda i, j: (i, g0 + nj + j)),
        ],
        out_specs=pl.BlockSpec((tm, tn), lambda i, j: (i, j)),
        compiler_params=_cparams("parallel", "parallel"),
        name="merge",
    )(y_ssm, y_attn, w_so, w_ao, proj, proj)


def _oproj_kernel(m_ref, w_ref, x_ref, o_ref):
    o_ref[...] = x_ref[...] + jnp.dot(m_ref[...], w_ref[...], preferred_element_type=F32)


def _oproj(merged, w_o, x, *, tm=1024, tn=512):
    s, d = x.shape
    return pl.pallas_call(
        _oproj_kernel,
        out_shape=jax.ShapeDtypeStruct((s, d), F32),
        grid=(s // tm, d // tn),
        in_specs=[
            pl.BlockSpec((tm, d), lambda i, j: (i, 0)),
            pl.BlockSpec((d, tn), lambda i, j: (0, j)),
            pl.BlockSpec((tm, tn), lambda i, j: (i, j)),
        ],
        out_specs=pl.BlockSpec((tm, tn), lambda i, j: (i, j)),
        compiler_params=_cparams("parallel", "parallel"),
        name="oproj",
    )(merged, w_o, x)


def _final_kernel(x_ref, p_ref, wp_ref, wg_ref, gple_ref, gpost_ref, gfin_ref, o_ref):
    x1 = x_ref[...]
    e = _rms(jnp.dot(p_ref[...].astype(BF16), wp_ref[...], preferred_element_type=F32), gpost_ref[...])
    xn = _rms(x1, gple_ref[...]).astype(BF16)
    gate = _sigmoid(jnp.dot(xn, wg_ref[...], preferred_element_type=F32))
    o_ref[...] = _rms(x1 + gate * e, gfin_ref[...])


def _final(x1, p, w_ple, w_gate, g_ple, g_post, g_final, *, tm=256):
    s, d = x1.shape
    pd = p.shape[1]
    vec = pl.BlockSpec((1, d), lambda i: (0, 0))
    return pl.pallas_call(
        _final_kernel,
        out_shape=jax.ShapeDtypeStruct((s, d), F32),
        grid=(s // tm,),
        in_specs=[
            pl.BlockSpec((tm, d), lambda i: (i, 0)),
            pl.BlockSpec((tm, pd), lambda i: (i, 0)),
            pl.BlockSpec((pd, d), lambda i: (0, 0)),
            pl.BlockSpec((d, d), lambda i: (0, 0)),
            vec, vec, vec,
        ],
        out_specs=pl.BlockSpec((tm, d), lambda i: (i, 0)),
        compiler_params=_cparams("parallel"),
        name="final",
    )(x1, p, w_ple, w_gate, g_ple, g_post, g_final)


def _layer(x, p, g_mix, w_in, g_q, w_uq, w_uq_idx, g_kidx, a_re, a_im, log_dt, b_re, b_im,
           c_re, c_im, d_skip, w_glu, w_ssm_out, w_attn_out, w_o, g_ple, w_ple_gate, w_ple,
           g_ple_post):
    s, d = x.shape
    ssm_w = d // 2
    attn_w = N_HEADS * HEAD_DIM
    kvw = N_KV_HEADS * HEAD_DIM
    n_a = 2 * ssm_w + Q_LORA_RANK + 2 * kvw + attn_w
    n_i = IDX_DIM + IDX_HEADS
    w_i = jnp.pad(w_in[:, n_a:n_a + n_i], ((0, 0), (0, V7X_LANES - n_i))).astype(BF16)
    w_g = w_in[:, n_a + n_i:]
    row = lambda v: v.reshape(1, -1)

    proj, c_q, kidx, w_idx = _proj(x, row(g_mix), w_in, n_a, 2 * ssm_w, w_g, w_i, row(g_kidx))

    n_chunks = s // S5_CHUNK
    cp, bbt, e, a1, a2 = _s5_operators(a_re, a_im, log_dt, b_re, b_im, c_re, c_im,
                                       levels=n_chunks.bit_length() - 1)
    y = _s5(proj, ssm_w, cp, bbt, e, a1, a2)
    y_ssm = _s5_post(y, proj, row(d_skip), w_glu.astype(BF16))

    w_q = jnp.concatenate([w_uq, w_uq_idx], axis=1).astype(BF16)
    q, qi = _qproj(c_q, 0, row(g_q), w_q)
    k_off = 2 * ssm_w + Q_LORA_RANK
    z_block = (k_off + 2 * kvw) // attn_w
    y_attn = _dsa(q, qi, w_idx, proj, z_block, k_off // kvw, kidx)

    merged = _merge(y_ssm, y_attn, w_ssm_out.astype(BF16), w_attn_out.astype(BF16), proj, n_a)
    x1 = _oproj(merged, w_o.astype(BF16), x)
    return x1, (p, w_ple.astype(BF16), w_ple_gate.astype(BF16), row(g_ple), row(g_ple_post))


def kernel(x, p, g_mix, w_in, g_q, w_uq, w_uq_idx, g_kidx, a_re, a_im, log_dt, b_re, b_im, c_re, c_im,
           d_skip, w_glu, w_ssm_out, w_attn_out, w_o, g_ple, w_ple_gate, w_ple, g_ple_post, g_final):
    bsz, seq, d = x.shape
    depth = p.shape[0]
    assert bsz == 1 and depth == 1, "kernel is specialised to one sequence and one layer"
    x2 = x.reshape(seq, d)
    x1, (p0, wp, wg, gple, gpost) = _layer(
        x2, p[0, 0], g_mix[0], w_in[0], g_q[0], w_uq[0], w_uq_idx[0], g_kidx[0], a_re[0], a_im[0],
        log_dt[0], b_re[0], b_im[0], c_re[0], c_im[0], d_skip[0], w_glu[0], w_ssm_out[0],
        w_attn_out[0], w_o[0], g_ple[0], w_ple_gate[0], w_ple[0], g_ple_post[0])
    out = _final(x1, p0, wp, wg, gple, gpost, g_final.reshape(1, -1))
    return out.reshape(bsz, seq, d)
```

```python
import functools
import math

import numpy as np
import jax
import jax.numpy as jnp
from jax import lax
from jax.experimental import pallas as pl
from jax.experimental.pallas import tpu as pltpu

F32 = jnp.float32
BF16 = jnp.bfloat16
I32 = jnp.int32

EPS = 1e-6
SSM_GROUP = 16
SSM_STATE = 64
N_HEADS = 8
N_KV_HEADS = 2
HEAD_DIM = 128
Q_LORA_RANK = 512
IDX_HEADS = 16
IDX_DIM = 64
TOPK_MAX = 256

V7X_LANES = 128
V7X_VMEM_LIMIT = 56 * 1024 * 1024

S5_CHUNK = 32
DSA_TQ = 256
DSA_TK = 512
SEARCH_ROWS = 128

INT_MIN = -(2 ** 31)
KEY_NEG_INF = -2139095041
NEG_INIT = -1e30

_NT = (((1,), (1,)), ((), ()))


def _cparams(*sem):
    return pltpu.CompilerParams(dimension_semantics=sem, vmem_limit_bytes=V7X_VMEM_LIMIT)


def _rms(x, g):
    ms = jnp.mean(x * x, axis=-1, keepdims=True)
    return x * lax.rsqrt(ms + EPS) * g


def _sigmoid(x):
    return 1.0 / (1.0 + jnp.exp(-x))


def _gelu_tanh(x):
    c = math.sqrt(2.0 / math.pi)
    return 0.5 * x * (1.0 + jnp.tanh(c * (x + 0.044715 * (x * x * x))))


def _proj_kernel(x_ref, g_ref, wa_ref, wg_ref, wi_ref, gk_ref, ob_ref, cq_ref, ki_ref, widx_ref, h_ref,
                 *, n_a_tiles, cq_tile, n_cq_tiles, w_scale):
    j = pl.program_id(1)

    @pl.when(j == 0)
    def _():
        h = _rms(x_ref[...], g_ref[...]).astype(BF16)
        h_ref[...] = h
        r = jnp.dot(h, wi_ref[...], preferred_element_type=F32)
        ki_ref[...] = _rms(r[:, :IDX_DIM], gk_ref[...]).astype(BF16)
        widx_ref[...] = r[:, IDX_DIM:IDX_DIM + IDX_HEADS] * w_scale

    @pl.when(j < n_a_tiles)
    def _():
        acc = jnp.dot(h_ref[...], wa_ref[...].astype(BF16), preferred_element_type=F32)
        ob_ref[...] = acc.astype(BF16)

        @pl.when(jnp.logical_and(j >= cq_tile, j < cq_tile + n_cq_tiles))
        def _():
            cq_ref[...] = acc

    @pl.when(j >= n_a_tiles)
    def _():
        acc = jnp.dot(h_ref[...], wg_ref[...].astype(BF16), preferred_element_type=F32)
        ob_ref[...] = _sigmoid(acc).astype(BF16)


def _proj(x, g, w_in, n_a, cq_col, w_g, w_i, gk, *, tm=2048, tn=256):
    s, d = x.shape
    tm = min(tm, s)
    n_g = w_g.shape[1]
    n_a_tiles = n_a // tn
    cq_tile, n_cq_tiles = cq_col // tn, Q_LORA_RANK // tn
    return pl.pallas_call(
        functools.partial(_proj_kernel, n_a_tiles=n_a_tiles, cq_tile=cq_tile, n_cq_tiles=n_cq_tiles,
                          w_scale=(IDX_HEADS ** -0.5) * (IDX_DIM ** -0.5)),
        out_shape=(jax.ShapeDtypeStruct((s, n_a + n_g), BF16),
                   jax.ShapeDtypeStruct((s, Q_LORA_RANK), F32),
                   jax.ShapeDtypeStruct((s, IDX_DIM), BF16),
                   jax.ShapeDtypeStruct((s, IDX_HEADS), F32)),
        grid=(s // tm, (n_a + n_g) // tn),
        in_specs=[
            pl.BlockSpec((tm, d), lambda i, j: (i, 0), pipeline_mode=pl.Buffered(1)),
            pl.BlockSpec((1, d), lambda i, j: (0, 0)),
            pl.BlockSpec((None, d, tn), lambda i, j: (0, 0, jnp.minimum(j, n_a_tiles - 1))),
            pl.BlockSpec((d, tn), lambda i, j: (0, jnp.maximum(j - n_a_tiles, 0))),
            pl.BlockSpec((d, V7X_LANES), lambda i, j: (0, 0)),
            pl.BlockSpec((1, IDX_DIM), lambda i, j: (0, 0)),
        ],
        out_specs=(pl.BlockSpec((tm, tn), lambda i, j: (i, j)),
                   pl.BlockSpec((tm, tn), lambda i, j: (i, jnp.clip(j - cq_tile, 0, n_cq_tiles - 1))),
                   pl.BlockSpec((tm, IDX_DIM), lambda i, j: (i, 0)),
                   pl.BlockSpec((tm, IDX_HEADS), lambda i, j: (i, 0))),
        scratch_shapes=[pltpu.VMEM((tm, d), BF16)],
        compiler_params=_cparams("parallel", "arbitrary"),
        name="proj",
    )(x, g, w_in, w_g, w_i, gk)


def _qproj_kernel(c_ref, g_ref, w_ref, q_ref, qi_ref, *, q_scale):
    cq = _rms(c_ref[...], g_ref[...]).astype(BF16)
    r = jnp.dot(cq, w_ref[...], preferred_element_type=F32)
    for h in range(N_HEADS):
        q_ref[h] = (r[:, h * HEAD_DIM:(h + 1) * HEAD_DIM] * q_scale).astype(BF16)
    base = N_HEADS * HEAD_DIM
    for h in range(IDX_HEADS):
        qi_ref[h] = r[:, base + h * IDX_DIM: base + (h + 1) * IDX_DIM].astype(BF16)


def _qproj(proj_a, cq_block, g, w, *, tm=512):
    s = proj_a.shape[0]
    return pl.pallas_call(
        functools.partial(_qproj_kernel, q_scale=HEAD_DIM ** -0.5 * math.log2(math.e)),
        out_shape=(jax.ShapeDtypeStruct((N_HEADS, s, HEAD_DIM), BF16),
                   jax.ShapeDtypeStruct((IDX_HEADS, s, IDX_DIM), BF16)),
        grid=(s // tm,),
        in_specs=[
            pl.BlockSpec((tm, Q_LORA_RANK), lambda i: (i, cq_block)),
            pl.BlockSpec((1, Q_LORA_RANK), lambda i: (0, 0)),
            pl.BlockSpec(w.shape, lambda i: (0, 0)),
        ],
        out_specs=(pl.BlockSpec((N_HEADS, tm, HEAD_DIM), lambda i: (0, i, 0)),
                   pl.BlockSpec((IDX_HEADS, tm, IDX_DIM), lambda i: (0, i, 0))),
        compiler_params=_cparams("parallel"),
        name="qproj",
    )(proj_a, g, w)


def _cmul(ar, ai, br, bi):
    return ar * br - ai * bi, ar * bi + ai * br


def _s5_operators(a_re, a_im, log_dt, b_re, b_im, c_re, c_im, levels):
    t_len = S5_CHUNK
    g_n, n_st = a_re.shape
    dt = jnp.exp(log_dt)[:, None]
    mag = jnp.exp(dt * a_re)
    abr = mag * jnp.cos(dt * a_im)
    abi = mag * jnp.sin(dt * a_im)
    den = a_re * a_re + a_im * a_im
    nr = abr - 1.0
    f_re = (nr * a_re + abi * a_im) / den
    f_im = (abi * a_re - nr * a_im) / den
    bt_re = b_re.transpose(0, 2, 1)
    bt_im = b_im.transpose(0, 2, 1)
    bbt_re = f_re[:, None, :] * bt_re - f_im[:, None, :] * bt_im
    bbt_im = f_re[:, None, :] * bt_im + f_im[:, None, :] * bt_re

    nbits = t_len.bit_length()
    sq = [(abr, abi)]
    for _ in range(nbits - 1):
        sq.append(_cmul(*sq[-1], *sq[-1]))
    j = np.arange(t_len + 1)
    pr = jnp.ones((g_n, t_len + 1, n_st), F32)
    pi = jnp.zeros((g_n, t_len + 1, n_st), F32)
    for b in range(nbits):
        bit = jnp.asarray(((j >> b) & 1).astype(bool))[None, :, None]
        mr, mi = _cmul(pr, pi, sq[b][0][:, None, :], sq[b][1][:, None, :])
        pr = jnp.where(bit, mr, pr)
        pi = jnp.where(bit, mi, pi)

    prj, pij = pr[:, :, None, :], pi[:, :, None, :]
    cp_re = c_re[:, None] * prj - c_im[:, None] * pij
    cp_im = c_re[:, None] * pij + c_im[:, None] * prj
    cp = jnp.concatenate([cp_re, -cp_im], axis=-1).reshape(g_n, (t_len + 1) * SSM_GROUP, 2 * n_st)

    prr, pir = prj[:, t_len - 1::-1], pij[:, t_len - 1::-1]
    e_re = prr * bbt_re[:, None] - pir * bbt_im[:, None]
    e_im = prr * bbt_im[:, None] + pir * bbt_re[:, None]
    e = jnp.concatenate([e_re, e_im], axis=-1).reshape(g_n, t_len * SSM_GROUP, 2 * n_st)

    lv_r, lv_i = [pr[:, t_len]], [pi[:, t_len]]
    for _ in range(levels - 1):
        r2, i2 = _cmul(lv_r[-1], lv_i[-1], lv_r[-1], lv_i[-1])
        lv_r.append(r2)
        lv_i.append(i2)
    ar = jnp.stack(lv_r, axis=1)
    ai = jnp.stack(lv_i, axis=1)
    a1 = jnp.concatenate([ar, ar], axis=-1)
    a2 = jnp.concatenate([-ai, ai], axis=-1)
    return cp, jnp.concatenate([bbt_re, bbt_im], axis=-1), e.astype(BF16), a1, a2


def _s5_kernel(*refs, t_len):
    gpt = V7X_LANES // SSM_GROUP
    (u_ref, cp_ref, bbt_ref, e_ref, a1_ref, a2_ref, y_ref,
     rep_ref, master_ref, ebd_ref, fbd_ref, tm_ref) = refs
    n_chunks = u_ref.shape[0] // t_len
    ct = t_len * SSM_GROUP
    n2 = cp_ref.shape[2]
    pair = 2 * V7X_LANES

    @pl.when(pl.program_id(0) == 0)
    def _():
        ebd_ref[...] = jnp.zeros(ebd_ref.shape, BF16)
        fbd_ref[...] = jnp.zeros(fbd_ref.shape, BF16)
        r = lax.broadcasted_iota(I32, rep_ref.shape, 0)
        q = lax.broadcasted_iota(I32, rep_ref.shape, 1)
        same_lag = (r // SSM_GROUP) == (q // V7X_LANES)
        same_ch = (r % SSM_GROUP) == (q % SSM_GROUP)
        rep_ref[...] = jnp.where(same_lag, jnp.where(same_ch, 1.0, 0.0), 0.0).astype(BF16)

    kt8 = jnp.concatenate(
        [lax.dot_general(bbt_ref[k], cp_ref[k, :ct, :], _NT, precision=lax.Precision.HIGHEST,
                         preferred_element_type=F32) for k in range(gpt)], axis=0)
    dall = jnp.dot(kt8.astype(BF16), rep_ref[...], preferred_element_type=F32)
    rg = lax.broadcasted_iota(I32, dall.shape, 0) // SSM_GROUP
    lg = (lax.broadcasted_iota(I32, dall.shape, 1) % V7X_LANES) // SSM_GROUP
    dall = jnp.where(rg == lg, dall, 0.0).astype(BF16)
    for b in range(t_len - 1):
        r0 = (t_len - 2 - b) * V7X_LANES
        master_ref[b * V7X_LANES:(b + 1) * V7X_LANES, :] = dall[:, r0:r0 + pair]
    master_ref[(t_len - 1) * V7X_LANES:, :] = jnp.concatenate(
        [jnp.zeros((V7X_LANES, V7X_LANES), BF16), dall[:, :V7X_LANES]], axis=1)
    for s in range(t_len):
        for k in range(gpt):
            rows = slice(s * V7X_LANES + k * SSM_GROUP, s * V7X_LANES + (k + 1) * SSM_GROUP)
            cols = slice(k * n2, (k + 1) * n2)
            ebd_ref[rows, cols] = e_ref[k, s * SSM_GROUP:(s + 1) * SSM_GROUP, :]
            fbd_ref[rows, cols] = cp_ref[k, (s + 1) * SSM_GROUP:(s + 2) * SSM_GROUP, :].astype(BF16)

    tm_ref[...] = u_ref[...].astype(F32)
    u = jnp.concatenate([tm_ref[pl.ds(s, n_chunks, stride=t_len), :].astype(BF16) for s in range(t_len)],
                        axis=1)
    hend = jnp.dot(u, ebd_ref[...], preferred_element_type=F32)
    row = lax.broadcasted_iota(I32, (hend.shape[0], n2), 0)
    carries = []
    for k in range(gpt):
        p = hend[:, k * n2:(k + 1) * n2]
        for lv in range(a1_ref.shape[1]):
            sh = 1 << lv
            a1 = a1_ref[k, lv:lv + 1, :]
            a2 = a2_ref[k, lv:lv + 1, :]
            ps = jnp.where(row >= sh, pltpu.roll(p, sh, axis=0), 0.0)
            p = p + a1 * ps + a2 * pltpu.roll(ps, n2 // 2, axis=1)
        carries.append(jnp.where(row >= 1, pltpu.roll(p, 1, axis=0), 0.0).astype(BF16))
    carry = jnp.concatenate(carries, axis=1)

    for tp in range(t_len // 2):
        kdim = (tp + 1) * pair
        y2 = jnp.dot(u[:, :kdim], master_ref[(t_len - 2 - 2 * tp) * V7X_LANES:, :],
                     preferred_element_type=F32)
        y2 = y2 + lax.dot_general(carry, fbd_ref[tp * pair:(tp + 1) * pair, :], _NT,
                                  preferred_element_type=F32)
        tm_ref[(2 * tp) * n_chunks:(2 * tp + 1) * n_chunks, :] = y2[:, :V7X_LANES]
        tm_ref[(2 * tp + 1) * n_chunks:(2 * tp + 2) * n_chunks, :] = y2[:, V7X_LANES:]

    for c in range(n_chunks):
        y_ref[c * t_len:(c + 1) * t_len, :] = tm_ref[pl.ds(c, t_len, stride=n_chunks), :]


def _s5(proj, ssm_w, cp, bbt, e, a1, a2):
    s, width = proj.shape
    t_len = S5_CHUNK
    n_chunks = s // t_len
    gpt = V7X_LANES // SSM_GROUP
    ct = t_len * SSM_GROUP
    n2 = e.shape[-1]
    levels = a1.shape[1]
    assert n_chunks == 1 << levels and n2 == V7X_LANES
    grp = lambda rows: pl.BlockSpec((gpt, rows, n2), lambda g: (g, 0, 0))
    col = pl.BlockSpec((s, V7X_LANES), lambda g: (0, g))
    return pl.pallas_call(
        functools.partial(_s5_kernel, t_len=t_len),
        out_shape=jax.ShapeDtypeStruct((s, ssm_w), F32),
        grid=(ssm_w // V7X_LANES,),
        in_specs=[col, grp(ct + SSM_GROUP), grp(SSM_GROUP), grp(ct), grp(levels), grp(levels)],
        out_specs=col,
        scratch_shapes=[
            pltpu.VMEM((ct, t_len * V7X_LANES), BF16),
            pltpu.VMEM((t_len * V7X_LANES, 2 * V7X_LANES), BF16),
            pltpu.VMEM((t_len * V7X_LANES, gpt * n2), BF16),
            pltpu.VMEM((t_len * V7X_LANES, gpt * n2), BF16),
            pltpu.VMEM((s, V7X_LANES), F32),
        ],
        compiler_params=_cparams("arbitrary"),
        name="s5",
    )(proj, cp, bbt, e, a1, a2)


def _s5_post_kernel(y_ref, u_ref, z_ref, d_ref, w_ref, o_ref):
    width = y_ref.shape[1]
    yy = _gelu_tanh(y_ref[...] + d_ref[...] * u_ref[...].astype(F32))
    r = jnp.dot(yy.astype(BF16), w_ref[...], preferred_element_type=F32)
    glu = r[:, :width] * _sigmoid(r[:, width:])
    z = z_ref[...].astype(F32)
    o_ref[...] = (glu * (z * _sigmoid(z))).astype(BF16)


def _s5_post(y, proj_a, d_skip, w_glu, *, tm=512):
    s, width = y.shape
    return pl.pallas_call(
        _s5_post_kernel,
        out_shape=jax.ShapeDtypeStruct((s, width), BF16),
        grid=(s // tm,),
        in_specs=[
            pl.BlockSpec((tm, width), lambda i: (i, 0)),
            pl.BlockSpec((tm, width), lambda i: (i, 0)),
            pl.BlockSpec((tm, width), lambda i: (i, 1)),
            pl.BlockSpec((1, width), lambda i: (0, 0)),
            pl.BlockSpec(w_glu.shape, lambda i: (0, 0)),
        ],
        out_specs=pl.BlockSpec((tm, width), lambda i: (i, 0)),
        compiler_params=_cparams("parallel"),
        name="s5_post",
    )(y, proj_a, proj_a, d_skip, w_glu)


def _key_to_float(key):
    bits = key ^ (lax.shift_right_arithmetic(key, 31) & 0x7FFFFFFF)
    return lax.bitcast_convert_type(bits, F32)


def _dsa_kernel(q_ref, qi_ref, w_ref, z_ref, k_ref, v_ref, ki_ref, o_ref,
                sc_ref, wb_ref, res_ref, cand_ref, cnt_ref, m_ref, l_ref, acc_ref, *, topk):
    tq = DSA_TQ
    halves = tq // V7X_LANES
    cpt = DSA_TK // V7X_LANES
    rep = N_HEADS // N_KV_HEADS
    i = pl.program_id(0)
    n_valid = halves * (i + 1)
    n_it = (n_valid + cpt - 1) // cpt

    for h in range(IDX_HEADS):
        wb_ref[h] = jnp.broadcast_to(w_ref[:, h:h + 1], (tq, V7X_LANES))
    qi = qi_ref[...].reshape(IDX_HEADS * tq, IDX_DIM)

    def score_tile(kt, diag):
        ki = ki_ref[pl.ds(pl.multiple_of(kt * tq, tq), tq), :]
        r = lax.dot_general(qi, ki, _NT, preferred_element_type=F32)
        for half in range(halves):
            lanes = slice(half * V7X_LANES, (half + 1) * V7X_LANES)
            acc = jnp.zeros((tq, V7X_LANES), F32)
            for h in range(IDX_HEADS):
                acc = acc + wb_ref[h] * jnp.maximum(r[h * tq:(h + 1) * tq, lanes], 0.0)
            if diag:
                col = lax.broadcasted_iota(I32, acc.shape, 1) + half * V7X_LANES
                rowi = lax.broadcasted_iota(I32, acc.shape, 0)
                acc = jnp.where(col <= rowi, acc, -jnp.inf)
            sc_ref[halves * kt + half] = acc

    def score_body(kt, c):
        score_tile(kt, False)
        return c

    lax.fori_loop(0, i, score_body, 0)
    score_tile(i, True)

    for j in range(cpt - halves):
        @pl.when(n_valid + j < n_it * cpt)
        def _():
            sc_ref[n_valid + j] = jnp.full((tq, V7X_LANES), -jnp.inf, F32)

    res_ref[...] = jnp.full(res_ref.shape, INT_MIN, I32)

    def pass_body(p, c):
        bit = lax.shift_left(jnp.int32(1), 31 - p)
        cand_ref[...] = _key_to_float(res_ref[...] + bit)
        cnts = []
        for rc in range(tq // SEARCH_ROWS):
            rows = slice(rc * SEARCH_ROWS, (rc + 1) * SEARCH_ROWS)
            cand = cand_ref[rows, :]

            def it_body(it, cnt, rows=rows, cand=cand):
                for j in range(cpt):
                    cnt = cnt + jnp.where(sc_ref[cpt * it + j, rows, :] >= cand, 1.0, 0.0)
                return cnt

            cnts.append(lax.fori_loop(0, n_it, it_body, jnp.zeros((SEARCH_ROWS, V7X_LANES), F32)))
        tot = jnp.sum(jnp.concatenate(cnts, axis=0), axis=1, keepdims=True)
        ok = tot >= float(topk)
        res = res_ref[...]
        res_ref[...] = jnp.where(ok, res + bit, res)
        cnt_ref[...] = jnp.where(ok, tot, cnt_ref[...])
        return c

    cnt_ref[...] = (jnp.full(cnt_ref.shape, DSA_TK, I32) * n_it).astype(F32)
    lax.fori_loop(0, 32, pass_body, 0)
    res = res_ref[...]
    tau = _key_to_float(jnp.maximum(res, KEY_NEG_INF + 1))
    cand_ref[...] = tau

    excess = jnp.where(res > KEY_NEG_INF, cnt_ref[...] - float(topk), 0.0).astype(I32)
    lane = lax.broadcasted_iota(I32, (tq, V7X_LANES), 1)
    n_chunks = n_it * cpt

    def drop_body(r, c):
        active = excess > r

        def min_body(ch, mv):
            sc = sc_ref[ch]
            return jnp.minimum(mv, jnp.where(sc >= tau, sc, jnp.inf))

        mv = lax.fori_loop(0, n_chunks, min_body, jnp.full((tq, V7X_LANES), jnp.inf, F32))
        mval = jnp.min(mv, axis=1, keepdims=True)

        def idx_body(ch, ix):
            col = (lane + ch * V7X_LANES).astype(F32)
            return jnp.maximum(ix, jnp.where(sc_ref[ch] == mval, col, -1.0))

        ix = lax.fori_loop(0, n_chunks, idx_body, jnp.full((tq, V7X_LANES), -1.0, F32))
        idx = jnp.max(ix, axis=1, keepdims=True)

        def drop_chunk(ch, c2):
            col = (lane + ch * V7X_LANES).astype(F32)
            sc = sc_ref[ch]
            sc_ref[ch] = jnp.where(active, jnp.where(col == idx, -jnp.inf, sc), sc)
            return c2

        lax.fori_loop(0, n_chunks, drop_chunk, 0)
        return c

    lax.fori_loop(0, jnp.max(excess), drop_body, 0)

    m_ref[...] = jnp.full(m_ref.shape, NEG_INIT, F32)
    l_ref[...] = jnp.zeros(l_ref.shape, F32)
    acc_ref[...] = jnp.zeros(acc_ref.shape, F32)
    ones = jnp.ones((DSA_TK, HEAD_DIM), BF16)

    def attn_body(it, c):
        tau = cand_ref[...]
        neg = jnp.concatenate(
            [jnp.where(sc_ref[cpt * it + j] >= tau, 0.0, -jnp.inf) for j in range(cpt)], axis=1)
        ks = pl.ds(pl.multiple_of(it * DSA_TK, DSA_TK), DSA_TK)
        for g in range(N_KV_HEADS):
            hs = slice(g * rep, (g + 1) * rep)
            cols = slice(g * HEAD_DIM, (g + 1) * HEAD_DIM)
            qg = q_ref[hs].reshape(rep * tq, HEAD_DIM)
            s = lax.dot_general(qg, k_ref[ks, cols], _NT, preferred_element_type=F32)
            s = s.reshape(rep, tq, DSA_TK) + neg[None]
            m_prev = m_ref[hs]
            m_new = jnp.maximum(m_prev, jnp.max(s, axis=2, keepdims=True))
            alpha = jnp.exp2(m_prev - m_new)
            p = jnp.concatenate(
                [jnp.exp2(s[..., j * V7X_LANES:(j + 1) * V7X_LANES] - m_new) for j in range(cpt)], axis=2)
            v_aug = jnp.concatenate([v_ref[ks, cols], ones], axis=1)
            pv = jnp.dot(p.reshape(rep * tq, DSA_TK).astype(BF16), v_aug,
                         preferred_element_type=F32).reshape(rep, tq, 2 * HEAD_DIM)
            acc_ref[hs] = alpha * acc_ref[hs] + pv[..., :HEAD_DIM]
            l_ref[hs] = alpha * l_ref[hs] + pv[..., HEAD_DIM:]
            m_ref[hs] = m_new
        return c

    lax.fori_loop(0, n_it, attn_body, 0)

    for h in range(N_HEADS):
        cols = slice(h * HEAD_DIM, (h + 1) * HEAD_DIM)
        z = z_ref[:, cols].astype(F32)
        o_ref[:, cols] = ((acc_ref[h] / l_ref[h]) * (z * _sigmoid(z))).astype(BF16)


def _dsa(q, qi, w_idx, proj, z_block, k_block, kidx):
    s = kidx.shape[0]
    tq = DSA_TQ
    width = N_HEADS * HEAD_DIM
    kvw = N_KV_HEADS * HEAD_DIM
    topk = min(TOPK_MAX, s // 4)
    return pl.pallas_call(
        functools.partial(_dsa_kernel, topk=topk),
        out_shape=jax.ShapeDtypeStruct((s, width), BF16),
        grid=(s // tq,),
        in_specs=[
            pl.BlockSpec((N_HEADS, tq, HEAD_DIM), lambda i: (0, i, 0)),
            pl.BlockSpec((IDX_HEADS, tq, IDX_DIM), lambda i: (0, i, 0)),
            pl.BlockSpec((tq, IDX_HEADS), lambda i: (i, 0)),
            pl.BlockSpec((tq, width), lambda i: (i, z_block)),
            pl.BlockSpec((s, kvw), lambda i: (0, k_block)),
            pl.BlockSpec((s, kvw), lambda i: (0, k_block + 1)),
            pl.BlockSpec((s, IDX_DIM), lambda i: (0, 0)),
        ],
        out_specs=pl.BlockSpec((tq, width), lambda i: (i, 0)),
        scratch_shapes=[
            pltpu.VMEM((s // V7X_LANES, tq, V7X_LANES), F32),
            pltpu.VMEM((IDX_HEADS, tq, V7X_LANES), F32),
            pltpu.VMEM((tq, V7X_LANES), I32),
            pltpu.VMEM((tq, V7X_LANES), F32),
            pltpu.VMEM((tq, V7X_LANES), F32),
            pltpu.VMEM((N_HEADS, tq, V7X_LANES), F32),
            pltpu.VMEM((N_HEADS, tq, V7X_LANES), F32),
            pltpu.VMEM((N_HEADS, tq, HEAD_DIM), F32),
        ],
        compiler_params=_cparams("arbitrary"),
        name="dsa",
    )(q, qi, w_idx, proj, proj, proj, kidx)


def _merge_kernel(ys_ref, ya_ref, ws_ref, wa_ref, gs_ref, ga_ref, o_ref):
    o_s = jnp.dot(ys_ref[...], ws_ref[...], preferred_element_type=F32)
    o_a = jnp.dot(ya_ref[...], wa_ref[...], preferred_element_type=F32)
    o_ref[...] = (gs_ref[...].astype(F32) * o_s + ga_ref[...].astype(F32) * o_a).astype(BF16)


def _merge(y_ssm, y_attn, w_so, w_ao, proj, gate_col, *, tm=1024, tn=512):
    s, kdim = y_ssm.shape
    d = w_so.shape[1]
    nj = d // tn
    g0 = gate_col // tn
    return pl.pallas_call(
        _merge_kernel,
        out_shape=jax.ShapeDtypeStruct((s, d), BF16),
        grid=(s // tm, nj),
        in_specs=[
            pl.BlockSpec((tm, kdim), lambda i, j: (i, 0)),
            pl.BlockSpec((tm, kdim), lambda i, j: (i, 0)),
            pl.BlockSpec((kdim, tn), lambda i, j: (0, j)),
            pl.BlockSpec((kdim, tn), lambda i, j: (0, j)),
            pl.BlockSpec((tm, tn), lambda i, j: (i, g0 + j)),
            pl.BlockSpec((tm, tn), lambda i, j: (i, g0 + nj + j)),
        ],
        out_specs=pl.BlockSpec((tm, tn), lambda i, j: (i, j)),
        compiler_params=_cparams("parallel", "parallel"),
        name="merge",
    )(y_ssm, y_attn, w_so, w_ao, proj, proj)


def _oproj_kernel(m_ref, w_ref, x_ref, o_ref):
    o_ref[...] = x_ref[...] + jnp.dot(m_ref[...], w_ref[...], preferred_element_type=F32)


def _oproj(merged, w_o, x, *, tm=1024, tn=512):
    s, d = x.shape
    return pl.pallas_call(
        _oproj_kernel,
        out_shape=jax.ShapeDtypeStruct((s, d), F32),
        grid=(s // tm, d // tn),
        in_specs=[
            pl.BlockSpec((tm, d), lambda i, j: (i, 0)),
            pl.BlockSpec((d, tn), lambda i, j: (0, j)),
            pl.BlockSpec((tm, tn), lambda i, j: (i, j)),
        ],
        out_specs=pl.BlockSpec((tm, tn), lambda i, j: (i, j)),
        compiler_params=_cparams("parallel", "parallel"),
        name="oproj",
    )(merged, w_o, x)


def _final_kernel(x_ref, p_ref, wp_ref, wg_ref, gple_ref, gpost_ref, gfin_ref, o_ref):
    x1 = x_ref[...]
    e = _rms(jnp.dot(p_ref[...].astype(BF16), wp_ref[...], preferred_element_type=F32), gpost_ref[...])
    xn = _rms(x1, gple_ref[...]).astype(BF16)
    gate = _sigmoid(jnp.dot(xn, wg_ref[...], preferred_element_type=F32))
    o_ref[...] = _rms(x1 + gate * e, gfin_ref[...])


def _final(x1, p, w_ple, w_gate, g_ple, g_post, g_final, *, tm=256):
    s, d = x1.shape
    pd = p.shape[1]
    vec = pl.BlockSpec((1, d), lambda i: (0, 0))
    return pl.pallas_call(
        _final_kernel,
        out_shape=jax.ShapeDtypeStruct((s, d), F32),
        grid=(s // tm,),
        in_specs=[
            pl.BlockSpec((tm, d), lambda i: (i, 0)),
            pl.BlockSpec((tm, pd), lambda i: (i, 0)),
            pl.BlockSpec((pd, d), lambda i: (0, 0)),
            pl.BlockSpec((d, d), lambda i: (0, 0)),
            vec, vec, vec,
        ],
        out_specs=pl.BlockSpec((tm, d), lambda i: (i, 0)),
        compiler_params=_cparams("parallel"),
        name="final",
    )(x1, p, w_ple, w_gate, g_ple, g_post, g_final)


def _layer(x, p, g_mix, w_in, g_q, w_uq, w_uq_idx, g_kidx, a_re, a_im, log_dt, b_re, b_im,
           c_re, c_im, d_skip, w_glu, w_ssm_out, w_attn_out, w_o, g_ple, w_ple_gate, w_ple,
           g_ple_post):
    s, d = x.shape
    ssm_w = d // 2
    attn_w = N_HEADS * HEAD_DIM
    kvw = N_KV_HEADS * HEAD_DIM
    n_a = 2 * ssm_w + Q_LORA_RANK + 2 * kvw + attn_w
    n_i = IDX_DIM + IDX_HEADS
    w_i = jnp.pad(w_in[0, :, n_a:n_a + n_i], ((0, 0), (0, V7X_LANES - n_i))).astype(BF16)
    w_g = w_in[0, :, n_a + n_i:]
    row = lambda v: v.reshape(1, -1)

    proj, c_q, kidx, w_idx = _proj(x, row(g_mix), w_in, n_a, 2 * ssm_w, w_g, w_i, row(g_kidx))

    n_chunks = s // S5_CHUNK
    cp, bbt, e, a1, a2 = _s5_operators(a_re, a_im, log_dt, b_re, b_im, c_re, c_im,
                                       levels=n_chunks.bit_length() - 1)
    y = _s5(proj, ssm_w, cp, bbt, e, a1, a2)
    y_ssm = _s5_post(y, proj, row(d_skip), w_glu.astype(BF16))

    w_q = jnp.concatenate([w_uq, w_uq_idx], axis=1).astype(BF16)
    q, qi = _qproj(c_q, 0, row(g_q), w_q)
    k_off = 2 * ssm_w + Q_LORA_RANK
    z_block = (k_off + 2 * kvw) // attn_w
    y_attn = _dsa(q, qi, w_idx, proj, z_block, k_off // kvw, kidx)

    merged = _merge(y_ssm, y_attn, w_ssm_out.astype(BF16), w_attn_out.astype(BF16), proj, n_a)
    x1 = _oproj(merged, w_o.astype(BF16), x)
    return x1, (p, w_ple.astype(BF16), w_ple_gate.astype(BF16), row(g_ple), row(g_ple_post))


def kernel(x, p, g_mix, w_in, g_q, w_uq, w_uq_idx, g_kidx, a_re, a_im, log_dt, b_re, b_im, c_re, c_im,
           d_skip, w_glu, w_ssm_out, w_attn_out, w_o, g_ple, w_ple_gate, w_ple, g_ple_post, g_final):
    bsz, seq, d = x.shape
    depth = p.shape[0]
    assert bsz == 1 and depth == 1, "kernel is specialised to one sequence and one layer"
    x2 = x.reshape(seq, d)
    x1, (p0, wp, wg, gple, gpost) = _layer(
        x2, p[0, 0], g_mix[0], w_in[:1], g_q[0], w_uq[0], w_uq_idx[0], g_kidx[0], a_re[0], a_im[0],
        log_dt[0], b_re[0], b_im[0], c_re[0], c_im[0], d_skip[0], w_glu[0], w_ssm_out[0],
        w_attn_out[0], w_o[0], g_ple[0], w_ple_gate[0], w_ple[0], g_ple_post[0])
    out = _final(x1, p0, wp, wg, gple, gpost, g_final.reshape(1, -1))
    return out.reshape(bsz, seq, d)
```

```python
import functools
import math

import numpy as np
import jax
import jax.numpy as jnp
from jax import lax
from jax.experimental import pallas as pl
from jax.experimental.pallas import tpu as pltpu

F32 = jnp.float32
BF16 = jnp.bfloat16
I32 = jnp.int32

EPS = 1e-6
SSM_GROUP = 16
SSM_STATE = 64
N_HEADS = 8
N_KV_HEADS = 2
HEAD_DIM = 128
Q_LORA_RANK = 512
IDX_HEADS = 16
IDX_DIM = 64
TOPK_MAX = 256

V7X_LANES = 128
V7X_VMEM_LIMIT = 56 * 1024 * 1024

S5_CHUNK = 32
DSA_TQ = 256
DSA_TK = 512
SEARCH_ROWS = 128

INT_MIN = -(2 ** 31)
KEY_NEG_INF = -2139095041
NEG_INIT = -1e30

_NT = (((1,), (1,)), ((), ()))


def _cparams(*sem):
    return pltpu.CompilerParams(dimension_semantics=sem, vmem_limit_bytes=V7X_VMEM_LIMIT)


def _rms(x, g):
    ms = jnp.mean(x * x, axis=-1, keepdims=True)
    return x * lax.rsqrt(ms + EPS) * g


def _sigmoid(x):
    return 1.0 / (1.0 + jnp.exp(-x))


def _gelu_tanh(x):
    c = math.sqrt(2.0 / math.pi)
    return 0.5 * x * (1.0 + jnp.tanh(c * (x + 0.044715 * (x * x * x))))


def _proj_kernel(x_ref, g_ref, wa_ref, wg_ref, wi_ref, gk_ref, ob_ref, cq_ref, ki_ref, widx_ref, h_ref,
                 *, n_a_tiles, cq_tile, n_cq_tiles, w_scale):
    j = pl.program_id(1)

    @pl.when(j == 0)
    def _():
        h = _rms(x_ref[...], g_ref[...]).astype(BF16)
        h_ref[...] = h
        r = jnp.dot(h, wi_ref[...], preferred_element_type=F32)
        ki_ref[...] = _rms(r[:, :IDX_DIM], gk_ref[...]).astype(BF16)
        widx_ref[...] = r[:, IDX_DIM:IDX_DIM + IDX_HEADS] * w_scale

    @pl.when(j < n_a_tiles)
    def _():
        acc = jnp.dot(h_ref[...], wa_ref[...].astype(BF16), preferred_element_type=F32)
        ob_ref[...] = acc.astype(BF16)

        @pl.when(jnp.logical_and(j >= cq_tile, j < cq_tile + n_cq_tiles))
        def _():
            cq_ref[...] = acc

    @pl.when(j >= n_a_tiles)
    def _():
        acc = jnp.dot(h_ref[...], wg_ref[...].astype(BF16), preferred_element_type=F32)
        ob_ref[...] = _sigmoid(acc).astype(BF16)


def _proj(x, g, w_in, n_a, cq_col, w_g, w_i, gk, *, tm=2048, tn=256):
    s, d = x.shape
    tm = min(tm, s)
    n_g = w_g.shape[1]
    n_a_tiles = n_a // tn
    cq_tile, n_cq_tiles = cq_col // tn, Q_LORA_RANK // tn
    return pl.pallas_call(
        functools.partial(_proj_kernel, n_a_tiles=n_a_tiles, cq_tile=cq_tile, n_cq_tiles=n_cq_tiles,
                          w_scale=(IDX_HEADS ** -0.5) * (IDX_DIM ** -0.5)),
        out_shape=(jax.ShapeDtypeStruct((s, n_a + n_g), BF16),
                   jax.ShapeDtypeStruct((s, Q_LORA_RANK), F32),
                   jax.ShapeDtypeStruct((s, IDX_DIM), BF16),
                   jax.ShapeDtypeStruct((s, IDX_HEADS), F32)),
        grid=(s // tm, (n_a + n_g) // tn),
        in_specs=[
            pl.BlockSpec((tm, d), lambda i, j: (i, 0), pipeline_mode=pl.Buffered(1)),
            pl.BlockSpec((1, d), lambda i, j: (0, 0)),
            pl.BlockSpec((None, d, tn), lambda i, j: (0, 0, jnp.minimum(j, n_a_tiles - 1))),
            pl.BlockSpec((d, tn), lambda i, j: (0, jnp.maximum(j - n_a_tiles, 0))),
            pl.BlockSpec((d, V7X_LANES), lambda i, j: (0, 0)),
            pl.BlockSpec((1, IDX_DIM), lambda i, j: (0, 0)),
        ],
        out_specs=(pl.BlockSpec((tm, tn), lambda i, j: (i, j)),
                   pl.BlockSpec((tm, tn), lambda i, j: (i, jnp.clip(j - cq_tile, 0, n_cq_tiles - 1))),
                   pl.BlockSpec((tm, IDX_DIM), lambda i, j: (i, 0)),
                   pl.BlockSpec((tm, IDX_HEADS), lambda i, j: (i, 0))),
        scratch_shapes=[pltpu.VMEM((tm, d), BF16)],
        compiler_params=_cparams("parallel", "arbitrary"),
        name="proj",
    )(x, g, w_in, w_g, w_i, gk)


def _qproj_kernel(c_ref, g_ref, w_ref, q_ref, qi_ref, *, q_scale):
    cq = _rms(c_ref[...], g_ref[...]).astype(BF16)
    r = jnp.dot(cq, w_ref[...], preferred_element_type=F32)
    for h in range(N_HEADS):
        q_ref[h] = (r[:, h * HEAD_DIM:(h + 1) * HEAD_DIM] * q_scale).astype(BF16)
    base = N_HEADS * HEAD_DIM
    for h in range(IDX_HEADS):
        qi_ref[h] = r[:, base + h * IDX_DIM: base + (h + 1) * IDX_DIM].astype(BF16)


def _qproj(proj_a, cq_block, g, w, *, tm=512):
    s = proj_a.shape[0]
    return pl.pallas_call(
        functools.partial(_qproj_kernel, q_scale=HEAD_DIM ** -0.5 * math.log2(math.e)),
        out_shape=(jax.ShapeDtypeStruct((N_HEADS, s, HEAD_DIM), BF16),
                   jax.ShapeDtypeStruct((IDX_HEADS, s, IDX_DIM), BF16)),
        grid=(s // tm,),
        in_specs=[
            pl.BlockSpec((tm, Q_LORA_RANK), lambda i: (i, cq_block)),
            pl.BlockSpec((1, Q_LORA_RANK), lambda i: (0, 0)),
            pl.BlockSpec(w.shape, lambda i: (0, 0)),
        ],
        out_specs=(pl.BlockSpec((N_HEADS, tm, HEAD_DIM), lambda i: (0, i, 0)),
                   pl.BlockSpec((IDX_HEADS, tm, IDX_DIM), lambda i: (0, i, 0))),
        compiler_params=_cparams("parallel"),
        name="qproj",
    )(proj_a, g, w)


def _cmul(ar, ai, br, bi):
    return ar * br - ai * bi, ar * bi + ai * br


def _s5_operators(a_re, a_im, log_dt, b_re, b_im, c_re, c_im, levels):
    t_len = S5_CHUNK
    g_n, n_st = a_re.shape
    dt = jnp.exp(log_dt)[:, None]
    mag = jnp.exp(dt * a_re)
    abr = mag * jnp.cos(dt * a_im)
    abi = mag * jnp.sin(dt * a_im)
    den = a_re * a_re + a_im * a_im
    nr = abr - 1.0
    f_re = (nr * a_re + abi * a_im) / den
    f_im = (abi * a_re - nr * a_im) / den
    bt_re = b_re.transpose(0, 2, 1)
    bt_im = b_im.transpose(0, 2, 1)
    bbt_re = f_re[:, None, :] * bt_re - f_im[:, None, :] * bt_im
    bbt_im = f_re[:, None, :] * bt_im + f_im[:, None, :] * bt_re

    nbits = t_len.bit_length()
    sq = [(abr, abi)]
    for _ in range(nbits - 1):
        sq.append(_cmul(*sq[-1], *sq[-1]))
    j = np.arange(t_len + 1)
    pr = jnp.ones((g_n, t_len + 1, n_st), F32)
    pi = jnp.zeros((g_n, t_len + 1, n_st), F32)
    for b in range(nbits):
        bit = jnp.asarray(((j >> b) & 1).astype(bool))[None, :, None]
        mr, mi = _cmul(pr, pi, sq[b][0][:, None, :], sq[b][1][:, None, :])
        pr = jnp.where(bit, mr, pr)
        pi = jnp.where(bit, mi, pi)

    prj, pij = pr[:, :, None, :], pi[:, :, None, :]
    cp_re = c_re[:, None] * prj - c_im[:, None] * pij
    cp_im = c_re[:, None] * pij + c_im[:, None] * prj
    cp = jnp.concatenate([cp_re, -cp_im], axis=-1).reshape(g_n, (t_len + 1) * SSM_GROUP, 2 * n_st)

    prr, pir = prj[:, t_len - 1::-1], pij[:, t_len - 1::-1]
    e_re = prr * bbt_re[:, None] - pir * bbt_im[:, None]
    e_im = prr * bbt_im[:, None] + pir * bbt_re[:, None]
    e = jnp.concatenate([e_re, e_im], axis=-1).reshape(g_n, t_len * SSM_GROUP, 2 * n_st)

    lv_r, lv_i = [pr[:, t_len]], [pi[:, t_len]]
    for _ in range(levels - 1):
        r2, i2 = _cmul(lv_r[-1], lv_i[-1], lv_r[-1], lv_i[-1])
        lv_r.append(r2)
        lv_i.append(i2)
    ar = jnp.stack(lv_r, axis=1)
    ai = jnp.stack(lv_i, axis=1)
    a1 = jnp.concatenate([ar, ar], axis=-1)
    a2 = jnp.concatenate([-ai, ai], axis=-1)
    return cp, jnp.concatenate([bbt_re, bbt_im], axis=-1), e.astype(BF16), a1, a2


def _s5_kernel(*refs, t_len):
    gpt = V7X_LANES // SSM_GROUP
    (u_ref, cp_ref, bbt_ref, e_ref, a1_ref, a2_ref, y_ref,
     rep_ref, master_ref, ebd_ref, fbd_ref, tm_ref) = refs
    n_chunks = u_ref.shape[0] // t_len
    ct = t_len * SSM_GROUP
    n2 = cp_ref.shape[2]
    pair = 2 * V7X_LANES

    @pl.when(pl.program_id(0) == 0)
    def _():
        ebd_ref[...] = jnp.zeros(ebd_ref.shape, BF16)
        fbd_ref[...] = jnp.zeros(fbd_ref.shape, BF16)
        r = lax.broadcasted_iota(I32, rep_ref.shape, 0)
        q = lax.broadcasted_iota(I32, rep_ref.shape, 1)
        same_lag = (r // SSM_GROUP) == (q // V7X_LANES)
        same_ch = (r % SSM_GROUP) == (q % SSM_GROUP)
        rep_ref[...] = jnp.where(same_lag, jnp.where(same_ch, 1.0, 0.0), 0.0).astype(BF16)

    kt8 = jnp.concatenate(
        [lax.dot_general(bbt_ref[k], cp_ref[k, :ct, :], _NT, precision=lax.Precision.HIGHEST,
                         preferred_element_type=F32) for k in range(gpt)], axis=0)
    dall = jnp.dot(kt8.astype(BF16), rep_ref[...], preferred_element_type=F32)
    rg = lax.broadcasted_iota(I32, dall.shape, 0) // SSM_GROUP
    lg = (lax.broadcasted_iota(I32, dall.shape, 1) % V7X_LANES) // SSM_GROUP
    dall = jnp.where(rg == lg, dall, 0.0).astype(BF16)
    for b in range(t_len - 1):
        r0 = (t_len - 2 - b) * V7X_LANES
        master_ref[b * V7X_LANES:(b + 1) * V7X_LANES, :] = dall[:, r0:r0 + pair]
    master_ref[(t_len - 1) * V7X_LANES:, :] = jnp.concatenate(
        [jnp.zeros((V7X_LANES, V7X_LANES), BF16), dall[:, :V7X_LANES]], axis=1)
    for s in range(t_len):
        for k in range(gpt):
            rows = slice(s * V7X_LANES + k * SSM_GROUP, s * V7X_LANES + (k + 1) * SSM_GROUP)
            cols = slice(k * n2, (k + 1) * n2)
            ebd_ref[rows, cols] = e_ref[k, s * SSM_GROUP:(s + 1) * SSM_GROUP, :]
            fbd_ref[rows, cols] = cp_ref[k, (s + 1) * SSM_GROUP:(s + 2) * SSM_GROUP, :].astype(BF16)

    tm_ref[...] = u_ref[...].astype(F32)
    u = jnp.concatenate([tm_ref[pl.ds(s, n_chunks, stride=t_len), :].astype(BF16) for s in range(t_len)],
                        axis=1)
    hend = jnp.dot(u, ebd_ref[...], preferred_element_type=F32)
    row = lax.broadcasted_iota(I32, (hend.shape[0], n2), 0)
    carries = []
    for k in range(gpt):
        p = hend[:, k * n2:(k + 1) * n2]
        for lv in range(a1_ref.shape[1]):
            sh = 1 << lv
            a1 = a1_ref[k, lv:lv + 1, :]
            a2 = a2_ref[k, lv:lv + 1, :]
            ps = jnp.where(row >= sh, pltpu.roll(p, sh, axis=0), 0.0)
            p = p + a1 * ps + a2 * pltpu.roll(ps, n2 // 2, axis=1)
        carries.append(jnp.where(row >= 1, pltpu.roll(p, 1, axis=0), 0.0).astype(BF16))
    carry = jnp.concatenate(carries, axis=1)

    for tp in range(t_len // 2):
        kdim = (tp + 1) * pair
        y2 = jnp.dot(u[:, :kdim], master_ref[(t_len - 2 - 2 * tp) * V7X_LANES:, :],
                     preferred_element_type=F32)
        y2 = y2 + lax.dot_general(carry, fbd_ref[tp * pair:(tp + 1) * pair, :], _NT,
                                  preferred_element_type=F32)
        tm_ref[(2 * tp) * n_chunks:(2 * tp + 1) * n_chunks, :] = y2[:, :V7X_LANES]
        tm_ref[(2 * tp + 1) * n_chunks:(2 * tp + 2) * n_chunks, :] = y2[:, V7X_LANES:]

    for c in range(n_chunks):
        y_ref[c * t_len:(c + 1) * t_len, :] = tm_ref[pl.ds(c, t_len, stride=n_chunks), :]


def _s5(proj, ssm_w, cp, bbt, e, a1, a2):
    s, width = proj.shape
    t_len = S5_CHUNK
    n_chunks = s // t_len
    gpt = V7X_LANES // SSM_GROUP
    ct = t_len * SSM_GROUP
    n2 = e.shape[-1]
    levels = a1.shape[1]
    assert n_chunks == 1 << levels and n2 == V7X_LANES
    grp = lambda rows: pl.BlockSpec((gpt, rows, n2), lambda g: (g, 0, 0))
    col = pl.BlockSpec((s, V7X_LANES), lambda g: (0, g))
    return pl.pallas_call(
        functools.partial(_s5_kernel, t_len=t_len),
        out_shape=jax.ShapeDtypeStruct((s, ssm_w), F32),
        grid=(ssm_w // V7X_LANES,),
        in_specs=[col, grp(ct + SSM_GROUP), grp(SSM_GROUP), grp(ct), grp(levels), grp(levels)],
        out_specs=col,
        scratch_shapes=[
            pltpu.VMEM((ct, t_len * V7X_LANES), BF16),
            pltpu.VMEM((t_len * V7X_LANES, 2 * V7X_LANES), BF16),
            pltpu.VMEM((t_len * V7X_LANES, gpt * n2), BF16),
            pltpu.VMEM((t_len * V7X_LANES, gpt * n2), BF16),
            pltpu.VMEM((s, V7X_LANES), F32),
        ],
        compiler_params=_cparams("arbitrary"),
        name="s5",
    )(proj, cp, bbt, e, a1, a2)


def _s5_post_kernel(y_ref, u_ref, z_ref, d_ref, w_ref, o_ref):
    width = y_ref.shape[1]
    yy = _gelu_tanh(y_ref[...] + d_ref[...] * u_ref[...].astype(F32))
    r = jnp.dot(yy.astype(BF16), w_ref[...], preferred_element_type=F32)
    glu = r[:, :width] * _sigmoid(r[:, width:])
    z = z_ref[...].astype(F32)
    o_ref[...] = (glu * (z * _sigmoid(z))).astype(BF16)


def _s5_post(y, proj_a, d_skip, w_glu, *, tm=512):
    s, width = y.shape
    return pl.pallas_call(
        _s5_post_kernel,
        out_shape=jax.ShapeDtypeStruct((s, width), BF16),
        grid=(s // tm,),
        in_specs=[
            pl.BlockSpec((tm, width), lambda i: (i, 0)),
            pl.BlockSpec((tm, width), lambda i: (i, 0)),
            pl.BlockSpec((tm, width), lambda i: (i, 1)),
            pl.BlockSpec((1, width), lambda i: (0, 0)),
            pl.BlockSpec(w_glu.shape, lambda i: (0, 0)),
        ],
        out_specs=pl.BlockSpec((tm, width), lambda i: (i, 0)),
        compiler_params=_cparams("parallel"),
        name="s5_post",
    )(y, proj_a, proj_a, d_skip, w_glu)


def _key_to_float(key):
    bits = key ^ (lax.shift_right_arithmetic(key, 31) & 0x7FFFFFFF)
    return lax.bitcast_convert_type(bits, F32)


def _dsa_kernel(q_ref, qi_ref, w_ref, z_ref, k_ref, v_ref, ki_ref, o_ref,
                sc_ref, wb_ref, res_ref, cand_ref, cnt_ref, m_ref, l_ref, acc_ref, *, topk):
    tq = DSA_TQ
    halves = tq // V7X_LANES
    cpt = DSA_TK // V7X_LANES
    rep = N_HEADS // N_KV_HEADS
    i = pl.program_id(0)
    n_valid = halves * (i + 1)
    n_it = (n_valid + cpt - 1) // cpt

    for h in range(IDX_HEADS):
        wb_ref[h] = jnp.broadcast_to(w_ref[:, h:h + 1], (tq, V7X_LANES))
    qi = qi_ref[...].reshape(IDX_HEADS * tq, IDX_DIM)

    def score_tile(kt, diag):
        ki = ki_ref[pl.ds(pl.multiple_of(kt * tq, tq), tq), :]
        r = lax.dot_general(qi, ki, _NT, preferred_element_type=F32)
        for half in range(halves):
            lanes = slice(half * V7X_LANES, (half + 1) * V7X_LANES)
            acc = jnp.zeros((tq, V7X_LANES), F32)
            for h in range(IDX_HEADS):
                acc = acc + wb_ref[h] * jnp.maximum(r[h * tq:(h + 1) * tq, lanes], 0.0)
            if diag:
                col = lax.broadcasted_iota(I32, acc.shape, 1) + half * V7X_LANES
                rowi = lax.broadcasted_iota(I32, acc.shape, 0)
                acc = jnp.where(col <= rowi, acc, -jnp.inf)
            sc_ref[halves * kt + half] = acc

    def score_body(kt, c):
        score_tile(kt, False)
        return c

    lax.fori_loop(0, i, score_body, 0)
    score_tile(i, True)

    for j in range(cpt - halves):
        @pl.when(n_valid + j < n_it * cpt)
        def _():
            sc_ref[n_valid + j] = jnp.full((tq, V7X_LANES), -jnp.inf, F32)

    res_ref[...] = jnp.full(res_ref.shape, INT_MIN, I32)

    def pass_body(p, c):
        bit = lax.shift_left(jnp.int32(1), 31 - p)
        cand_ref[...] = _key_to_float(res_ref[...] + bit)
        cnts = []
        for rc in range(tq // SEARCH_ROWS):
            rows = slice(rc * SEARCH_ROWS, (rc + 1) * SEARCH_ROWS)
            cand = cand_ref[rows, :]

            def it_body(it, cnt, rows=rows, cand=cand):
                for j in range(cpt):
                    cnt = cnt + jnp.where(sc_ref[cpt * it + j, rows, :] >= cand, 1.0, 0.0)
                return cnt

            cnts.append(lax.fori_loop(0, n_it, it_body, jnp.zeros((SEARCH_ROWS, V7X_LANES), F32)))
        tot = jnp.sum(jnp.concatenate(cnts, axis=0), axis=1, keepdims=True)
        ok = tot >= float(topk)
        res = res_ref[...]
        res_ref[...] = jnp.where(ok, res + bit, res)
        cnt_ref[...] = jnp.where(ok, tot, cnt_ref[...])
        return c

    cnt_ref[...] = (jnp.full(cnt_ref.shape, DSA_TK, I32) * n_it).astype(F32)
    lax.fori_loop(0, 32, pass_body, 0)
    res = res_ref[...]
    tau = _key_to_float(jnp.maximum(res, KEY_NEG_INF + 1))
    cand_ref[...] = tau

    excess = jnp.where(res > KEY_NEG_INF, cnt_ref[...] - float(topk), 0.0).astype(I32)
    lane = lax.broadcasted_iota(I32, (tq, V7X_LANES), 1)
    n_chunks = n_it * cpt

    def drop_body(r, c):
        active = excess > r

        def min_body(ch, mv):
            sc = sc_ref[ch]
            return jnp.minimum(mv, jnp.where(sc >= tau, sc, jnp.inf))

        mv = lax.fori_loop(0, n_chunks, min_body, jnp.full((tq, V7X_LANES), jnp.inf, F32))
        mval = jnp.min(mv, axis=1, keepdims=True)

        def idx_body(ch, ix):
            col = (lane + ch * V7X_LANES).astype(F32)
            return jnp.maximum(ix, jnp.where(sc_ref[ch] == mval, col, -1.0))

        ix = lax.fori_loop(0, n_chunks, idx_body, jnp.full((tq, V7X_LANES), -1.0, F32))
        idx = jnp.max(ix, axis=1, keepdims=True)

        def drop_chunk(ch, c2):
            col = (lane + ch * V7X_LANES).astype(F32)
            sc = sc_ref[ch]
            sc_ref[ch] = jnp.where(active, jnp.where(col == idx, -jnp.inf, sc), sc)
            return c2

        lax.fori_loop(0, n_chunks, drop_chunk, 0)
        return c

    lax.fori_loop(0, jnp.max(excess), drop_body, 0)

    m_ref[...] = jnp.full(m_ref.shape, NEG_INIT, F32)
    l_ref[...] = jnp.zeros(l_ref.shape, F32)
    acc_ref[...] = jnp.zeros(acc_ref.shape, F32)
    ones = jnp.ones((DSA_TK, HEAD_DIM), BF16)

    def attn_body(it, c):
        tau = cand_ref[...]
        neg = jnp.concatenate(
            [jnp.where(sc_ref[cpt * it + j] >= tau, 0.0, -jnp.inf) for j in range(cpt)], axis=1)
        ks = pl.ds(pl.multiple_of(it * DSA_TK, DSA_TK), DSA_TK)
        for g in range(N_KV_HEADS):
            hs = slice(g * rep, (g + 1) * rep)
            cols = slice(g * HEAD_DIM, (g + 1) * HEAD_DIM)
            qg = q_ref[hs].reshape(rep * tq, HEAD_DIM)
            s = lax.dot_general(qg, k_ref[ks, cols], _NT, preferred_element_type=F32)
            s = s.reshape(rep, tq, DSA_TK) + neg[None]
            m_prev = m_ref[hs]
            m_new = jnp.maximum(m_prev, jnp.max(s, axis=2, keepdims=True))
            alpha = jnp.exp2(m_prev - m_new)
            p = jnp.concatenate(
                [jnp.exp2(s[..., j * V7X_LANES:(j + 1) * V7X_LANES] - m_new) for j in range(cpt)], axis=2)
            v_aug = jnp.concatenate([v_ref[ks, cols], ones], axis=1)
            pv = jnp.dot(p.reshape(rep * tq, DSA_TK).astype(BF16), v_aug,
                         preferred_element_type=F32).reshape(rep, tq, 2 * HEAD_DIM)
            acc_ref[hs] = alpha * acc_ref[hs] + pv[..., :HEAD_DIM]
            l_ref[hs] = alpha * l_ref[hs] + pv[..., HEAD_DIM:]
            m_ref[hs] = m_new
        return c

    lax.fori_loop(0, n_it, attn_body, 0)

    for h in range(N_HEADS):
        cols = slice(h * HEAD_DIM, (h + 1) * HEAD_DIM)
        z = z_ref[:, cols].astype(F32)
        o_ref[:, cols] = ((acc_ref[h] / l_ref[h]) * (z * _sigmoid(z))).astype(BF16)


def _dsa(q, qi, w_idx, proj, z_block, k_block, kidx):
    s = kidx.shape[0]
    tq = DSA_TQ
    width = N_HEADS * HEAD_DIM
    kvw = N_KV_HEADS * HEAD_DIM
    topk = min(TOPK_MAX, s // 4)
    return pl.pallas_call(
        functools.partial(_dsa_kernel, topk=topk),
        out_shape=jax.ShapeDtypeStruct((s, width), BF16),
        grid=(s // tq,),
        in_specs=[
            pl.BlockSpec((N_HEADS, tq, HEAD_DIM), lambda i: (0, i, 0)),
            pl.BlockSpec((IDX_HEADS, tq, IDX_DIM), lambda i: (0, i, 0)),
            pl.BlockSpec((tq, IDX_HEADS), lambda i: (i, 0)),
            pl.BlockSpec((tq, width), lambda i: (i, z_block)),
            pl.BlockSpec((s, kvw), lambda i: (0, k_block)),
            pl.BlockSpec((s, kvw), lambda i: (0, k_block + 1)),
            pl.BlockSpec((s, IDX_DIM), lambda i: (0, 0)),
        ],
        out_specs=pl.BlockSpec((tq, width), lambda i: (i, 0)),
        scratch_shapes=[
            pltpu.VMEM((s // V7X_LANES, tq, V7X_LANES), F32),
            pltpu.VMEM((IDX_HEADS, tq, V7X_LANES), F32),
            pltpu.VMEM((tq, V7X_LANES), I32),
            pltpu.VMEM((tq, V7X_LANES), F32),
            pltpu.VMEM((tq, V7X_LANES), F32),
            pltpu.VMEM((N_HEADS, tq, V7X_LANES), F32),
            pltpu.VMEM((N_HEADS, tq, V7X_LANES), F32),
            pltpu.VMEM((N_HEADS, tq, HEAD_DIM), F32),
        ],
        compiler_params=_cparams("arbitrary"),
        name="dsa",
    )(q, qi, w_idx, proj, proj, proj, kidx)


def _tail_kernel(ys_ref, ya_ref, gs_ref, ga_ref, x_ref, p_ref, wso_ref, wao_ref, wo_ref, wp_ref, wg_ref,
                 gple_ref, gpost_ref, gfin_ref, o_ref):
    o_s = jnp.dot(ys_ref[...], wso_ref[...], preferred_element_type=F32)
    o_a = jnp.dot(ya_ref[...], wao_ref[...], preferred_element_type=F32)
    merged = (gs_ref[...].astype(F32) * o_s + ga_ref[...].astype(F32) * o_a).astype(BF16)
    x1 = x_ref[...] + jnp.dot(merged, wo_ref[...], preferred_element_type=F32)
    e = _rms(jnp.dot(p_ref[...].astype(BF16), wp_ref[...], preferred_element_type=F32), gpost_ref[...])
    xn = _rms(x1, gple_ref[...]).astype(BF16)
    gate = _sigmoid(jnp.dot(xn, wg_ref[...], preferred_element_type=F32))
    o_ref[...] = _rms(x1 + gate * e, gfin_ref[...])


def _tail(y_ssm, y_attn, proj, gate_col, x, p, w_so, w_ao, w_o, w_ple, w_gate, g_ple, g_post, g_final,
          *, tm=256):
    s, d = x.shape
    kdim = y_ssm.shape[1]
    pd = p.shape[1]
    g0 = gate_col // d
    rows = lambda width, blk=0: pl.BlockSpec((tm, width), lambda i: (i, blk))
    held = lambda shape: pl.BlockSpec(shape, lambda i: (0, 0), pipeline_mode=pl.Buffered(1))
    vec = pl.BlockSpec((1, d), lambda i: (0, 0))
    return pl.pallas_call(
        _tail_kernel,
        out_shape=jax.ShapeDtypeStruct((s, d), F32),
        grid=(s // tm,),
        in_specs=[rows(kdim), rows(kdim), rows(d, g0), rows(d, g0 + 1), rows(d), rows(pd),
                  held((kdim, d)), held((kdim, d)), held((d, d)), held((pd, d)), held((d, d)),
                  vec, vec, vec],
        out_specs=rows(d),
        compiler_params=_cparams("parallel"),
        name="tail",
    )(y_ssm, y_attn, proj, proj, x, p, w_so, w_ao, w_o, w_ple, w_gate, g_ple, g_post, g_final)


def kernel(x, p, g_mix, w_in, g_q, w_uq, w_uq_idx, g_kidx, a_re, a_im, log_dt, b_re, b_im, c_re, c_im,
           d_skip, w_glu, w_ssm_out, w_attn_out, w_o, g_ple, w_ple_gate, w_ple, g_ple_post, g_final):
    bsz, s, d = x.shape
    assert bsz == 1 and p.shape[0] == 1, "kernel is specialised to one sequence and one layer"
    x = x.reshape(s, d)
    ssm_w = d // 2
    attn_w = N_HEADS * HEAD_DIM
    kvw = N_KV_HEADS * HEAD_DIM
    n_a = 2 * ssm_w + Q_LORA_RANK + 2 * kvw + attn_w
    n_i = IDX_DIM + IDX_HEADS
    w_i = jnp.pad(w_in[0, :, n_a:n_a + n_i], ((0, 0), (0, V7X_LANES - n_i))).astype(BF16)
    w_g = w_in[0, :, n_a + n_i:]
    row = lambda v: v.reshape(1, -1)

    proj, c_q, kidx, w_idx = _proj(x, row(g_mix), w_in[:1], n_a, 2 * ssm_w, w_g, w_i, row(g_kidx))

    n_chunks = s // S5_CHUNK
    cp, bbt, e, a1, a2 = _s5_operators(a_re[0], a_im[0], log_dt[0], b_re[0], b_im[0], c_re[0], c_im[0],
                                       levels=n_chunks.bit_length() - 1)
    y = _s5(proj, ssm_w, cp, bbt, e, a1, a2)
    y_ssm = _s5_post(y, proj, row(d_skip), w_glu[0].astype(BF16))

    w_q = jnp.concatenate([w_uq[0], w_uq_idx[0]], axis=1).astype(BF16)
    q, qi = _qproj(c_q, 0, row(g_q), w_q)
    k_off = 2 * ssm_w + Q_LORA_RANK
    z_block = (k_off + 2 * kvw) // attn_w
    y_attn = _dsa(q, qi, w_idx, proj, z_block, k_off // kvw, kidx)

    out = _tail(y_ssm, y_attn, proj, n_a, x, p[0, 0], w_ssm_out[0].astype(BF16), w_attn_out[0].astype(BF16),
                w_o[0].astype(BF16), w_ple[0].astype(BF16), w_ple_gate[0].astype(BF16),
                row(g_ple), row(g_ple_post), row(g_final))
    return out.reshape(bsz, s, d)
```

```python
import functools
import math

import numpy as np
import jax
import jax.numpy as jnp
from jax import lax
from jax.experimental import pallas as pl
from jax.experimental.pallas import tpu as pltpu

F32 = jnp.float32
BF16 = jnp.bfloat16
I32 = jnp.int32

EPS = 1e-6
SSM_GROUP = 16
SSM_STATE = 64
N_HEADS = 8
N_KV_HEADS = 2
HEAD_DIM = 128
Q_LORA_RANK = 512
IDX_HEADS = 16
IDX_DIM = 64
TOPK_MAX = 256

V7X_LANES = 128
V7X_VMEM_LIMIT = 56 * 1024 * 1024

S5_CHUNK = 32
DSA_TQ = 256
DSA_TK = 512
SEARCH_ROWS = 128

SEARCH_SPAN_MAX = 1 << 30
SEARCH_BITS_MIN = 20

INT_MIN = -(2 ** 31)
KEY_NEG_INF = -2139095041
NEG_INIT = -1e30

_NT = (((1,), (1,)), ((), ()))


def _cparams(*sem):
    return pltpu.CompilerParams(dimension_semantics=sem, vmem_limit_bytes=V7X_VMEM_LIMIT)


def _rms(x, g):
    ms = jnp.mean(x * x, axis=-1, keepdims=True)
    return x * lax.rsqrt(ms + EPS) * g


def _sigmoid(x):
    return 1.0 / (1.0 + jnp.exp(-x))


def _gelu_tanh(x):
    c = math.sqrt(2.0 / math.pi)
    return 0.5 * x * (1.0 + jnp.tanh(c * (x + 0.044715 * (x * x * x))))


def _proj_kernel(x_ref, g_ref, wa_ref, wg_ref, wi_ref, gk_ref, ob_ref, cq_ref, ki_ref, widx_ref, h_ref,
                 *, n_a_tiles, cq_tile, n_cq_tiles, w_scale):
    j = pl.program_id(1)

    @pl.when(j == 0)
    def _():
        h = _rms(x_ref[...], g_ref[...]).astype(BF16)
        h_ref[...] = h
        r = jnp.dot(h, wi_ref[...], preferred_element_type=F32)
        ki_ref[...] = _rms(r[:, :IDX_DIM], gk_ref[...]).astype(BF16)
        widx_ref[...] = r[:, IDX_DIM:IDX_DIM + IDX_HEADS] * w_scale

    @pl.when(j < n_a_tiles)
    def _():
        acc = jnp.dot(h_ref[...], wa_ref[...].astype(BF16), preferred_element_type=F32)
        ob_ref[...] = acc.astype(BF16)

        @pl.when(jnp.logical_and(j >= cq_tile, j < cq_tile + n_cq_tiles))
        def _():
            cq_ref[...] = acc

    @pl.when(j >= n_a_tiles)
    def _():
        acc = jnp.dot(h_ref[...], wg_ref[...].astype(BF16), preferred_element_type=F32)
        ob_ref[...] = _sigmoid(acc).astype(BF16)


def _proj(x, g, w_in, n_a, cq_col, w_g, w_i, gk, *, tm=2048, tn=256):
    s, d = x.shape
    tm = min(tm, s)
    n_g = w_g.shape[1]
    n_a_tiles = n_a // tn
    cq_tile, n_cq_tiles = cq_col // tn, Q_LORA_RANK // tn
    return pl.pallas_call(
        functools.partial(_proj_kernel, n_a_tiles=n_a_tiles, cq_tile=cq_tile, n_cq_tiles=n_cq_tiles,
                          w_scale=(IDX_HEADS ** -0.5) * (IDX_DIM ** -0.5)),
        out_shape=(jax.ShapeDtypeStruct((s, n_a + n_g), BF16),
                   jax.ShapeDtypeStruct((s, Q_LORA_RANK), F32),
                   jax.ShapeDtypeStruct((s, IDX_DIM), BF16),
                   jax.ShapeDtypeStruct((s, IDX_HEADS), F32)),
        grid=(s // tm, (n_a + n_g) // tn),
        in_specs=[
            pl.BlockSpec((tm, d), lambda i, j: (i, 0), pipeline_mode=pl.Buffered(1)),
            pl.BlockSpec((1, d), lambda i, j: (0, 0)),
            pl.BlockSpec((None, d, tn), lambda i, j: (0, 0, jnp.minimum(j, n_a_tiles - 1))),
            pl.BlockSpec((d, tn), lambda i, j: (0, jnp.maximum(j - n_a_tiles, 0))),
            pl.BlockSpec((d, V7X_LANES), lambda i, j: (0, 0)),
            pl.BlockSpec((1, IDX_DIM), lambda i, j: (0, 0)),
        ],
        out_specs=(pl.BlockSpec((tm, tn), lambda i, j: (i, j)),
                   pl.BlockSpec((tm, tn), lambda i, j: (i, jnp.clip(j - cq_tile, 0, n_cq_tiles - 1))),
                   pl.BlockSpec((tm, IDX_DIM), lambda i, j: (i, 0)),
                   pl.BlockSpec((tm, IDX_HEADS), lambda i, j: (i, 0))),
        scratch_shapes=[pltpu.VMEM((tm, d), BF16)],
        compiler_params=_cparams("parallel", "arbitrary"),
        name="proj",
    )(x, g, w_in, w_g, w_i, gk)


def _qproj_kernel(c_ref, g_ref, w_ref, q_ref, qi_ref, *, q_scale):
    cq = _rms(c_ref[...], g_ref[...]).astype(BF16)
    r = jnp.dot(cq, w_ref[...], preferred_element_type=F32)
    for h in range(N_HEADS):
        q_ref[h] = (r[:, h * HEAD_DIM:(h + 1) * HEAD_DIM] * q_scale).astype(BF16)
    base = N_HEADS * HEAD_DIM
    for h in range(IDX_HEADS):
        qi_ref[h] = r[:, base + h * IDX_DIM: base + (h + 1) * IDX_DIM].astype(BF16)


def _qproj(proj_a, cq_block, g, w, *, tm=512):
    s = proj_a.shape[0]
    return pl.pallas_call(
        functools.partial(_qproj_kernel, q_scale=HEAD_DIM ** -0.5 * math.log2(math.e)),
        out_shape=(jax.ShapeDtypeStruct((N_HEADS, s, HEAD_DIM), BF16),
                   jax.ShapeDtypeStruct((IDX_HEADS, s, IDX_DIM), BF16)),
        grid=(s // tm,),
        in_specs=[
            pl.BlockSpec((tm, Q_LORA_RANK), lambda i: (i, cq_block)),
            pl.BlockSpec((1, Q_LORA_RANK), lambda i: (0, 0)),
            pl.BlockSpec(w.shape, lambda i: (0, 0)),
        ],
        out_specs=(pl.BlockSpec((N_HEADS, tm, HEAD_DIM), lambda i: (0, i, 0)),
                   pl.BlockSpec((IDX_HEADS, tm, IDX_DIM), lambda i: (0, i, 0))),
        compiler_params=_cparams("parallel"),
        name="qproj",
    )(proj_a, g, w)


def _cmul(ar, ai, br, bi):
    return ar * br - ai * bi, ar * bi + ai * br


def _s5_operators(a_re, a_im, log_dt, b_re, b_im, c_re, c_im, levels):
    t_len = S5_CHUNK
    g_n, n_st = a_re.shape
    dt = jnp.exp(log_dt)[:, None]
    mag = jnp.exp(dt * a_re)
    abr = mag * jnp.cos(dt * a_im)
    abi = mag * jnp.sin(dt * a_im)
    den = a_re * a_re + a_im * a_im
    nr = abr - 1.0
    f_re = (nr * a_re + abi * a_im) / den
    f_im = (abi * a_re - nr * a_im) / den
    bt_re = b_re.transpose(0, 2, 1)
    bt_im = b_im.transpose(0, 2, 1)
    bbt_re = f_re[:, None, :] * bt_re - f_im[:, None, :] * bt_im
    bbt_im = f_re[:, None, :] * bt_im + f_im[:, None, :] * bt_re

    nbits = t_len.bit_length()
    sq = [(abr, abi)]
    for _ in range(nbits - 1):
        sq.append(_cmul(*sq[-1], *sq[-1]))
    j = np.arange(t_len + 1)
    pr = jnp.ones((g_n, t_len + 1, n_st), F32)
    pi = jnp.zeros((g_n, t_len + 1, n_st), F32)
    for b in range(nbits):
        bit = jnp.asarray(((j >> b) & 1).astype(bool))[None, :, None]
        mr, mi = _cmul(pr, pi, sq[b][0][:, None, :], sq[b][1][:, None, :])
        pr = jnp.where(bit, mr, pr)
        pi = jnp.where(bit, mi, pi)

    prj, pij = pr[:, :, None, :], pi[:, :, None, :]
    cp_re = c_re[:, None] * prj - c_im[:, None] * pij
    cp_im = c_re[:, None] * pij + c_im[:, None] * prj
    cp = jnp.concatenate([cp_re, -cp_im], axis=-1).reshape(g_n, (t_len + 1) * SSM_GROUP, 2 * n_st)

    prr, pir = prj[:, t_len - 1::-1], pij[:, t_len - 1::-1]
    e_re = prr * bbt_re[:, None] - pir * bbt_im[:, None]
    e_im = prr * bbt_im[:, None] + pir * bbt_re[:, None]
    e = jnp.concatenate([e_re, e_im], axis=-1).reshape(g_n, t_len * SSM_GROUP, 2 * n_st)

    lv_r, lv_i = [pr[:, t_len]], [pi[:, t_len]]
    for _ in range(levels - 1):
        r2, i2 = _cmul(lv_r[-1], lv_i[-1], lv_r[-1], lv_i[-1])
        lv_r.append(r2)
        lv_i.append(i2)
    ar = jnp.stack(lv_r, axis=1)
    ai = jnp.stack(lv_i, axis=1)
    a1 = jnp.concatenate([ar, ar], axis=-1)
    a2 = jnp.concatenate([-ai, ai], axis=-1)
    return cp, jnp.concatenate([bbt_re, bbt_im], axis=-1), e.astype(BF16), a1, a2


def _s5_kernel(*refs, t_len):
    gpt = V7X_LANES // SSM_GROUP
    (u_ref, cp_ref, bbt_ref, e_ref, a1_ref, a2_ref, y_ref,
     rep_ref, master_ref, ebd_ref, fbd_ref, tm_ref) = refs
    n_chunks = u_ref.shape[0] // t_len
    ct = t_len * SSM_GROUP
    n2 = cp_ref.shape[2]
    pair = 2 * V7X_LANES

    @pl.when(pl.program_id(0) == 0)
    def _():
        ebd_ref[...] = jnp.zeros(ebd_ref.shape, BF16)
        fbd_ref[...] = jnp.zeros(fbd_ref.shape, BF16)
        r = lax.broadcasted_iota(I32, rep_ref.shape, 0)
        q = lax.broadcasted_iota(I32, rep_ref.shape, 1)
        same_lag = (r // SSM_GROUP) == (q // V7X_LANES)
        same_ch = (r % SSM_GROUP) == (q % SSM_GROUP)
        rep_ref[...] = jnp.where(same_lag, jnp.where(same_ch, 1.0, 0.0), 0.0).astype(BF16)

    kt8 = jnp.concatenate(
        [lax.dot_general(bbt_ref[k], cp_ref[k, :ct, :], _NT, precision=lax.Precision.HIGHEST,
                         preferred_element_type=F32) for k in range(gpt)], axis=0)
    dall = jnp.dot(kt8.astype(BF16), rep_ref[...], preferred_element_type=F32)
    rg = lax.broadcasted_iota(I32, dall.shape, 0) // SSM_GROUP
    lg = (lax.broadcasted_iota(I32, dall.shape, 1) % V7X_LANES) // SSM_GROUP
    dall = jnp.where(rg == lg, dall, 0.0).astype(BF16)
    for b in range(t_len - 1):
        r0 = (t_len - 2 - b) * V7X_LANES
        master_ref[b * V7X_LANES:(b + 1) * V7X_LANES, :] = dall[:, r0:r0 + pair]
    master_ref[(t_len - 1) * V7X_LANES:, :] = jnp.concatenate(
        [jnp.zeros((V7X_LANES, V7X_LANES), BF16), dall[:, :V7X_LANES]], axis=1)
    for s in range(t_len):
        for k in range(gpt):
            rows = slice(s * V7X_LANES + k * SSM_GROUP, s * V7X_LANES + (k + 1) * SSM_GROUP)
            cols = slice(k * n2, (k + 1) * n2)
            ebd_ref[rows, cols] = e_ref[k, s * SSM_GROUP:(s + 1) * SSM_GROUP, :]
            fbd_ref[rows, cols] = cp_ref[k, (s + 1) * SSM_GROUP:(s + 2) * SSM_GROUP, :].astype(BF16)

    tm_ref[...] = u_ref[...].astype(F32)
    u = jnp.concatenate([tm_ref[pl.ds(s, n_chunks, stride=t_len), :].astype(BF16) for s in range(t_len)],
                        axis=1)
    hend = jnp.dot(u, ebd_ref[...], preferred_element_type=F32)
    row = lax.broadcasted_iota(I32, (hend.shape[0], n2), 0)
    carries = []
    for k in range(gpt):
        p = hend[:, k * n2:(k + 1) * n2]
        for lv in range(a1_ref.shape[1]):
            sh = 1 << lv
            a1 = a1_ref[k, lv:lv + 1, :]
            a2 = a2_ref[k, lv:lv + 1, :]
            ps = jnp.where(row >= sh, pltpu.roll(p, sh, axis=0), 0.0)
            p = p + a1 * ps + a2 * pltpu.roll(ps, n2 // 2, axis=1)
        carries.append(jnp.where(row >= 1, pltpu.roll(p, 1, axis=0), 0.0).astype(BF16))
    carry = jnp.concatenate(carries, axis=1)

    for tp in range(t_len // 2):
        kdim = (tp + 1) * pair
        y2 = jnp.dot(u[:, :kdim], master_ref[(t_len - 2 - 2 * tp) * V7X_LANES:, :],
                     preferred_element_type=F32)
        y2 = y2 + lax.dot_general(carry, fbd_ref[tp * pair:(tp + 1) * pair, :], _NT,
                                  preferred_element_type=F32)
        tm_ref[(2 * tp) * n_chunks:(2 * tp + 1) * n_chunks, :] = y2[:, :V7X_LANES]
        tm_ref[(2 * tp + 1) * n_chunks:(2 * tp + 2) * n_chunks, :] = y2[:, V7X_LANES:]

    for c in range(n_chunks):
        y_ref[c * t_len:(c + 1) * t_len, :] = tm_ref[pl.ds(c, t_len, stride=n_chunks), :]


def _s5(proj, ssm_w, cp, bbt, e, a1, a2):
    s, width = proj.shape
    t_len = S5_CHUNK
    n_chunks = s // t_len
    gpt = V7X_LANES // SSM_GROUP
    ct = t_len * SSM_GROUP
    n2 = e.shape[-1]
    levels = a1.shape[1]
    assert n_chunks == 1 << levels and n2 == V7X_LANES
    grp = lambda rows: pl.BlockSpec((gpt, rows, n2), lambda g: (g, 0, 0))
    col = pl.BlockSpec((s, V7X_LANES), lambda g: (0, g))
    return pl.pallas_call(
        functools.partial(_s5_kernel, t_len=t_len),
        out_shape=jax.ShapeDtypeStruct((s, ssm_w), F32),
        grid=(ssm_w // V7X_LANES,),
        in_specs=[col, grp(ct + SSM_GROUP), grp(SSM_GROUP), grp(ct), grp(levels), grp(levels)],
        out_specs=col,
        scratch_shapes=[
            pltpu.VMEM((ct, t_len * V7X_LANES), BF16),
            pltpu.VMEM((t_len * V7X_LANES, 2 * V7X_LANES), BF16),
            pltpu.VMEM((t_len * V7X_LANES, gpt * n2), BF16),
            pltpu.VMEM((t_len * V7X_LANES, gpt * n2), BF16),
            pltpu.VMEM((s, V7X_LANES), F32),
        ],
        compiler_params=_cparams("arbitrary"),
        name="s5",
    )(proj, cp, bbt, e, a1, a2)


def _s5_post_kernel(y_ref, u_ref, z_ref, d_ref, w_ref, o_ref):
    width = y_ref.shape[1]
    yy = _gelu_tanh(y_ref[...] + d_ref[...] * u_ref[...].astype(F32))
    r = jnp.dot(yy.astype(BF16), w_ref[...], preferred_element_type=F32)
    glu = r[:, :width] * _sigmoid(r[:, width:])
    z = z_ref[...].astype(F32)
    o_ref[...] = (glu * (z * _sigmoid(z))).astype(BF16)


def _s5_post(y, proj_a, d_skip, w_glu, *, tm=512):
    s, width = y.shape
    return pl.pallas_call(
        _s5_post_kernel,
        out_shape=jax.ShapeDtypeStruct((s, width), BF16),
        grid=(s // tm,),
        in_specs=[
            pl.BlockSpec((tm, width), lambda i: (i, 0)),
            pl.BlockSpec((tm, width), lambda i: (i, 0)),
            pl.BlockSpec((tm, width), lambda i: (i, 1)),
            pl.BlockSpec((1, width), lambda i: (0, 0)),
            pl.BlockSpec(w_glu.shape, lambda i: (0, 0)),
        ],
        out_specs=pl.BlockSpec((tm, width), lambda i: (i, 0)),
        compiler_params=_cparams("parallel"),
        name="s5_post",
    )(y, proj_a, proj_a, d_skip, w_glu)


def _key_to_float(key):
    bits = key ^ (lax.shift_right_arithmetic(key, 31) & 0x7FFFFFFF)
    return lax.bitcast_convert_type(bits, F32)


def _sort_key(x):
    bits = lax.bitcast_convert_type(x, I32)
    return bits ^ (lax.shift_right_arithmetic(bits, 31) & 0x7FFFFFFF)


def _dsa_kernel(q_ref, qi_ref, w_ref, z_ref, k_ref, v_ref, ki_ref, o_ref,
                sc_ref, wb_ref, gmax_ref, res_ref, cand_ref, cnt_ref, m_ref, l_ref, acc_ref, *, topk):
    tq = DSA_TQ
    halves = tq // V7X_LANES
    cpt = DSA_TK // V7X_LANES
    rep = N_HEADS // N_KV_HEADS
    i = pl.program_id(0)
    n_valid = halves * (i + 1)
    n_it = (n_valid + cpt - 1) // cpt

    for h in range(IDX_HEADS):
        wb_ref[h] = jnp.broadcast_to(w_ref[:, h:h + 1], (tq, V7X_LANES))
    qi = qi_ref[...].reshape(IDX_HEADS * tq, IDX_DIM)
    gmax_ref[...] = jnp.full(gmax_ref.shape, -jnp.inf, F32)

    def score_tile(kt, diag):
        ki = ki_ref[pl.ds(pl.multiple_of(kt * tq, tq), tq), :]
        r = lax.dot_general(qi, ki, _NT, preferred_element_type=F32)
        for half in range(halves):
            lanes = slice(half * V7X_LANES, (half + 1) * V7X_LANES)
            acc = jnp.zeros((tq, V7X_LANES), F32)
            for h in range(IDX_HEADS):
                acc = acc + wb_ref[h] * jnp.maximum(r[h * tq:(h + 1) * tq, lanes], 0.0)
            if diag:
                col = lax.broadcasted_iota(I32, acc.shape, 1) + half * V7X_LANES
                rowi = lax.broadcasted_iota(I32, acc.shape, 0)
                acc = jnp.where(col <= rowi, acc, -jnp.inf)
            gmax_ref[half] = jnp.maximum(gmax_ref[half], acc)
            sc_ref[halves * kt + half] = acc

    def score_body(kt, c):
        score_tile(kt, False)
        return c

    lax.fori_loop(0, i, score_body, 0)
    score_tile(i, True)

    for j in range(cpt - halves):
        @pl.when(n_valid + j < n_it * cpt)
        def _():
            sc_ref[n_valid + j] = jnp.full((tq, V7X_LANES), -jnp.inf, F32)

    def count_ge(thresholds):
        cand_ref[...] = thresholds
        cnts = []
        for rc in range(tq // SEARCH_ROWS):
            rows = slice(rc * SEARCH_ROWS, (rc + 1) * SEARCH_ROWS)
            cand = cand_ref[rows, :]

            def it_body(it, cnt, rows=rows, cand=cand):
                for j in range(cpt):
                    cnt = cnt + jnp.where(sc_ref[cpt * it + j, rows, :] >= cand, 1.0, 0.0)
                return cnt

            cnts.append(lax.fori_loop(0, n_it, it_body, jnp.zeros((SEARCH_ROWS, V7X_LANES), F32)))
        return jnp.sum(jnp.concatenate(cnts, axis=0), axis=1, keepdims=True)

    gmin = gmax_ref[0]
    gtop = gmax_ref[0]
    for half in range(1, halves):
        gmin = jnp.minimum(gmin, gmax_ref[half])
        gtop = jnp.maximum(gtop, gmax_ref[half])
    shape = res_ref.shape
    lo = jnp.broadcast_to(jnp.min(gmin, axis=1, keepdims=True), shape)
    hi = jnp.broadcast_to(jnp.max(gtop, axis=1, keepdims=True), shape)
    lo_key, hi_key = _sort_key(lo), _sort_key(hi)
    cnt_lo = jnp.broadcast_to(count_ge(lo), shape)
    same_sign = (lo_key ^ hi_key) >= 0
    span = jnp.where(same_sign, hi_key - lo_key, SEARCH_SPAN_MAX)
    span = jnp.where(cnt_lo >= float(topk), span, SEARCH_SPAN_MAX)
    span = jnp.clip(span, 1 << SEARCH_BITS_MIN, SEARCH_SPAN_MAX)
    span_bits = jnp.max(32 - lax.clz(span))
    narrow = span_bits < SEARCH_SPAN_MAX.bit_length()
    n_pass = jnp.where(narrow, span_bits + 1, 32)
    res_ref[...] = jnp.where(narrow, lo_key, INT_MIN)
    cnt_ref[...] = jnp.where(narrow, cnt_lo, (jnp.full(shape, DSA_TK, I32) * n_it).astype(F32))

    def pass_body(p, c):
        bit = lax.shift_left(jnp.int32(1), n_pass - 1 - p)
        res = res_ref[...]
        tot = count_ge(jnp.where(res + bit < res, jnp.nan, _key_to_float(res + bit)))
        ok = tot >= float(topk)
        res_ref[...] = jnp.where(ok, res + bit, res)
        cnt_ref[...] = jnp.where(ok, tot, cnt_ref[...])
        return c

    lax.fori_loop(0, n_pass, pass_body, 0)
    res = res_ref[...]
    tau = _key_to_float(jnp.maximum(res, KEY_NEG_INF + 1))
    cand_ref[...] = tau

    excess = jnp.where(res > KEY_NEG_INF, cnt_ref[...] - float(topk), 0.0).astype(I32)
    lane = lax.broadcasted_iota(I32, (tq, V7X_LANES), 1)
    n_chunks = n_it * cpt

    def drop_body(r, c):
        active = excess > r

        def min_body(ch, mv):
            sc = sc_ref[ch]
            return jnp.minimum(mv, jnp.where(sc >= tau, sc, jnp.inf))

        mv = lax.fori_loop(0, n_chunks, min_body, jnp.full((tq, V7X_LANES), jnp.inf, F32))
        mval = jnp.min(mv, axis=1, keepdims=True)

        def idx_body(ch, ix):
            col = (lane + ch * V7X_LANES).astype(F32)
            return jnp.maximum(ix, jnp.where(sc_ref[ch] == mval, col, -1.0))

        ix = lax.fori_loop(0, n_chunks, idx_body, jnp.full((tq, V7X_LANES), -1.0, F32))
        idx = jnp.max(ix, axis=1, keepdims=True)

        def drop_chunk(ch, c2):
            col = (lane + ch * V7X_LANES).astype(F32)
            sc = sc_ref[ch]
            sc_ref[ch] = jnp.where(active, jnp.where(col == idx, -jnp.inf, sc), sc)
            return c2

        lax.fori_loop(0, n_chunks, drop_chunk, 0)
        return c

    lax.fori_loop(0, jnp.max(excess), drop_body, 0)

    m_ref[...] = jnp.full(m_ref.shape, NEG_INIT, F32)
    l_ref[...] = jnp.zeros(l_ref.shape, F32)
    acc_ref[...] = jnp.zeros(acc_ref.shape, F32)
    ones = jnp.ones((DSA_TK, HEAD_DIM), BF16)

    def attn_body(it, c):
        tau = cand_ref[...]
        neg = jnp.concatenate(
            [jnp.where(sc_ref[cpt * it + j] >= tau, 0.0, -jnp.inf) for j in range(cpt)], axis=1)
        ks = pl.ds(pl.multiple_of(it * DSA_TK, DSA_TK), DSA_TK)
        for g in range(N_KV_HEADS):
            hs = slice(g * rep, (g + 1) * rep)
            cols = slice(g * HEAD_DIM, (g + 1) * HEAD_DIM)
            qg = q_ref[hs].reshape(rep * tq, HEAD_DIM)
            s = lax.dot_general(qg, k_ref[ks, cols], _NT, preferred_element_type=F32)
            s = s.reshape(rep, tq, DSA_TK) + neg[None]
            m_prev = m_ref[hs]
            m_new = jnp.maximum(m_prev, jnp.max(s, axis=2, keepdims=True))
            alpha = jnp.exp2(m_prev - m_new)
            p = jnp.concatenate(
                [jnp.exp2(s[..., j * V7X_LANES:(j + 1) * V7X_LANES] - m_new) for j in range(cpt)], axis=2)
            v_aug = jnp.concatenate([v_ref[ks, cols], ones], axis=1)
            pv = jnp.dot(p.reshape(rep * tq, DSA_TK).astype(BF16), v_aug,
                         preferred_element_type=F32).reshape(rep, tq, 2 * HEAD_DIM)
            acc_ref[hs] = alpha * acc_ref[hs] + pv[..., :HEAD_DIM]
            l_ref[hs] = alpha * l_ref[hs] + pv[..., HEAD_DIM:]
            m_ref[hs] = m_new
        return c

    lax.fori_loop(0, n_it, attn_body, 0)

    for h in range(N_HEADS):
        cols = slice(h * HEAD_DIM, (h + 1) * HEAD_DIM)
        z = z_ref[:, cols].astype(F32)
        o_ref[:, cols] = ((acc_ref[h] / l_ref[h]) * (z * _sigmoid(z))).astype(BF16)


def _dsa(q, qi, w_idx, proj, z_block, k_block, kidx):
    s = kidx.shape[0]
    tq = DSA_TQ
    width = N_HEADS * HEAD_DIM
    kvw = N_KV_HEADS * HEAD_DIM
    topk = min(TOPK_MAX, s // 4)
    held = pl.Buffered(1)
    return pl.pallas_call(
        functools.partial(_dsa_kernel, topk=topk),
        out_shape=jax.ShapeDtypeStruct((s, width), BF16),
        grid=(s // tq,),
        in_specs=[
            pl.BlockSpec((N_HEADS, tq, HEAD_DIM), lambda i: (0, i, 0)),
            pl.BlockSpec((IDX_HEADS, tq, IDX_DIM), lambda i: (0, i, 0)),
            pl.BlockSpec((tq, IDX_HEADS), lambda i: (i, 0)),
            pl.BlockSpec((tq, width), lambda i: (i, z_block)),
            pl.BlockSpec((s, kvw), lambda i: (0, k_block), pipeline_mode=held),
            pl.BlockSpec((s, kvw), lambda i: (0, k_block + 1), pipeline_mode=held),
            pl.BlockSpec((s, IDX_DIM), lambda i: (0, 0), pipeline_mode=held),
        ],
        out_specs=pl.BlockSpec((tq, width), lambda i: (i, 0)),
        scratch_shapes=[
            pltpu.VMEM((s // V7X_LANES, tq, V7X_LANES), F32),
            pltpu.VMEM((IDX_HEADS, tq, V7X_LANES), F32),
            pltpu.VMEM((tq // V7X_LANES, tq, V7X_LANES), F32),
            pltpu.VMEM((tq, V7X_LANES), I32),
            pltpu.VMEM((tq, V7X_LANES), F32),
            pltpu.VMEM((tq, V7X_LANES), F32),
            pltpu.VMEM((N_HEADS, tq, V7X_LANES), F32),
            pltpu.VMEM((N_HEADS, tq, V7X_LANES), F32),
            pltpu.VMEM((N_HEADS, tq, HEAD_DIM), F32),
        ],
        compiler_params=_cparams("arbitrary"),
        name="dsa",
    )(q, qi, w_idx, proj, proj, proj, kidx)


def _tail_kernel(ys_ref, ya_ref, gs_ref, ga_ref, x_ref, p_ref, wso_ref, wao_ref, wo_ref, wp_ref, wg_ref,
                 gple_ref, gpost_ref, gfin_ref, o_ref):
    o_s = jnp.dot(ys_ref[...], wso_ref[...], preferred_element_type=F32)
    o_a = jnp.dot(ya_ref[...], wao_ref[...], preferred_element_type=F32)
    merged = (gs_ref[...].astype(F32) * o_s + ga_ref[...].astype(F32) * o_a).astype(BF16)
    x1 = x_ref[...] + jnp.dot(merged, wo_ref[...], preferred_element_type=F32)
    e = _rms(jnp.dot(p_ref[...].astype(BF16), wp_ref[...], preferred_element_type=F32), gpost_ref[...])
    xn = _rms(x1, gple_ref[...]).astype(BF16)
    gate = _sigmoid(jnp.dot(xn, wg_ref[...], preferred_element_type=F32))
    o_ref[...] = _rms(x1 + gate * e, gfin_ref[...])


def _tail(y_ssm, y_attn, proj, gate_col, x, p, w_so, w_ao, w_o, w_ple, w_gate, g_ple, g_post, g_final,
          *, tm=256):
    s, d = x.shape
    kdim = y_ssm.shape[1]
    pd = p.shape[1]
    g0 = gate_col // d
    rows = lambda width, blk=0: pl.BlockSpec((tm, width), lambda i: (i, blk))
    held = lambda shape: pl.BlockSpec(shape, lambda i: (0, 0), pipeline_mode=pl.Buffered(1))
    vec = pl.BlockSpec((1, d), lambda i: (0, 0))
    return pl.pallas_call(
        _tail_kernel,
        out_shape=jax.ShapeDtypeStruct((s, d), F32),
        grid=(s // tm,),
        in_specs=[rows(kdim), rows(kdim), rows(d, g0), rows(d, g0 + 1), rows(d), rows(pd),
                  held((kdim, d)), held((kdim, d)), held((d, d)), held((pd, d)), held((d, d)),
                  vec, vec, vec],
        out_specs=rows(d),
        compiler_params=_cparams("parallel"),
        name="tail",
    )(y_ssm, y_attn, proj, proj, x, p, w_so, w_ao, w_o, w_ple, w_gate, g_ple, g_post, g_final)


def kernel(x, p, g_mix, w_in, g_q, w_uq, w_uq_idx, g_kidx, a_re, a_im, log_dt, b_re, b_im, c_re, c_im,
           d_skip, w_glu, w_ssm_out, w_attn_out, w_o, g_ple, w_ple_gate, w_ple, g_ple_post, g_final):
    bsz, s, d = x.shape
    assert bsz == 1 and p.shape[0] == 1, "kernel is specialised to one sequence and one layer"
    x = x.reshape(s, d)
    ssm_w = d // 2
    attn_w = N_HEADS * HEAD_DIM
    kvw = N_KV_HEADS * HEAD_DIM
    n_a = 2 * ssm_w + Q_LORA_RANK + 2 * kvw + attn_w
    n_i = IDX_DIM + IDX_HEADS
    w_i = w_in[0, :, n_a:n_a + V7X_LANES].astype(BF16)
    w_g = w_in[0, :, n_a + n_i:]
    row = lambda v: v.reshape(1, -1)

    proj, c_q, kidx, w_idx = _proj(x, row(g_mix), w_in[:1], n_a, 2 * ssm_w, w_g, w_i, row(g_kidx))

    n_chunks = s // S5_CHUNK
    cp, bbt, e, a1, a2 = _s5_operators(a_re[0], a_im[0], log_dt[0], b_re[0], b_im[0], c_re[0], c_im[0],
                                       levels=n_chunks.bit_length() - 1)
    y = _s5(proj, ssm_w, cp, bbt, e, a1, a2)
    y_ssm = _s5_post(y, proj, row(d_skip), w_glu[0].astype(BF16))

    w_q = jnp.concatenate([w_uq[0], w_uq_idx[0]], axis=1).astype(BF16)
    q, qi = _qproj(c_q, 0, row(g_q), w_q)
    k_off = 2 * ssm_w + Q_LORA_RANK
    z_block = (k_off + 2 * kvw) // attn_w
    y_attn = _dsa(q, qi, w_idx, proj, z_block, k_off // kvw, kidx)

    out = _tail(y_ssm, y_attn, proj, n_a, x, p[0, 0], w_ssm_out[0].astype(BF16), w_attn_out[0].astype(BF16),
                w_o[0].astype(BF16), w_ple[0].astype(BF16), w_ple_gate[0].astype(BF16),
                row(g_ple), row(g_ple_post), row(g_final))
    return out.reshape(bsz, s, d)
```

```python
import functools
import math

import numpy as np
import jax
import jax.numpy as jnp
from jax import lax
from jax.experimental import pallas as pl
from jax.experimental.pallas import tpu as pltpu

F32 = jnp.float32
BF16 = jnp.bfloat16
I32 = jnp.int32

EPS = 1e-6
SSM_GROUP = 16
SSM_STATE = 64
N_HEADS = 8
N_KV_HEADS = 2
HEAD_DIM = 128
Q_LORA_RANK = 512
IDX_HEADS = 16
IDX_DIM = 64
TOPK_MAX = 256

V7X_LANES = 128
V7X_VMEM_LIMIT = 56 * 1024 * 1024

S5_CHUNK = 32
DSA_TQ = 256
DSA_TK = 512
SEARCH_ROWS = 128

INT_MIN = -(2 ** 31)
KEY_NEG_INF = -2139095041
NEG_INIT = -1e30

_NT = (((1,), (1,)), ((), ()))


def _cparams(*sem):
    return pltpu.CompilerParams(dimension_semantics=sem, vmem_limit_bytes=V7X_VMEM_LIMIT)


def _rms(x, g):
    ms = jnp.mean(x * x, axis=-1, keepdims=True)
    return x * lax.rsqrt(ms + EPS) * g


def _sigmoid(x):
    return 1.0 / (1.0 + jnp.exp(-x))


def _gelu_tanh(x):
    c = math.sqrt(2.0 / math.pi)
    return 0.5 * x * (1.0 + jnp.tanh(c * (x + 0.044715 * (x * x * x))))


def _proj_kernel(x_ref, g_ref, wa_ref, wg_ref, wi_ref, gk_ref, ob_ref, cq_ref, ki_ref, widx_ref, h_ref,
                 *, n_a_tiles, cq_tile, n_cq_tiles, w_scale):
    j = pl.program_id(1)

    @pl.when(j == 0)
    def _():
        h = _rms(x_ref[...], g_ref[...]).astype(BF16)
        h_ref[...] = h
        r = lax.dot_general(h, wi_ref[...].astype(BF16), _NT, preferred_element_type=F32)
        ki_ref[...] = _rms(r[:, :IDX_DIM], gk_ref[...]).astype(BF16)
        widx_ref[...] = r[:, IDX_DIM:IDX_DIM + IDX_HEADS] * w_scale

    @pl.when(j < n_a_tiles)
    def _():
        acc = lax.dot_general(h_ref[...], wa_ref[...].astype(BF16), _NT, preferred_element_type=F32)
        ob_ref[...] = acc.astype(BF16)

        @pl.when(jnp.logical_and(j >= cq_tile, j < cq_tile + n_cq_tiles))
        def _():
            cq_ref[...] = acc

    @pl.when(j >= n_a_tiles)
    def _():
        acc = lax.dot_general(h_ref[...], wg_ref[0].astype(BF16), _NT, preferred_element_type=F32)
        ob_ref[...] = _sigmoid(acc).astype(BF16)


def _proj(x, g, w_t, n_a, cq_col, gate_row, gk, *, tm=2048, tn=256):
    s, d = x.shape
    tm = min(tm, s)
    n_g = w_t.shape[1] - gate_row
    n_a_tiles = n_a // tn
    cq_tile, n_cq_tiles = cq_col // tn, Q_LORA_RANK // tn
    return pl.pallas_call(
        functools.partial(_proj_kernel, n_a_tiles=n_a_tiles, cq_tile=cq_tile, n_cq_tiles=n_cq_tiles,
                          w_scale=(IDX_HEADS ** -0.5) * (IDX_DIM ** -0.5)),
        out_shape=(jax.ShapeDtypeStruct((s, n_a + n_g), BF16),
                   jax.ShapeDtypeStruct((s, Q_LORA_RANK), F32),
                   jax.ShapeDtypeStruct((s, IDX_DIM), BF16),
                   jax.ShapeDtypeStruct((s, IDX_HEADS), F32)),
        grid=(s // tm, (n_a + n_g) // tn),
        in_specs=[
            pl.BlockSpec((tm, d), lambda i, j: (i, 0), pipeline_mode=pl.Buffered(1)),
            pl.BlockSpec((1, d), lambda i, j: (0, 0)),
            pl.BlockSpec((None, tn, d), lambda i, j: (0, jnp.minimum(j, n_a_tiles - 1), 0)),
            pl.BlockSpec((pl.Element(1), pl.Element(tn), pl.Element(d)),
                         lambda i, j: (0, pl.multiple_of(gate_row + jnp.maximum(j - n_a_tiles, 0) * tn, 8), 0)),
            pl.BlockSpec((None, V7X_LANES, d), lambda i, j: (0, n_a // V7X_LANES, 0)),
            pl.BlockSpec((1, IDX_DIM), lambda i, j: (0, 0)),
        ],
        out_specs=(pl.BlockSpec((tm, tn), lambda i, j: (i, j)),
                   pl.BlockSpec((tm, tn), lambda i, j: (i, jnp.clip(j - cq_tile, 0, n_cq_tiles - 1))),
                   pl.BlockSpec((tm, IDX_DIM), lambda i, j: (i, 0)),
                   pl.BlockSpec((tm, IDX_HEADS), lambda i, j: (i, 0))),
        scratch_shapes=[pltpu.VMEM((tm, d), BF16)],
        compiler_params=_cparams("parallel", "arbitrary"),
        name="proj",
    )(x, g, w_t, w_t, w_t, gk)


def _qproj_kernel(c_ref, g_ref, w_ref, q_ref, qi_ref, *, q_scale):
    cq = _rms(c_ref[...], g_ref[...]).astype(BF16)
    r = jnp.dot(cq, w_ref[...], preferred_element_type=F32)
    for h in range(N_HEADS):
        q_ref[h] = (r[:, h * HEAD_DIM:(h + 1) * HEAD_DIM] * q_scale).astype(BF16)
    base = N_HEADS * HEAD_DIM
    for h in range(IDX_HEADS):
        qi_ref[h] = r[:, base + h * IDX_DIM: base + (h + 1) * IDX_DIM].astype(BF16)


def _qproj(proj_a, cq_block, g, w, *, tm=512):
    s = proj_a.shape[0]
    return pl.pallas_call(
        functools.partial(_qproj_kernel, q_scale=HEAD_DIM ** -0.5 * math.log2(math.e)),
        out_shape=(jax.ShapeDtypeStruct((N_HEADS, s, HEAD_DIM), BF16),
                   jax.ShapeDtypeStruct((IDX_HEADS, s, IDX_DIM), BF16)),
        grid=(s // tm,),
        in_specs=[
            pl.BlockSpec((tm, Q_LORA_RANK), lambda i: (i, cq_block)),
            pl.BlockSpec((1, Q_LORA_RANK), lambda i: (0, 0)),
            pl.BlockSpec(w.shape, lambda i: (0, 0)),
        ],
        out_specs=(pl.BlockSpec((N_HEADS, tm, HEAD_DIM), lambda i: (0, i, 0)),
                   pl.BlockSpec((IDX_HEADS, tm, IDX_DIM), lambda i: (0, i, 0))),
        compiler_params=_cparams("parallel"),
        name="qproj",
    )(proj_a, g, w)


def _cmul(ar, ai, br, bi):
    return ar * br - ai * bi, ar * bi + ai * br


def _s5_operators(a_re, a_im, log_dt, b_re, b_im, c_re, c_im, levels):
    t_len = S5_CHUNK
    g_n, n_st = a_re.shape
    dt = jnp.exp(log_dt)[:, None]
    mag = jnp.exp(dt * a_re)
    abr = mag * jnp.cos(dt * a_im)
    abi = mag * jnp.sin(dt * a_im)
    den = a_re * a_re + a_im * a_im
    nr = abr - 1.0
    f_re = (nr * a_re + abi * a_im) / den
    f_im = (abi * a_re - nr * a_im) / den
    bt_re = b_re.transpose(0, 2, 1)
    bt_im = b_im.transpose(0, 2, 1)
    bbt_re = f_re[:, None, :] * bt_re - f_im[:, None, :] * bt_im
    bbt_im = f_re[:, None, :] * bt_im + f_im[:, None, :] * bt_re

    nbits = t_len.bit_length()
    sq = [(abr, abi)]
    for _ in range(nbits - 1):
        sq.append(_cmul(*sq[-1], *sq[-1]))
    j = np.arange(t_len + 1)
    pr = jnp.ones((g_n, t_len + 1, n_st), F32)
    pi = jnp.zeros((g_n, t_len + 1, n_st), F32)
    for b in range(nbits):
        bit = jnp.asarray(((j >> b) & 1).astype(bool))[None, :, None]
        mr, mi = _cmul(pr, pi, sq[b][0][:, None, :], sq[b][1][:, None, :])
        pr = jnp.where(bit, mr, pr)
        pi = jnp.where(bit, mi, pi)

    prj, pij = pr[:, :, None, :], pi[:, :, None, :]
    cp_re = c_re[:, None] * prj - c_im[:, None] * pij
    cp_im = c_re[:, None] * pij + c_im[:, None] * prj
    cp = jnp.concatenate([cp_re, -cp_im], axis=-1).reshape(g_n, (t_len + 1) * SSM_GROUP, 2 * n_st)

    prr, pir = prj[:, t_len - 1::-1], pij[:, t_len - 1::-1]
    e_re = prr * bbt_re[:, None] - pir * bbt_im[:, None]
    e_im = prr * bbt_im[:, None] + pir * bbt_re[:, None]
    e = jnp.concatenate([e_re, e_im], axis=-1).reshape(g_n, t_len * SSM_GROUP, 2 * n_st)

    lv_r, lv_i = [pr[:, t_len]], [pi[:, t_len]]
    for _ in range(levels - 1):
        r2, i2 = _cmul(lv_r[-1], lv_i[-1], lv_r[-1], lv_i[-1])
        lv_r.append(r2)
        lv_i.append(i2)
    ar = jnp.stack(lv_r, axis=1)
    ai = jnp.stack(lv_i, axis=1)
    a1 = jnp.concatenate([ar, ar], axis=-1)
    a2 = jnp.concatenate([-ai, ai], axis=-1)
    return cp, jnp.concatenate([bbt_re, bbt_im], axis=-1), e.astype(BF16), a1, a2


def _s5_kernel(*refs, t_len):
    gpt = V7X_LANES // SSM_GROUP
    (u_ref, cp_ref, bbt_ref, e_ref, a1_ref, a2_ref, y_ref,
     rep_ref, master_ref, ebd_ref, fbd_ref, tm_ref) = refs
    n_chunks = u_ref.shape[0] // t_len
    ct = t_len * SSM_GROUP
    n2 = cp_ref.shape[2]
    pair = 2 * V7X_LANES

    @pl.when(pl.program_id(0) == 0)
    def _():
        ebd_ref[...] = jnp.zeros(ebd_ref.shape, BF16)
        fbd_ref[...] = jnp.zeros(fbd_ref.shape, BF16)
        r = lax.broadcasted_iota(I32, rep_ref.shape, 0)
        q = lax.broadcasted_iota(I32, rep_ref.shape, 1)
        same_lag = (r // SSM_GROUP) == (q // V7X_LANES)
        same_ch = (r % SSM_GROUP) == (q % SSM_GROUP)
        rep_ref[...] = jnp.where(same_lag, jnp.where(same_ch, 1.0, 0.0), 0.0).astype(BF16)

    kt8 = jnp.concatenate(
        [lax.dot_general(bbt_ref[k], cp_ref[k, :ct, :], _NT, precision=lax.Precision.HIGHEST,
                         preferred_element_type=F32) for k in range(gpt)], axis=0)
    dall = jnp.dot(kt8.astype(BF16), rep_ref[...], preferred_element_type=F32)
    rg = lax.broadcasted_iota(I32, dall.shape, 0) // SSM_GROUP
    lg = (lax.broadcasted_iota(I32, dall.shape, 1) % V7X_LANES) // SSM_GROUP
    dall = jnp.where(rg == lg, dall, 0.0).astype(BF16)
    for b in range(t_len - 1):
        r0 = (t_len - 2 - b) * V7X_LANES
        master_ref[b * V7X_LANES:(b + 1) * V7X_LANES, :] = dall[:, r0:r0 + pair]
    master_ref[(t_len - 1) * V7X_LANES:, :] = jnp.concatenate(
        [jnp.zeros((V7X_LANES, V7X_LANES), BF16), dall[:, :V7X_LANES]], axis=1)
    for s in range(t_len):
        for k in range(gpt):
            rows = slice(s * V7X_LANES + k * SSM_GROUP, s * V7X_LANES + (k + 1) * SSM_GROUP)
            cols = slice(k * n2, (k + 1) * n2)
            ebd_ref[rows, cols] = e_ref[k, s * SSM_GROUP:(s + 1) * SSM_GROUP, :]
            fbd_ref[rows, cols] = cp_ref[k, (s + 1) * SSM_GROUP:(s + 2) * SSM_GROUP, :].astype(BF16)

    tm_ref[...] = u_ref[...].astype(F32)
    u = jnp.concatenate([tm_ref[pl.ds(s, n_chunks, stride=t_len), :].astype(BF16) for s in range(t_len)],
                        axis=1)
    hend = jnp.dot(u, ebd_ref[...], preferred_element_type=F32)
    row = lax.broadcasted_iota(I32, (hend.shape[0], n2), 0)
    carries = []
    for k in range(gpt):
        p = hend[:, k * n2:(k + 1) * n2]
        for lv in range(a1_ref.shape[1]):
            sh = 1 << lv
            a1 = a1_ref[k, lv:lv + 1, :]
            a2 = a2_ref[k, lv:lv + 1, :]
            ps = jnp.where(row >= sh, pltpu.roll(p, sh, axis=0), 0.0)
            p = p + a1 * ps + a2 * pltpu.roll(ps, n2 // 2, axis=1)
        carries.append(jnp.where(row >= 1, pltpu.roll(p, 1, axis=0), 0.0).astype(BF16))
    carry = jnp.concatenate(carries, axis=1)

    for tp in range(t_len // 2):
        kdim = (tp + 1) * pair
        y2 = jnp.dot(u[:, :kdim], master_ref[(t_len - 2 - 2 * tp) * V7X_LANES:, :],
                     preferred_element_type=F32)
        y2 = y2 + lax.dot_general(carry, fbd_ref[tp * pair:(tp + 1) * pair, :], _NT,
                                  preferred_element_type=F32)
        tm_ref[(2 * tp) * n_chunks:(2 * tp + 1) * n_chunks, :] = y2[:, :V7X_LANES]
        tm_ref[(2 * tp + 1) * n_chunks:(2 * tp + 2) * n_chunks, :] = y2[:, V7X_LANES:]

    for c in range(n_chunks):
        y_ref[c * t_len:(c + 1) * t_len, :] = tm_ref[pl.ds(c, t_len, stride=n_chunks), :]


def _s5(proj, ssm_w, cp, bbt, e, a1, a2):
    s, width = proj.shape
    t_len = S5_CHUNK
    n_chunks = s // t_len
    gpt = V7X_LANES // SSM_GROUP
    ct = t_len * SSM_GROUP
    n2 = e.shape[-1]
    levels = a1.shape[1]
    assert n_chunks == 1 << levels and n2 == V7X_LANES
    grp = lambda rows: pl.BlockSpec((gpt, rows, n2), lambda g: (g, 0, 0))
    col = pl.BlockSpec((s, V7X_LANES), lambda g: (0, g))
    return pl.pallas_call(
        functools.partial(_s5_kernel, t_len=t_len),
        out_shape=jax.ShapeDtypeStruct((s, ssm_w), F32),
        grid=(ssm_w // V7X_LANES,),
        in_specs=[col, grp(ct + SSM_GROUP), grp(SSM_GROUP), grp(ct), grp(levels), grp(levels)],
        out_specs=col,
        scratch_shapes=[
            pltpu.VMEM((ct, t_len * V7X_LANES), BF16),
            pltpu.VMEM((t_len * V7X_LANES, 2 * V7X_LANES), BF16),
            pltpu.VMEM((t_len * V7X_LANES, gpt * n2), BF16),
            pltpu.VMEM((t_len * V7X_LANES, gpt * n2), BF16),
            pltpu.VMEM((s, V7X_LANES), F32),
        ],
        compiler_params=_cparams("arbitrary"),
        name="s5",
    )(proj, cp, bbt, e, a1, a2)


def _s5_post_kernel(y_ref, u_ref, z_ref, d_ref, w_ref, o_ref):
    width = y_ref.shape[1]
    yy = _gelu_tanh(y_ref[...] + d_ref[...] * u_ref[...].astype(F32))
    r = jnp.dot(yy.astype(BF16), w_ref[...], preferred_element_type=F32)
    glu = r[:, :width] * _sigmoid(r[:, width:])
    z = z_ref[...].astype(F32)
    o_ref[...] = (glu * (z * _sigmoid(z))).astype(BF16)


def _s5_post(y, proj_a, d_skip, w_glu, *, tm=512):
    s, width = y.shape
    return pl.pallas_call(
        _s5_post_kernel,
        out_shape=jax.ShapeDtypeStruct((s, width), BF16),
        grid=(s // tm,),
        in_specs=[
            pl.BlockSpec((tm, width), lambda i: (i, 0)),
            pl.BlockSpec((tm, width), lambda i: (i, 0)),
            pl.BlockSpec((tm, width), lambda i: (i, 1)),
            pl.BlockSpec((1, width), lambda i: (0, 0)),
            pl.BlockSpec(w_glu.shape, lambda i: (0, 0)),
        ],
        out_specs=pl.BlockSpec((tm, width), lambda i: (i, 0)),
        compiler_params=_cparams("parallel"),
        name="s5_post",
    )(y, proj_a, proj_a, d_skip, w_glu)


def _key_to_float(key):
    bits = key ^ (lax.shift_right_arithmetic(key, 31) & 0x7FFFFFFF)
    return lax.bitcast_convert_type(bits, F32)


def _dsa_kernel(q_ref, qi_ref, w_ref, z_ref, k_ref, v_ref, ki_ref, o_ref,
                sc_ref, wb_ref, res_ref, cand_ref, cnt_ref, m_ref, l_ref, acc_ref, *, topk):
    tq = DSA_TQ
    halves = tq // V7X_LANES
    cpt = DSA_TK // V7X_LANES
    rep = N_HEADS // N_KV_HEADS
    i = pl.program_id(0)
    n_valid = halves * (i + 1)
    n_it = (n_valid + cpt - 1) // cpt

    for h in range(IDX_HEADS):
        wb_ref[h] = jnp.broadcast_to(w_ref[:, h:h + 1], (tq, V7X_LANES))
    qi = qi_ref[...].reshape(IDX_HEADS * tq, IDX_DIM)

    def score_tile(kt, diag):
        ki = ki_ref[pl.ds(pl.multiple_of(kt * tq, tq), tq), :]
        r = lax.dot_general(qi, ki, _NT, preferred_element_type=F32)
        for half in range(halves):
            lanes = slice(half * V7X_LANES, (half + 1) * V7X_LANES)
            acc = jnp.zeros((tq, V7X_LANES), F32)
            for h in range(IDX_HEADS):
                acc = acc + wb_ref[h] * jnp.maximum(r[h * tq:(h + 1) * tq, lanes], 0.0)
            if diag:
                col = lax.broadcasted_iota(I32, acc.shape, 1) + half * V7X_LANES
                rowi = lax.broadcasted_iota(I32, acc.shape, 0)
                acc = jnp.where(col <= rowi, acc, -jnp.inf)
            sc_ref[halves * kt + half] = acc

    def score_body(kt, c):
        score_tile(kt, False)
        return c

    lax.fori_loop(0, i, score_body, 0)
    score_tile(i, True)

    for j in range(cpt - halves):
        @pl.when(n_valid + j < n_it * cpt)
        def _():
            sc_ref[n_valid + j] = jnp.full((tq, V7X_LANES), -jnp.inf, F32)

    res_ref[...] = jnp.full(res_ref.shape, INT_MIN, I32)

    def pass_body(p, c):
        bit = lax.shift_left(jnp.int32(1), 31 - p)
        cand_ref[...] = _key_to_float(res_ref[...] + bit)
        cnts = []
        for rc in range(tq // SEARCH_ROWS):
            rows = slice(rc * SEARCH_ROWS, (rc + 1) * SEARCH_ROWS)
            cand = cand_ref[rows, :]

            def it_body(it, cnt, rows=rows, cand=cand):
                for j in range(cpt):
                    cnt = cnt + jnp.where(sc_ref[cpt * it + j, rows, :] >= cand, 1.0, 0.0)
                return cnt

            cnts.append(lax.fori_loop(0, n_it, it_body, jnp.zeros((SEARCH_ROWS, V7X_LANES), F32)))
        tot = jnp.sum(jnp.concatenate(cnts, axis=0), axis=1, keepdims=True)
        ok = tot >= float(topk)
        res = res_ref[...]
        res_ref[...] = jnp.where(ok, res + bit, res)
        cnt_ref[...] = jnp.where(ok, tot, cnt_ref[...])
        return c

    cnt_ref[...] = (jnp.full(cnt_ref.shape, DSA_TK, I32) * n_it).astype(F32)
    lax.fori_loop(0, 32, pass_body, 0)
    res = res_ref[...]
    tau = _key_to_float(jnp.maximum(res, KEY_NEG_INF + 1))
    cand_ref[...] = tau

    excess = jnp.where(res > KEY_NEG_INF, cnt_ref[...] - float(topk), 0.0).astype(I32)
    lane = lax.broadcasted_iota(I32, (tq, V7X_LANES), 1)
    n_chunks = n_it * cpt

    def drop_body(r, c):
        active = excess > r

        def min_body(ch, mv):
            sc = sc_ref[ch]
            return jnp.minimum(mv, jnp.where(sc >= tau, sc, jnp.inf))

        mv = lax.fori_loop(0, n_chunks, min_body, jnp.full((tq, V7X_LANES), jnp.inf, F32))
        mval = jnp.min(mv, axis=1, keepdims=True)

        def idx_body(ch, ix):
            col = (lane + ch * V7X_LANES).astype(F32)
            return jnp.maximum(ix, jnp.where(sc_ref[ch] == mval, col, -1.0))

        ix = lax.fori_loop(0, n_chunks, idx_body, jnp.full((tq, V7X_LANES), -1.0, F32))
        idx = jnp.max(ix, axis=1, keepdims=True)

        def drop_chunk(ch, c2):
            col = (lane + ch * V7X_LANES).astype(F32)
            sc = sc_ref[ch]
            sc_ref[ch] = jnp.where(active, jnp.where(col == idx, -jnp.inf, sc), sc)
            return c2

        lax.fori_loop(0, n_chunks, drop_chunk, 0)
        return c

    lax.fori_loop(0, jnp.max(excess), drop_body, 0)

    m_ref[...] = jnp.full(m_ref.shape, NEG_INIT, F32)
    l_ref[...] = jnp.zeros(l_ref.shape, F32)
    acc_ref[...] = jnp.zeros(acc_ref.shape, F32)
    ones = jnp.ones((DSA_TK, HEAD_DIM), BF16)

    def attn_body(it, c):
        tau = cand_ref[...]
        neg = jnp.concatenate(
            [jnp.where(sc_ref[cpt * it + j] >= tau, 0.0, -jnp.inf) for j in range(cpt)], axis=1)
        ks = pl.ds(pl.multiple_of(it * DSA_TK, DSA_TK), DSA_TK)
        for g in range(N_KV_HEADS):
            hs = slice(g * rep, (g + 1) * rep)
            cols = slice(g * HEAD_DIM, (g + 1) * HEAD_DIM)
            qg = q_ref[hs].reshape(rep * tq, HEAD_DIM)
            s = lax.dot_general(qg, k_ref[ks, cols], _NT, preferred_element_type=F32)
            s = s.reshape(rep, tq, DSA_TK) + neg[None]
            m_prev = m_ref[hs]
            m_new = jnp.maximum(m_prev, jnp.max(s, axis=2, keepdims=True))
            alpha = jnp.exp2(m_prev - m_new)
            p = jnp.concatenate(
                [jnp.exp2(s[..., j * V7X_LANES:(j + 1) * V7X_LANES] - m_new) for j in range(cpt)], axis=2)
            v_aug = jnp.concatenate([v_ref[ks, cols], ones], axis=1)
            pv = jnp.dot(p.reshape(rep * tq, DSA_TK).astype(BF16), v_aug,
                         preferred_element_type=F32).reshape(rep, tq, 2 * HEAD_DIM)
            acc_ref[hs] = alpha * acc_ref[hs] + pv[..., :HEAD_DIM]
            l_ref[hs] = alpha * l_ref[hs] + pv[..., HEAD_DIM:]
            m_ref[hs] = m_new
        return c

    lax.fori_loop(0, n_it, attn_body, 0)

    for h in range(N_HEADS):
        cols = slice(h * HEAD_DIM, (h + 1) * HEAD_DIM)
        z = z_ref[:, cols].astype(F32)
        o_ref[:, cols] = ((acc_ref[h] / l_ref[h]) * (z * _sigmoid(z))).astype(BF16)


def _dsa(q, qi, w_idx, proj, z_block, k_block, kidx):
    s = kidx.shape[0]
    tq = DSA_TQ
    width = N_HEADS * HEAD_DIM
    kvw = N_KV_HEADS * HEAD_DIM
    topk = min(TOPK_MAX, s // 4)
    held = pl.Buffered(1)
    return pl.pallas_call(
        functools.partial(_dsa_kernel, topk=topk),
        out_shape=jax.ShapeDtypeStruct((s, width), BF16),
        grid=(s // tq,),
        in_specs=[
            pl.BlockSpec((N_HEADS, tq, HEAD_DIM), lambda i: (0, i, 0)),
            pl.BlockSpec((IDX_HEADS, tq, IDX_DIM), lambda i: (0, i, 0)),
            pl.BlockSpec((tq, IDX_HEADS), lambda i: (i, 0)),
            pl.BlockSpec((tq, width), lambda i: (i, z_block)),
            pl.BlockSpec((s, kvw), lambda i: (0, k_block), pipeline_mode=held),
            pl.BlockSpec((s, kvw), lambda i: (0, k_block + 1), pipeline_mode=held),
            pl.BlockSpec((s, IDX_DIM), lambda i: (0, 0), pipeline_mode=held),
        ],
        out_specs=pl.BlockSpec((tq, width), lambda i: (i, 0)),
        scratch_shapes=[
            pltpu.VMEM((s // V7X_LANES, tq, V7X_LANES), F32),
            pltpu.VMEM((IDX_HEADS, tq, V7X_LANES), F32),
            pltpu.VMEM((tq, V7X_LANES), I32),
            pltpu.VMEM((tq, V7X_LANES), F32),
            pltpu.VMEM((tq, V7X_LANES), F32),
            pltpu.VMEM((N_HEADS, tq, V7X_LANES), F32),
            pltpu.VMEM((N_HEADS, tq, V7X_LANES), F32),
            pltpu.VMEM((N_HEADS, tq, HEAD_DIM), F32),
        ],
        compiler_params=_cparams("arbitrary"),
        name="dsa",
    )(q, qi, w_idx, proj, proj, proj, kidx)


def _tail_kernel(ys_ref, ya_ref, gs_ref, ga_ref, x_ref, p_ref, wso_ref, wao_ref, wo_ref, wp_ref, wg_ref,
                 gple_ref, gpost_ref, gfin_ref, o_ref):
    o_s = jnp.dot(ys_ref[...], wso_ref[...], preferred_element_type=F32)
    o_a = jnp.dot(ya_ref[...], wao_ref[...], preferred_element_type=F32)
    merged = (gs_ref[...].astype(F32) * o_s + ga_ref[...].astype(F32) * o_a).astype(BF16)
    x1 = x_ref[...] + jnp.dot(merged, wo_ref[...], preferred_element_type=F32)
    e = _rms(jnp.dot(p_ref[...].astype(BF16), wp_ref[...], preferred_element_type=F32), gpost_ref[...])
    xn = _rms(x1, gple_ref[...]).astype(BF16)
    gate = _sigmoid(jnp.dot(xn, wg_ref[...], preferred_element_type=F32))
    o_ref[...] = _rms(x1 + gate * e, gfin_ref[...])


def _tail(y_ssm, y_attn, proj, gate_col, x, p, w_so, w_ao, w_o, w_ple, w_gate, g_ple, g_post, g_final,
          *, tm=256):
    s, d = x.shape
    kdim = y_ssm.shape[1]
    pd = p.shape[1]
    g0 = gate_col // d
    rows = lambda width, blk=0: pl.BlockSpec((tm, width), lambda i: (i, blk))
    held = lambda shape: pl.BlockSpec(shape, lambda i: (0, 0), pipeline_mode=pl.Buffered(1))
    vec = pl.BlockSpec((1, d), lambda i: (0, 0))
    return pl.pallas_call(
        _tail_kernel,
        out_shape=jax.ShapeDtypeStruct((s, d), F32),
        grid=(s // tm,),
        in_specs=[rows(kdim), rows(kdim), rows(d, g0), rows(d, g0 + 1), rows(d), rows(pd),
                  held((kdim, d)), held((kdim, d)), held((d, d)), held((pd, d)), held((d, d)),
                  vec, vec, vec],
        out_specs=rows(d),
        compiler_params=_cparams("parallel"),
        name="tail",
    )(y_ssm, y_attn, proj, proj, x, p, w_so, w_ao, w_o, w_ple, w_gate, g_ple, g_post, g_final)


def kernel(x, p, g_mix, w_in, g_q, w_uq, w_uq_idx, g_kidx, a_re, a_im, log_dt, b_re, b_im, c_re, c_im,
           d_skip, w_glu, w_ssm_out, w_attn_out, w_o, g_ple, w_ple_gate, w_ple, g_ple_post, g_final):
    bsz, s, d = x.shape
    assert bsz == 1 and p.shape[0] == 1, "kernel is specialised to one sequence and one layer"
    x = x.reshape(s, d)
    ssm_w = d // 2
    attn_w = N_HEADS * HEAD_DIM
    kvw = N_KV_HEADS * HEAD_DIM
    n_a = 2 * ssm_w + Q_LORA_RANK + 2 * kvw + attn_w
    n_i = IDX_DIM + IDX_HEADS
    row = lambda v: v.reshape(1, -1)

    proj, c_q, kidx, w_idx = _proj(x, row(g_mix), jnp.swapaxes(w_in[:1], 1, 2), n_a, 2 * ssm_w,
                                   n_a + n_i, row(g_kidx))

    n_chunks = s // S5_CHUNK
    cp, bbt, e, a1, a2 = _s5_operators(a_re[0], a_im[0], log_dt[0], b_re[0], b_im[0], c_re[0], c_im[0],
                                       levels=n_chunks.bit_length() - 1)
    y = _s5(proj, ssm_w, cp, bbt, e, a1, a2)
    y_ssm = _s5_post(y, proj, row(d_skip), w_glu[0].astype(BF16))

    w_q = jnp.concatenate([w_uq[0], w_uq_idx[0]], axis=1).astype(BF16)
    q, qi = _qproj(c_q, 0, row(g_q), w_q)
    k_off = 2 * ssm_w + Q_LORA_RANK
    z_block = (k_off + 2 * kvw) // attn_w
    y_attn = _dsa(q, qi, w_idx, proj, z_block, k_off // kvw, kidx)

    out = _tail(y_ssm, y_attn, proj, n_a, x, p[0, 0], w_ssm_out[0].astype(BF16), w_attn_out[0].astype(BF16),
                w_o[0].astype(BF16), w_ple[0].astype(BF16), w_ple_gate[0].astype(BF16),
                row(g_ple), row(g_ple_post), row(g_final))
    return out.reshape(bsz, s, d)
```

```python
import functools
import math

import numpy as np
import jax
import jax.numpy as jnp
from jax import lax
from jax.experimental import pallas as pl
from jax.experimental.pallas import tpu as pltpu

F32 = jnp.float32
BF16 = jnp.bfloat16
I32 = jnp.int32

EPS = 1e-6
SSM_GROUP = 16
SSM_STATE = 64
N_HEADS = 8
N_KV_HEADS = 2
HEAD_DIM = 128
Q_LORA_RANK = 512
IDX_HEADS = 16
IDX_DIM = 64
TOPK_MAX = 256

V7X_LANES = 128
V7X_VMEM_LIMIT = 56 * 1024 * 1024

S5_CHUNK = 32
DSA_TQ = 256
DSA_TK = 512
SEARCH_ROWS = 128
ATTN_HEADS = 2
SCORE_HEADS = 2

INT_MIN = -(2 ** 31)
KEY_NEG_INF = -2139095041
NEG_INIT = -1e30

_NT = (((1,), (1,)), ((), ()))


def _cparams(*sem):
    return pltpu.CompilerParams(dimension_semantics=sem, vmem_limit_bytes=V7X_VMEM_LIMIT)


def _rms(x, g):
    ms = jnp.mean(x * x, axis=-1, keepdims=True)
    return x * lax.rsqrt(ms + EPS) * g


def _sigmoid(x):
    return 1.0 / (1.0 + jnp.exp(-x))


def _gelu_tanh(x):
    c = math.sqrt(2.0 / math.pi)
    return 0.5 * x * (1.0 + jnp.tanh(c * (x + 0.044715 * (x * x * x))))


def _proj_kernel(x_ref, g_ref, wa_ref, wg_ref, wi_ref, gk_ref, ob_ref, cq_ref, ki_ref, widx_ref, h_ref,
                 *, n_a_tiles, cq_tile, n_cq_tiles, w_scale):
    j = pl.program_id(1)

    @pl.when(j == 0)
    def _():
        h = _rms(x_ref[...], g_ref[...]).astype(BF16)
        h_ref[...] = h
        r = lax.dot_general(h, wi_ref[...].astype(BF16), _NT, preferred_element_type=F32)
        ki_ref[...] = _rms(r[:, :IDX_DIM], gk_ref[...]).astype(BF16)
        widx_ref[...] = r[:, IDX_DIM:IDX_DIM + IDX_HEADS] * w_scale

    @pl.when(j < n_a_tiles)
    def _():
        acc = lax.dot_general(h_ref[...], wa_ref[...].astype(BF16), _NT, preferred_element_type=F32)
        ob_ref[...] = acc.astype(BF16)

        @pl.when(jnp.logical_and(j >= cq_tile, j < cq_tile + n_cq_tiles))
        def _():
            cq_ref[...] = acc

    @pl.when(j >= n_a_tiles)
    def _():
        acc = lax.dot_general(h_ref[...], wg_ref[0].astype(BF16), _NT, preferred_element_type=F32)
        ob_ref[...] = _sigmoid(acc).astype(BF16)


def _proj(x, g, w_t, n_a, cq_col, gate_row, gk, *, tm=2048, tn=256):
    s, d = x.shape
    tm = min(tm, s)
    n_g = w_t.shape[1] - gate_row
    n_a_tiles = n_a // tn
    cq_tile, n_cq_tiles = cq_col // tn, Q_LORA_RANK // tn
    return pl.pallas_call(
        functools.partial(_proj_kernel, n_a_tiles=n_a_tiles, cq_tile=cq_tile, n_cq_tiles=n_cq_tiles,
                          w_scale=(IDX_HEADS ** -0.5) * (IDX_DIM ** -0.5)),
        out_shape=(jax.ShapeDtypeStruct((s, n_a + n_g), BF16),
                   jax.ShapeDtypeStruct((s, Q_LORA_RANK), F32),
                   jax.ShapeDtypeStruct((s, IDX_DIM), BF16),
                   jax.ShapeDtypeStruct((s, IDX_HEADS), F32)),
        grid=(s // tm, (n_a + n_g) // tn),
        in_specs=[
            pl.BlockSpec((tm, d), lambda i, j: (i, 0), pipeline_mode=pl.Buffered(1)),
            pl.BlockSpec((1, d), lambda i, j: (0, 0)),
            pl.BlockSpec((None, tn, d), lambda i, j: (0, jnp.minimum(j, n_a_tiles - 1), 0)),
            pl.BlockSpec((pl.Element(1), pl.Element(tn), pl.Element(d)),
                         lambda i, j: (0, pl.multiple_of(gate_row + jnp.maximum(j - n_a_tiles, 0) * tn, 8), 0)),
            pl.BlockSpec((None, V7X_LANES, d), lambda i, j: (0, n_a // V7X_LANES, 0)),
            pl.BlockSpec((1, IDX_DIM), lambda i, j: (0, 0)),
        ],
        out_specs=(pl.BlockSpec((tm, tn), lambda i, j: (i, j)),
                   pl.BlockSpec((tm, tn), lambda i, j: (i, jnp.clip(j - cq_tile, 0, n_cq_tiles - 1))),
                   pl.BlockSpec((tm, IDX_DIM), lambda i, j: (i, 0)),
                   pl.BlockSpec((tm, IDX_HEADS), lambda i, j: (i, 0))),
        scratch_shapes=[pltpu.VMEM((tm, d), BF16)],
        compiler_params=_cparams("parallel", "arbitrary"),
        name="proj",
    )(x, g, w_t, w_t, w_t, gk)


def _qproj_kernel(c_ref, g_ref, w_ref, q_ref, qi_ref, *, q_scale):
    cq = _rms(c_ref[...], g_ref[...]).astype(BF16)
    r = jnp.dot(cq, w_ref[...], preferred_element_type=F32)
    for h in range(N_HEADS):
        q_ref[h] = (r[:, h * HEAD_DIM:(h + 1) * HEAD_DIM] * q_scale).astype(BF16)
    base = N_HEADS * HEAD_DIM
    for h in range(IDX_HEADS):
        qi_ref[h] = r[:, base + h * IDX_DIM: base + (h + 1) * IDX_DIM].astype(BF16)


def _qproj(proj_a, cq_block, g, w, *, tm=512):
    s = proj_a.shape[0]
    return pl.pallas_call(
        functools.partial(_qproj_kernel, q_scale=HEAD_DIM ** -0.5 * math.log2(math.e)),
        out_shape=(jax.ShapeDtypeStruct((N_HEADS, s, HEAD_DIM), BF16),
                   jax.ShapeDtypeStruct((IDX_HEADS, s, IDX_DIM), BF16)),
        grid=(s // tm,),
        in_specs=[
            pl.BlockSpec((tm, Q_LORA_RANK), lambda i: (i, cq_block)),
            pl.BlockSpec((1, Q_LORA_RANK), lambda i: (0, 0)),
            pl.BlockSpec(w.shape, lambda i: (0, 0)),
        ],
        out_specs=(pl.BlockSpec((N_HEADS, tm, HEAD_DIM), lambda i: (0, i, 0)),
                   pl.BlockSpec((IDX_HEADS, tm, IDX_DIM), lambda i: (0, i, 0))),
        compiler_params=_cparams("parallel"),
        name="qproj",
    )(proj_a, g, w)


def _cmul(ar, ai, br, bi):
    return ar * br - ai * bi, ar * bi + ai * br


def _s5_operators(a_re, a_im, log_dt, b_re, b_im, c_re, c_im, levels):
    t_len = S5_CHUNK
    g_n, n_st = a_re.shape
    dt = jnp.exp(log_dt)[:, None]
    mag = jnp.exp(dt * a_re)
    abr = mag * jnp.cos(dt * a_im)
    abi = mag * jnp.sin(dt * a_im)
    den = a_re * a_re + a_im * a_im
    nr = abr - 1.0
    f_re = (nr * a_re + abi * a_im) / den
    f_im = (abi * a_re - nr * a_im) / den
    bt_re = b_re.transpose(0, 2, 1)
    bt_im = b_im.transpose(0, 2, 1)
    bbt_re = f_re[:, None, :] * bt_re - f_im[:, None, :] * bt_im
    bbt_im = f_re[:, None, :] * bt_im + f_im[:, None, :] * bt_re

    nbits = t_len.bit_length()
    sq = [(abr, abi)]
    for _ in range(nbits - 1):
        sq.append(_cmul(*sq[-1], *sq[-1]))
    j = np.arange(t_len + 1)
    pr = jnp.ones((g_n, t_len + 1, n_st), F32)
    pi = jnp.zeros((g_n, t_len + 1, n_st), F32)
    for b in range(nbits):
        bit = jnp.asarray(((j >> b) & 1).astype(bool))[None, :, None]
        mr, mi = _cmul(pr, pi, sq[b][0][:, None, :], sq[b][1][:, None, :])
        pr = jnp.where(bit, mr, pr)
        pi = jnp.where(bit, mi, pi)

    prj, pij = pr[:, :, None, :], pi[:, :, None, :]
    cp_re = c_re[:, None] * prj - c_im[:, None] * pij
    cp_im = c_re[:, None] * pij + c_im[:, None] * prj
    cp = jnp.concatenate([cp_re, -cp_im], axis=-1).reshape(g_n, (t_len + 1) * SSM_GROUP, 2 * n_st)

    prr, pir = prj[:, t_len - 1::-1], pij[:, t_len - 1::-1]
    e_re = prr * bbt_re[:, None] - pir * bbt_im[:, None]
    e_im = prr * bbt_im[:, None] + pir * bbt_re[:, None]
    e = jnp.concatenate([e_re, e_im], axis=-1).reshape(g_n, t_len * SSM_GROUP, 2 * n_st)

    lv_r, lv_i = [pr[:, t_len]], [pi[:, t_len]]
    for _ in range(levels - 1):
        r2, i2 = _cmul(lv_r[-1], lv_i[-1], lv_r[-1], lv_i[-1])
        lv_r.append(r2)
        lv_i.append(i2)
    ar = jnp.stack(lv_r, axis=1)
    ai = jnp.stack(lv_i, axis=1)
    a1 = jnp.concatenate([ar, ar], axis=-1)
    a2 = jnp.concatenate([-ai, ai], axis=-1)
    return cp, jnp.concatenate([bbt_re, bbt_im], axis=-1), e.astype(BF16), a1, a2


def _s5_kernel(*refs, t_len):
    gpt = V7X_LANES // SSM_GROUP
    (u_ref, cp_ref, bbt_ref, e_ref, a1_ref, a2_ref, y_ref,
     rep_ref, master_ref, ebd_ref, fbd_ref, tm_ref) = refs
    n_chunks = u_ref.shape[0] // t_len
    ct = t_len * SSM_GROUP
    n2 = cp_ref.shape[2]
    pair = 2 * V7X_LANES

    @pl.when(pl.program_id(0) == 0)
    def _():
        ebd_ref[...] = jnp.zeros(ebd_ref.shape, BF16)
        fbd_ref[...] = jnp.zeros(fbd_ref.shape, BF16)
        r = lax.broadcasted_iota(I32, rep_ref.shape, 0)
        q = lax.broadcasted_iota(I32, rep_ref.shape, 1)
        same_lag = (r // SSM_GROUP) == (q // V7X_LANES)
        same_ch = (r % SSM_GROUP) == (q % SSM_GROUP)
        rep_ref[...] = jnp.where(same_lag, jnp.where(same_ch, 1.0, 0.0), 0.0).astype(BF16)

    kt8 = jnp.concatenate(
        [lax.dot_general(bbt_ref[k], cp_ref[k, :ct, :], _NT, precision=lax.Precision.HIGHEST,
                         preferred_element_type=F32) for k in range(gpt)], axis=0)
    dall = jnp.dot(kt8.astype(BF16), rep_ref[...], preferred_element_type=F32)
    rg = lax.broadcasted_iota(I32, dall.shape, 0) // SSM_GROUP
    lg = (lax.broadcasted_iota(I32, dall.shape, 1) % V7X_LANES) // SSM_GROUP
    dall = jnp.where(rg == lg, dall, 0.0).astype(BF16)
    for b in range(t_len - 1):
        r0 = (t_len - 2 - b) * V7X_LANES
        master_ref[b * V7X_LANES:(b + 1) * V7X_LANES, :] = dall[:, r0:r0 + pair]
    master_ref[(t_len - 1) * V7X_LANES:, :] = jnp.concatenate(
        [jnp.zeros((V7X_LANES, V7X_LANES), BF16), dall[:, :V7X_LANES]], axis=1)
    for s in range(t_len):
        for k in range(gpt):
            rows = slice(s * V7X_LANES + k * SSM_GROUP, s * V7X_LANES + (k + 1) * SSM_GROUP)
            cols = slice(k * n2, (k + 1) * n2)
            ebd_ref[rows, cols] = e_ref[k, s * SSM_GROUP:(s + 1) * SSM_GROUP, :]
            fbd_ref[rows, cols] = cp_ref[k, (s + 1) * SSM_GROUP:(s + 2) * SSM_GROUP, :].astype(BF16)

    tm_ref[...] = u_ref[...].astype(F32)
    u = jnp.concatenate([tm_ref[pl.ds(s, n_chunks, stride=t_len), :].astype(BF16) for s in range(t_len)],
                        axis=1)
    hend = jnp.dot(u, ebd_ref[...], preferred_element_type=F32)
    row = lax.broadcasted_iota(I32, (hend.shape[0], n2), 0)
    carries = []
    for k in range(gpt):
        p = hend[:, k * n2:(k + 1) * n2]
        for lv in range(a1_ref.shape[1]):
            sh = 1 << lv
            a1 = a1_ref[k, lv:lv + 1, :]
            a2 = a2_ref[k, lv:lv + 1, :]
            ps = jnp.where(row >= sh, pltpu.roll(p, sh, axis=0), 0.0)
            p = p + a1 * ps + a2 * pltpu.roll(ps, n2 // 2, axis=1)
        carries.append(jnp.where(row >= 1, pltpu.roll(p, 1, axis=0), 0.0).astype(BF16))
    carry = jnp.concatenate(carries, axis=1)

    for tp in range(t_len // 2):
        kdim = (tp + 1) * pair
        y2 = jnp.dot(u[:, :kdim], master_ref[(t_len - 2 - 2 * tp) * V7X_LANES:, :],
                     preferred_element_type=F32)
        y2 = y2 + lax.dot_general(carry, fbd_ref[tp * pair:(tp + 1) * pair, :], _NT,
                                  preferred_element_type=F32)
        tm_ref[(2 * tp) * n_chunks:(2 * tp + 1) * n_chunks, :] = y2[:, :V7X_LANES]
        tm_ref[(2 * tp + 1) * n_chunks:(2 * tp + 2) * n_chunks, :] = y2[:, V7X_LANES:]

    for c in range(n_chunks):
        y_ref[c * t_len:(c + 1) * t_len, :] = tm_ref[pl.ds(c, t_len, stride=n_chunks), :]


def _s5(proj, ssm_w, cp, bbt, e, a1, a2):
    s, width = proj.shape
    t_len = S5_CHUNK
    n_chunks = s // t_len
    gpt = V7X_LANES // SSM_GROUP
    ct = t_len * SSM_GROUP
    n2 = e.shape[-1]
    levels = a1.shape[1]
    assert n_chunks == 1 << levels and n2 == V7X_LANES
    grp = lambda rows: pl.BlockSpec((gpt, rows, n2), lambda g: (g, 0, 0))
    col = pl.BlockSpec((s, V7X_LANES), lambda g: (0, g))
    return pl.pallas_call(
        functools.partial(_s5_kernel, t_len=t_len),
        out_shape=jax.ShapeDtypeStruct((s, ssm_w), F32),
        grid=(ssm_w // V7X_LANES,),
        in_specs=[col, grp(ct + SSM_GROUP), grp(SSM_GROUP), grp(ct), grp(levels), grp(levels)],
        out_specs=col,
        scratch_shapes=[
            pltpu.VMEM((ct, t_len * V7X_LANES), BF16),
            pltpu.VMEM((t_len * V7X_LANES, 2 * V7X_LANES), BF16),
            pltpu.VMEM((t_len * V7X_LANES, gpt * n2), BF16),
            pltpu.VMEM((t_len * V7X_LANES, gpt * n2), BF16),
            pltpu.VMEM((s, V7X_LANES), F32),
        ],
        compiler_params=_cparams("arbitrary"),
        name="s5",
    )(proj, cp, bbt, e, a1, a2)


def _s5_post_kernel(y_ref, u_ref, z_ref, d_ref, w_ref, o_ref):
    width = y_ref.shape[1]
    yy = _gelu_tanh(y_ref[...] + d_ref[...] * u_ref[...].astype(F32))
    r = jnp.dot(yy.astype(BF16), w_ref[...], preferred_element_type=F32)
    glu = r[:, :width] * _sigmoid(r[:, width:])
    z = z_ref[...].astype(F32)
    o_ref[...] = (glu * (z * _sigmoid(z))).astype(BF16)


def _s5_post(y, proj_a, d_skip, w_glu, *, tm=512):
    s, width = y.shape
    return pl.pallas_call(
        _s5_post_kernel,
        out_shape=jax.ShapeDtypeStruct((s, width), BF16),
        grid=(s // tm,),
        in_specs=[
            pl.BlockSpec((tm, width), lambda i: (i, 0)),
            pl.BlockSpec((tm, width), lambda i: (i, 0)),
            pl.BlockSpec((tm, width), lambda i: (i, 1)),
            pl.BlockSpec((1, width), lambda i: (0, 0)),
            pl.BlockSpec(w_glu.shape, lambda i: (0, 0)),
        ],
        out_specs=pl.BlockSpec((tm, width), lambda i: (i, 0)),
        compiler_params=_cparams("parallel"),
        name="s5_post",
    )(y, proj_a, proj_a, d_skip, w_glu)


def _key_to_float(key):
    bits = key ^ (lax.shift_right_arithmetic(key, 31) & 0x7FFFFFFF)
    return lax.bitcast_convert_type(bits, F32)


def _dsa_kernel(q_ref, qi_ref, w_ref, z_ref, k_ref, v_ref, ki_ref, o_ref,
                sc_ref, wb_ref, res_ref, cand_ref, cnt_ref, m_ref, l_ref, acc_ref, *, topk):
    tq = DSA_TQ
    halves = tq // V7X_LANES
    cpt = DSA_TK // V7X_LANES
    rep = N_HEADS // N_KV_HEADS
    i = pl.program_id(0)
    n_valid = halves * (i + 1)
    n_it = (n_valid + cpt - 1) // cpt

    for h in range(IDX_HEADS):
        wb_ref[h] = jnp.broadcast_to(w_ref[:, h:h + 1], (tq, V7X_LANES))
    def score_tile(kt, diag):
        ki = ki_ref[pl.ds(pl.multiple_of(kt * tq, tq), tq), :]
        r = [lax.dot_general(qi_ref[g * SCORE_HEADS:(g + 1) * SCORE_HEADS].reshape(SCORE_HEADS * tq, IDX_DIM),
                             ki, _NT, preferred_element_type=F32)
             for g in range(IDX_HEADS // SCORE_HEADS)]
        for half in range(halves):
            lanes = slice(half * V7X_LANES, (half + 1) * V7X_LANES)
            acc = jnp.zeros((tq, V7X_LANES), F32)
            for h in range(IDX_HEADS):
                hg, hl = divmod(h, SCORE_HEADS)
                acc = acc + wb_ref[h] * jnp.maximum(r[hg][hl * tq:(hl + 1) * tq, lanes], 0.0)
            if diag:
                col = lax.broadcasted_iota(I32, acc.shape, 1) + half * V7X_LANES
                rowi = lax.broadcasted_iota(I32, acc.shape, 0)
                acc = jnp.where(col <= rowi, acc, -jnp.inf)
            sc_ref[halves * kt + half] = acc

    def score_body(kt, c):
        score_tile(kt, False)
        return c

    lax.fori_loop(0, i, score_body, 0)
    score_tile(i, True)

    for j in range(cpt - halves):
        @pl.when(n_valid + j < n_it * cpt)
        def _():
            sc_ref[n_valid + j] = jnp.full((tq, V7X_LANES), -jnp.inf, F32)

    res_ref[...] = jnp.full(res_ref.shape, INT_MIN, I32)

    def pass_body(p, c):
        bit = lax.shift_left(jnp.int32(1), 31 - p)
        cand_ref[...] = _key_to_float(res_ref[...] + bit)
        cnts = []
        for rc in range(tq // SEARCH_ROWS):
            rows = slice(rc * SEARCH_ROWS, (rc + 1) * SEARCH_ROWS)
            cand = cand_ref[rows, :]

            def it_body(it, cnt, rows=rows, cand=cand):
                for j in range(cpt):
                    cnt = cnt + jnp.where(sc_ref[cpt * it + j, rows, :] >= cand, 1.0, 0.0)
                return cnt

            cnts.append(lax.fori_loop(0, n_it, it_body, jnp.zeros((SEARCH_ROWS, V7X_LANES), F32)))
        tot = jnp.sum(jnp.concatenate(cnts, axis=0), axis=1, keepdims=True)
        ok = tot >= float(topk)
        res = res_ref[...]
        res_ref[...] = jnp.where(ok, res + bit, res)
        cnt_ref[...] = jnp.where(ok, tot, cnt_ref[...])
        return c

    cnt_ref[...] = (jnp.full(cnt_ref.shape, DSA_TK, I32) * n_it).astype(F32)
    lax.fori_loop(0, 32, pass_body, 0)
    res = res_ref[...]
    tau = _key_to_float(jnp.maximum(res, KEY_NEG_INF + 1))
    cand_ref[...] = tau

    excess = jnp.where(res > KEY_NEG_INF, cnt_ref[...] - float(topk), 0.0).astype(I32)
    lane = lax.broadcasted_iota(I32, (tq, V7X_LANES), 1)
    n_chunks = n_it * cpt

    def drop_body(r, c):
        active = excess > r

        def min_body(ch, mv):
            sc = sc_ref[ch]
            return jnp.minimum(mv, jnp.where(sc >= tau, sc, jnp.inf))

        mv = lax.fori_loop(0, n_chunks, min_body, jnp.full((tq, V7X_LANES), jnp.inf, F32))
        mval = jnp.min(mv, axis=1, keepdims=True)

        def idx_body(ch, ix):
            col = (lane + ch * V7X_LANES).astype(F32)
            return jnp.maximum(ix, jnp.where(sc_ref[ch] == mval, col, -1.0))

        ix = lax.fori_loop(0, n_chunks, idx_body, jnp.full((tq, V7X_LANES), -1.0, F32))
        idx = jnp.max(ix, axis=1, keepdims=True)

        def drop_chunk(ch, c2):
            col = (lane + ch * V7X_LANES).astype(F32)
            sc = sc_ref[ch]
            sc_ref[ch] = jnp.where(active, jnp.where(col == idx, -jnp.inf, sc), sc)
            return c2

        lax.fori_loop(0, n_chunks, drop_chunk, 0)
        return c

    lax.fori_loop(0, jnp.max(excess), drop_body, 0)

    m_ref[...] = jnp.full(m_ref.shape, NEG_INIT, F32)
    l_ref[...] = jnp.zeros(l_ref.shape, F32)
    acc_ref[...] = jnp.zeros(acc_ref.shape, F32)
    ones = jnp.ones((DSA_TK, HEAD_DIM), BF16)

    def attn_body(it, c):
        tau = cand_ref[...]
        neg = jnp.concatenate(
            [jnp.where(sc_ref[cpt * it + j] >= tau, 0.0, -jnp.inf) for j in range(cpt)], axis=1)
        ks = pl.ds(pl.multiple_of(it * DSA_TK, DSA_TK), DSA_TK)
        for part in range(N_HEADS // ATTN_HEADS):
            g = part * ATTN_HEADS // rep
            hs = slice(part * ATTN_HEADS, (part + 1) * ATTN_HEADS)
            cols = slice(g * HEAD_DIM, (g + 1) * HEAD_DIM)
            qg = q_ref[hs].reshape(ATTN_HEADS * tq, HEAD_DIM)
            s = lax.dot_general(qg, k_ref[ks, cols], _NT, preferred_element_type=F32)
            s = s.reshape(ATTN_HEADS, tq, DSA_TK) + neg[None]
            m_prev = m_ref[hs]
            m_new = jnp.maximum(m_prev, jnp.max(s, axis=2, keepdims=True))
            alpha = jnp.exp2(m_prev - m_new)
            p = jnp.concatenate(
                [jnp.exp2(s[..., j * V7X_LANES:(j + 1) * V7X_LANES] - m_new) for j in range(cpt)], axis=2)
            v_aug = jnp.concatenate([v_ref[ks, cols], ones], axis=1)
            pv = jnp.dot(p.reshape(ATTN_HEADS * tq, DSA_TK).astype(BF16), v_aug,
                         preferred_element_type=F32).reshape(ATTN_HEADS, tq, 2 * HEAD_DIM)
            acc_ref[hs] = alpha * acc_ref[hs] + pv[..., :HEAD_DIM]
            l_ref[hs] = alpha * l_ref[hs] + pv[..., HEAD_DIM:]
            m_ref[hs] = m_new
        return c

    lax.fori_loop(0, n_it, attn_body, 0)

    for h in range(N_HEADS):
        cols = slice(h * HEAD_DIM, (h + 1) * HEAD_DIM)
        z = z_ref[:, cols].astype(F32)
        o_ref[:, cols] = ((acc_ref[h] / l_ref[h]) * (z * _sigmoid(z))).astype(BF16)


def _dsa(q, qi, w_idx, proj, z_block, k_block, kidx):
    s = kidx.shape[0]
    tq = DSA_TQ
    width = N_HEADS * HEAD_DIM
    kvw = N_KV_HEADS * HEAD_DIM
    topk = min(TOPK_MAX, s // 4)
    held = pl.Buffered(1)
    return pl.pallas_call(
        functools.partial(_dsa_kernel, topk=topk),
        out_shape=jax.ShapeDtypeStruct((s, width), BF16),
        grid=(s // tq,),
        in_specs=[
            pl.BlockSpec((N_HEADS, tq, HEAD_DIM), lambda i: (0, i, 0)),
            pl.BlockSpec((IDX_HEADS, tq, IDX_DIM), lambda i: (0, i, 0)),
            pl.BlockSpec((tq, IDX_HEADS), lambda i: (i, 0)),
            pl.BlockSpec((tq, width), lambda i: (i, z_block)),
            pl.BlockSpec((s, kvw), lambda i: (0, k_block), pipeline_mode=held),
            pl.BlockSpec((s, kvw), lambda i: (0, k_block + 1), pipeline_mode=held),
            pl.BlockSpec((s, IDX_DIM), lambda i: (0, 0), pipeline_mode=held),
        ],
        out_specs=pl.BlockSpec((tq, width), lambda i: (i, 0)),
        scratch_shapes=[
            pltpu.VMEM((s // V7X_LANES, tq, V7X_LANES), F32),
            pltpu.VMEM((IDX_HEADS, tq, V7X_LANES), F32),
            pltpu.VMEM((tq, V7X_LANES), I32),
            pltpu.VMEM((tq, V7X_LANES), F32),
            pltpu.VMEM((tq, V7X_LANES), F32),
            pltpu.VMEM((N_HEADS, tq, V7X_LANES), F32),
            pltpu.VMEM((N_HEADS, tq, V7X_LANES), F32),
            pltpu.VMEM((N_HEADS, tq, HEAD_DIM), F32),
        ],
        compiler_params=_cparams("arbitrary"),
        name="dsa",
    )(q, qi, w_idx, proj, proj, proj, kidx)


def _tail_kernel(ys_ref, ya_ref, gs_ref, ga_ref, x_ref, p_ref, wso_ref, wao_ref, wo_ref, wp_ref, wg_ref,
                 gple_ref, gpost_ref, gfin_ref, o_ref):
    o_s = jnp.dot(ys_ref[...], wso_ref[...], preferred_element_type=F32)
    o_a = jnp.dot(ya_ref[...], wao_ref[...], preferred_element_type=F32)
    merged = (gs_ref[...].astype(F32) * o_s + ga_ref[...].astype(F32) * o_a).astype(BF16)
    x1 = x_ref[...] + jnp.dot(merged, wo_ref[...], preferred_element_type=F32)
    e = _rms(jnp.dot(p_ref[...].astype(BF16), wp_ref[...], preferred_element_type=F32), gpost_ref[...])
    xn = _rms(x1, gple_ref[...]).astype(BF16)
    gate = _sigmoid(jnp.dot(xn, wg_ref[...], preferred_element_type=F32))
    o_ref[...] = _rms(x1 + gate * e, gfin_ref[...])


def _tail(y_ssm, y_attn, proj, gate_col, x, p, w_so, w_ao, w_o, w_ple, w_gate, g_ple, g_post, g_final,
          *, tm=256):
    s, d = x.shape
    kdim = y_ssm.shape[1]
    pd = p.shape[1]
    g0 = gate_col // d
    rows = lambda width, blk=0: pl.BlockSpec((tm, width), lambda i: (i, blk))
    held = lambda shape: pl.BlockSpec(shape, lambda i: (0, 0), pipeline_mode=pl.Buffered(1))
    vec = pl.BlockSpec((1, d), lambda i: (0, 0))
    return pl.pallas_call(
        _tail_kernel,
        out_shape=jax.ShapeDtypeStruct((s, d), F32),
        grid=(s // tm,),
        in_specs=[rows(kdim), rows(kdim), rows(d, g0), rows(d, g0 + 1), rows(d), rows(pd),
                  held((kdim, d)), held((kdim, d)), held((d, d)), held((pd, d)), held((d, d)),
                  vec, vec, vec],
        out_specs=rows(d),
        compiler_params=_cparams("parallel"),
        name="tail",
    )(y_ssm, y_attn, proj, proj, x, p, w_so, w_ao, w_o, w_ple, w_gate, g_ple, g_post, g_final)


def kernel(x, p, g_mix, w_in, g_q, w_uq, w_uq_idx, g_kidx, a_re, a_im, log_dt, b_re, b_im, c_re, c_im,
           d_skip, w_glu, w_ssm_out, w_attn_out, w_o, g_ple, w_ple_gate, w_ple, g_ple_post, g_final):
    bsz, s, d = x.shape
    assert bsz == 1 and p.shape[0] == 1, "kernel is specialised to one sequence and one layer"
    x = x.reshape(s, d)
    ssm_w = d // 2
    attn_w = N_HEADS * HEAD_DIM
    kvw = N_KV_HEADS * HEAD_DIM
    n_a = 2 * ssm_w + Q_LORA_RANK + 2 * kvw + attn_w
    n_i = IDX_DIM + IDX_HEADS
    row = lambda v: v.reshape(1, -1)

    proj, c_q, kidx, w_idx = _proj(x, row(g_mix), jnp.swapaxes(w_in[:1], 1, 2), n_a, 2 * ssm_w,
                                   n_a + n_i, row(g_kidx))

    n_chunks = s // S5_CHUNK
    cp, bbt, e, a1, a2 = _s5_operators(a_re[0], a_im[0], log_dt[0], b_re[0], b_im[0], c_re[0], c_im[0],
                                       levels=n_chunks.bit_length() - 1)
    y = _s5(proj, ssm_w, cp, bbt, e, a1, a2)
    y_ssm = _s5_post(y, proj, row(d_skip), w_glu[0].astype(BF16))

    w_q = jnp.concatenate([w_uq[0], w_uq_idx[0]], axis=1).astype(BF16)
    q, qi = _qproj(c_q, 0, row(g_q), w_q)
    k_off = 2 * ssm_w + Q_LORA_RANK
    z_block = (k_off + 2 * kvw) // attn_w
    y_attn = _dsa(q, qi, w_idx, proj, z_block, k_off // kvw, kidx)

    out = _tail(y_ssm, y_attn, proj, n_a, x, p[0, 0], w_ssm_out[0].astype(BF16), w_attn_out[0].astype(BF16),
                w_o[0].astype(BF16), w_ple[0].astype(BF16), w_ple_gate[0].astype(BF16),
                row(g_ple), row(g_ple_post), row(g_final))
    return out.reshape(bsz, s, d)
```

```python
import functools
import math

import numpy as np
import jax
import jax.numpy as jnp
from jax import lax
from jax.experimental import pallas as pl
from jax.experimental.pallas import tpu as pltpu

F32 = jnp.float32
BF16 = jnp.bfloat16
I32 = jnp.int32

EPS = 1e-6
SSM_GROUP = 16
SSM_STATE = 64
N_HEADS = 8
N_KV_HEADS = 2
HEAD_DIM = 128
Q_LORA_RANK = 512
IDX_HEADS = 16
IDX_DIM = 64
TOPK_MAX = 256

V7X_LANES = 128
V7X_VMEM_LIMIT = 56 * 1024 * 1024

S5_CHUNK = 32
DSA_TQ = 512
DSA_TK = 512
SEARCH_ROWS = 128
ATTN_HEADS = 1
SCORE_HEADS = 2

INT_MIN = -(2 ** 31)
KEY_NEG_INF = -2139095041
NEG_INIT = -1e30

_NT = (((1,), (1,)), ((), ()))


def _cparams(*sem):
    return pltpu.CompilerParams(dimension_semantics=sem, vmem_limit_bytes=V7X_VMEM_LIMIT)


def _rms(x, g):
    ms = jnp.mean(x * x, axis=-1, keepdims=True)
    return x * lax.rsqrt(ms + EPS) * g


def _sigmoid(x):
    return 1.0 / (1.0 + jnp.exp(-x))


def _gelu_tanh(x):
    c = math.sqrt(2.0 / math.pi)
    return 0.5 * x * (1.0 + jnp.tanh(c * (x + 0.044715 * (x * x * x))))


def _proj_kernel(x_ref, g_ref, wa_ref, wg_ref, wi_ref, gk_ref, ob_ref, cq_ref, ki_ref, widx_ref, h_ref,
                 *, n_a_tiles, cq_tile, n_cq_tiles, w_scale):
    j = pl.program_id(1)

    @pl.when(j == 0)
    def _():
        h = _rms(x_ref[...], g_ref[...]).astype(BF16)
        h_ref[...] = h
        r = lax.dot_general(h, wi_ref[...].astype(BF16), _NT, preferred_element_type=F32)
        ki_ref[...] = _rms(r[:, :IDX_DIM], gk_ref[...]).astype(BF16)
        widx_ref[...] = r[:, IDX_DIM:IDX_DIM + IDX_HEADS] * w_scale

    @pl.when(j < n_a_tiles)
    def _():
        acc = lax.dot_general(h_ref[...], wa_ref[...].astype(BF16), _NT, preferred_element_type=F32)
        ob_ref[...] = acc.astype(BF16)

        @pl.when(jnp.logical_and(j >= cq_tile, j < cq_tile + n_cq_tiles))
        def _():
            cq_ref[...] = acc

    @pl.when(j >= n_a_tiles)
    def _():
        acc = lax.dot_general(h_ref[...], wg_ref[0].astype(BF16), _NT, preferred_element_type=F32)
        ob_ref[...] = _sigmoid(acc).astype(BF16)


def _proj(x, g, w_t, n_a, cq_col, gate_row, gk, *, tm=2048, tn=256):
    s, d = x.shape
    tm = min(tm, s)
    n_g = w_t.shape[1] - gate_row
    n_a_tiles = n_a // tn
    cq_tile, n_cq_tiles = cq_col // tn, Q_LORA_RANK // tn
    return pl.pallas_call(
        functools.partial(_proj_kernel, n_a_tiles=n_a_tiles, cq_tile=cq_tile, n_cq_tiles=n_cq_tiles,
                          w_scale=(IDX_HEADS ** -0.5) * (IDX_DIM ** -0.5)),
        out_shape=(jax.ShapeDtypeStruct((s, n_a + n_g), BF16),
                   jax.ShapeDtypeStruct((s, Q_LORA_RANK), F32),
                   jax.ShapeDtypeStruct((s, IDX_DIM), BF16),
                   jax.ShapeDtypeStruct((s, IDX_HEADS), F32)),
        grid=(s // tm, (n_a + n_g) // tn),
        in_specs=[
            pl.BlockSpec((tm, d), lambda i, j: (i, 0), pipeline_mode=pl.Buffered(1)),
            pl.BlockSpec((1, d), lambda i, j: (0, 0)),
            pl.BlockSpec((None, tn, d), lambda i, j: (0, jnp.minimum(j, n_a_tiles - 1), 0)),
            pl.BlockSpec((pl.Element(1), pl.Element(tn), pl.Element(d)),
                         lambda i, j: (0, pl.multiple_of(gate_row + jnp.maximum(j - n_a_tiles, 0) * tn, 8), 0)),
            pl.BlockSpec((None, V7X_LANES, d), lambda i, j: (0, n_a // V7X_LANES, 0)),
            pl.BlockSpec((1, IDX_DIM), lambda i, j: (0, 0)),
        ],
        out_specs=(pl.BlockSpec((tm, tn), lambda i, j: (i, j)),
                   pl.BlockSpec((tm, tn), lambda i, j: (i, jnp.clip(j - cq_tile, 0, n_cq_tiles - 1))),
                   pl.BlockSpec((tm, IDX_DIM), lambda i, j: (i, 0)),
                   pl.BlockSpec((tm, IDX_HEADS), lambda i, j: (i, 0))),
        scratch_shapes=[pltpu.VMEM((tm, d), BF16)],
        compiler_params=_cparams("parallel", "arbitrary"),
        name="proj",
    )(x, g, w_t, w_t, w_t, gk)


def _qproj_kernel(c_ref, g_ref, w_ref, q_ref, qi_ref, *, q_scale):
    cq = _rms(c_ref[...], g_ref[...]).astype(BF16)
    r = jnp.dot(cq, w_ref[...], preferred_element_type=F32)
    for h in range(N_HEADS):
        q_ref[h] = (r[:, h * HEAD_DIM:(h + 1) * HEAD_DIM] * q_scale).astype(BF16)
    base = N_HEADS * HEAD_DIM
    for h in range(IDX_HEADS):
        qi_ref[h] = r[:, base + h * IDX_DIM: base + (h + 1) * IDX_DIM].astype(BF16)


def _qproj(proj_a, cq_block, g, w, *, tm=512):
    s = proj_a.shape[0]
    return pl.pallas_call(
        functools.partial(_qproj_kernel, q_scale=HEAD_DIM ** -0.5 * math.log2(math.e)),
        out_shape=(jax.ShapeDtypeStruct((N_HEADS, s, HEAD_DIM), BF16),
                   jax.ShapeDtypeStruct((IDX_HEADS, s, IDX_DIM), BF16)),
        grid=(s // tm,),
        in_specs=[
            pl.BlockSpec((tm, Q_LORA_RANK), lambda i: (i, cq_block)),
            pl.BlockSpec((1, Q_LORA_RANK), lambda i: (0, 0)),
            pl.BlockSpec(w.shape, lambda i: (0, 0)),
        ],
        out_specs=(pl.BlockSpec((N_HEADS, tm, HEAD_DIM), lambda i: (0, i, 0)),
                   pl.BlockSpec((IDX_HEADS, tm, IDX_DIM), lambda i: (0, i, 0))),
        compiler_params=_cparams("parallel"),
        name="qproj",
    )(proj_a, g, w)


def _cmul(ar, ai, br, bi):
    return ar * br - ai * bi, ar * bi + ai * br


def _s5_operators(a_re, a_im, log_dt, b_re, b_im, c_re, c_im, levels):
    t_len = S5_CHUNK
    g_n, n_st = a_re.shape
    dt = jnp.exp(log_dt)[:, None]
    mag = jnp.exp(dt * a_re)
    abr = mag * jnp.cos(dt * a_im)
    abi = mag * jnp.sin(dt * a_im)
    den = a_re * a_re + a_im * a_im
    nr = abr - 1.0
    f_re = (nr * a_re + abi * a_im) / den
    f_im = (abi * a_re - nr * a_im) / den
    bt_re = b_re.transpose(0, 2, 1)
    bt_im = b_im.transpose(0, 2, 1)
    bbt_re = f_re[:, None, :] * bt_re - f_im[:, None, :] * bt_im
    bbt_im = f_re[:, None, :] * bt_im + f_im[:, None, :] * bt_re

    nbits = t_len.bit_length()
    sq = [(abr, abi)]
    for _ in range(nbits - 1):
        sq.append(_cmul(*sq[-1], *sq[-1]))
    j = np.arange(t_len + 1)
    pr = jnp.ones((g_n, t_len + 1, n_st), F32)
    pi = jnp.zeros((g_n, t_len + 1, n_st), F32)
    for b in range(nbits):
        bit = jnp.asarray(((j >> b) & 1).astype(bool))[None, :, None]
        mr, mi = _cmul(pr, pi, sq[b][0][:, None, :], sq[b][1][:, None, :])
        pr = jnp.where(bit, mr, pr)
        pi = jnp.where(bit, mi, pi)

    prj, pij = pr[:, :, None, :], pi[:, :, None, :]
    cp_re = c_re[:, None] * prj - c_im[:, None] * pij
    cp_im = c_re[:, None] * pij + c_im[:, None] * prj
    cp = jnp.concatenate([cp_re, -cp_im], axis=-1).reshape(g_n, (t_len + 1) * SSM_GROUP, 2 * n_st)

    prr, pir = prj[:, t_len - 1::-1], pij[:, t_len - 1::-1]
    e_re = prr * bbt_re[:, None] - pir * bbt_im[:, None]
    e_im = prr * bbt_im[:, None] + pir * bbt_re[:, None]
    e = jnp.concatenate([e_re, e_im], axis=-1).reshape(g_n, t_len * SSM_GROUP, 2 * n_st)

    lv_r, lv_i = [pr[:, t_len]], [pi[:, t_len]]
    for _ in range(levels - 1):
        r2, i2 = _cmul(lv_r[-1], lv_i[-1], lv_r[-1], lv_i[-1])
        lv_r.append(r2)
        lv_i.append(i2)
    ar = jnp.stack(lv_r, axis=1)
    ai = jnp.stack(lv_i, axis=1)
    a1 = jnp.concatenate([ar, ar], axis=-1)
    a2 = jnp.concatenate([-ai, ai], axis=-1)
    return cp, jnp.concatenate([bbt_re, bbt_im], axis=-1), e.astype(BF16), a1, a2


def _s5_kernel(*refs, t_len):
    gpt = V7X_LANES // SSM_GROUP
    (u_ref, cp_ref, bbt_ref, e_ref, a1_ref, a2_ref, y_ref,
     rep_ref, master_ref, ebd_ref, fbd_ref, tm_ref) = refs
    n_chunks = u_ref.shape[0] // t_len
    ct = t_len * SSM_GROUP
    n2 = cp_ref.shape[2]
    pair = 2 * V7X_LANES

    @pl.when(pl.program_id(0) == 0)
    def _():
        ebd_ref[...] = jnp.zeros(ebd_ref.shape, BF16)
        fbd_ref[...] = jnp.zeros(fbd_ref.shape, BF16)
        r = lax.broadcasted_iota(I32, rep_ref.shape, 0)
        q = lax.broadcasted_iota(I32, rep_ref.shape, 1)
        same_lag = (r // SSM_GROUP) == (q // V7X_LANES)
        same_ch = (r % SSM_GROUP) == (q % SSM_GROUP)
        rep_ref[...] = jnp.where(same_lag, jnp.where(same_ch, 1.0, 0.0), 0.0).astype(BF16)

    kt8 = jnp.concatenate(
        [lax.dot_general(bbt_ref[k], cp_ref[k, :ct, :], _NT, precision=lax.Precision.HIGHEST,
                         preferred_element_type=F32) for k in range(gpt)], axis=0)
    dall = jnp.dot(kt8.astype(BF16), rep_ref[...], preferred_element_type=F32)
    rg = lax.broadcasted_iota(I32, dall.shape, 0) // SSM_GROUP
    lg = (lax.broadcasted_iota(I32, dall.shape, 1) % V7X_LANES) // SSM_GROUP
    dall = jnp.where(rg == lg, dall, 0.0).astype(BF16)
    for b in range(t_len - 1):
        r0 = (t_len - 2 - b) * V7X_LANES
        master_ref[b * V7X_LANES:(b + 1) * V7X_LANES, :] = dall[:, r0:r0 + pair]
    master_ref[(t_len - 1) * V7X_LANES:, :] = jnp.concatenate(
        [jnp.zeros((V7X_LANES, V7X_LANES), BF16), dall[:, :V7X_LANES]], axis=1)
    for s in range(t_len):
        for k in range(gpt):
            rows = slice(s * V7X_LANES + k * SSM_GROUP, s * V7X_LANES + (k + 1) * SSM_GROUP)
            cols = slice(k * n2, (k + 1) * n2)
            ebd_ref[rows, cols] = e_ref[k, s * SSM_GROUP:(s + 1) * SSM_GROUP, :]
            fbd_ref[rows, cols] = cp_ref[k, (s + 1) * SSM_GROUP:(s + 2) * SSM_GROUP, :].astype(BF16)

    tm_ref[...] = u_ref[...].astype(F32)
    u = jnp.concatenate([tm_ref[pl.ds(s, n_chunks, stride=t_len), :].astype(BF16) for s in range(t_len)],
                        axis=1)
    hend = jnp.dot(u, ebd_ref[...], preferred_element_type=F32)
    row = lax.broadcasted_iota(I32, (hend.shape[0], n2), 0)
    carries = []
    for k in range(gpt):
        p = hend[:, k * n2:(k + 1) * n2]
        for lv in range(a1_ref.shape[1]):
            sh = 1 << lv
            a1 = a1_ref[k, lv:lv + 1, :]
            a2 = a2_ref[k, lv:lv + 1, :]
            ps = jnp.where(row >= sh, pltpu.roll(p, sh, axis=0), 0.0)
            p = p + a1 * ps + a2 * pltpu.roll(ps, n2 // 2, axis=1)
        carries.append(jnp.where(row >= 1, pltpu.roll(p, 1, axis=0), 0.0).astype(BF16))
    carry = jnp.concatenate(carries, axis=1)

    for tp in range(t_len // 2):
        kdim = (tp + 1) * pair
        y2 = jnp.dot(u[:, :kdim], master_ref[(t_len - 2 - 2 * tp) * V7X_LANES:, :],
                     preferred_element_type=F32)
        y2 = y2 + lax.dot_general(carry, fbd_ref[tp * pair:(tp + 1) * pair, :], _NT,
                                  preferred_element_type=F32)
        tm_ref[(2 * tp) * n_chunks:(2 * tp + 1) * n_chunks, :] = y2[:, :V7X_LANES]
        tm_ref[(2 * tp + 1) * n_chunks:(2 * tp + 2) * n_chunks, :] = y2[:, V7X_LANES:]

    for c in range(n_chunks):
        y_ref[c * t_len:(c + 1) * t_len, :] = tm_ref[pl.ds(c, t_len, stride=n_chunks), :]


def _s5(proj, ssm_w, cp, bbt, e, a1, a2):
    s, width = proj.shape
    t_len = S5_CHUNK
    n_chunks = s // t_len
    gpt = V7X_LANES // SSM_GROUP
    ct = t_len * SSM_GROUP
    n2 = e.shape[-1]
    levels = a1.shape[1]
    assert n_chunks == 1 << levels and n2 == V7X_LANES
    grp = lambda rows: pl.BlockSpec((gpt, rows, n2), lambda g: (g, 0, 0))
    col = pl.BlockSpec((s, V7X_LANES), lambda g: (0, g))
    return pl.pallas_call(
        functools.partial(_s5_kernel, t_len=t_len),
        out_shape=jax.ShapeDtypeStruct((s, ssm_w), F32),
        grid=(ssm_w // V7X_LANES,),
        in_specs=[col, grp(ct + SSM_GROUP), grp(SSM_GROUP), grp(ct), grp(levels), grp(levels)],
        out_specs=col,
        scratch_shapes=[
            pltpu.VMEM((ct, t_len * V7X_LANES), BF16),
            pltpu.VMEM((t_len * V7X_LANES, 2 * V7X_LANES), BF16),
            pltpu.VMEM((t_len * V7X_LANES, gpt * n2), BF16),
            pltpu.VMEM((t_len * V7X_LANES, gpt * n2), BF16),
            pltpu.VMEM((s, V7X_LANES), F32),
        ],
        compiler_params=_cparams("arbitrary"),
        name="s5",
    )(proj, cp, bbt, e, a1, a2)


def _s5_post_kernel(y_ref, u_ref, z_ref, d_ref, w_ref, o_ref):
    width = y_ref.shape[1]
    yy = _gelu_tanh(y_ref[...] + d_ref[...] * u_ref[...].astype(F32))
    r = jnp.dot(yy.astype(BF16), w_ref[...], preferred_element_type=F32)
    glu = r[:, :width] * _sigmoid(r[:, width:])
    z = z_ref[...].astype(F32)
    o_ref[...] = (glu * (z * _sigmoid(z))).astype(BF16)


def _s5_post(y, proj_a, d_skip, w_glu, *, tm=512):
    s, width = y.shape
    return pl.pallas_call(
        _s5_post_kernel,
        out_shape=jax.ShapeDtypeStruct((s, width), BF16),
        grid=(s // tm,),
        in_specs=[
            pl.BlockSpec((tm, width), lambda i: (i, 0)),
            pl.BlockSpec((tm, width), lambda i: (i, 0)),
            pl.BlockSpec((tm, width), lambda i: (i, 1)),
            pl.BlockSpec((1, width), lambda i: (0, 0)),
            pl.BlockSpec(w_glu.shape, lambda i: (0, 0)),
        ],
        out_specs=pl.BlockSpec((tm, width), lambda i: (i, 0)),
        compiler_params=_cparams("parallel"),
        name="s5_post",
    )(y, proj_a, proj_a, d_skip, w_glu)


def _key_to_float(key):
    bits = key ^ (lax.shift_right_arithmetic(key, 31) & 0x7FFFFFFF)
    return lax.bitcast_convert_type(bits, F32)


def _dsa_kernel(q_ref, qi_ref, w_ref, z_ref, k_ref, v_ref, ki_ref, o_ref,
                sc_ref, wb_ref, res_ref, cand_ref, cnt_ref, m_ref, l_ref, acc_ref, *, topk):
    tq = DSA_TQ
    halves = tq // V7X_LANES
    cpt = DSA_TK // V7X_LANES
    rep = N_HEADS // N_KV_HEADS
    i = pl.program_id(0)
    n_valid = halves * (i + 1)
    n_it = (n_valid + cpt - 1) // cpt

    for h in range(IDX_HEADS):
        wb_ref[h] = jnp.broadcast_to(w_ref[:, h:h + 1], (tq, V7X_LANES))
    def score_tile(kt, diag):
        ki = ki_ref[pl.ds(pl.multiple_of(kt * tq, tq), tq), :]
        r = [lax.dot_general(qi_ref[g * SCORE_HEADS:(g + 1) * SCORE_HEADS].reshape(SCORE_HEADS * tq, IDX_DIM),
                             ki, _NT, preferred_element_type=F32)
             for g in range(IDX_HEADS // SCORE_HEADS)]
        for half in range(halves):
            lanes = slice(half * V7X_LANES, (half + 1) * V7X_LANES)
            acc = jnp.zeros((tq, V7X_LANES), F32)
            for h in range(IDX_HEADS):
                hg, hl = divmod(h, SCORE_HEADS)
                acc = acc + wb_ref[h] * jnp.maximum(r[hg][hl * tq:(hl + 1) * tq, lanes], 0.0)
            if diag:
                col = lax.broadcasted_iota(I32, acc.shape, 1) + half * V7X_LANES
                rowi = lax.broadcasted_iota(I32, acc.shape, 0)
                acc = jnp.where(col <= rowi, acc, -jnp.inf)
            sc_ref[halves * kt + half] = acc

    def score_body(kt, c):
        score_tile(kt, False)
        return c

    lax.fori_loop(0, i, score_body, 0)
    score_tile(i, True)

    for j in range(cpt - halves):
        @pl.when(n_valid + j < n_it * cpt)
        def _():
            sc_ref[n_valid + j] = jnp.full((tq, V7X_LANES), -jnp.inf, F32)

    res_ref[...] = jnp.full(res_ref.shape, INT_MIN, I32)

    def pass_body(p, c):
        bit = lax.shift_left(jnp.int32(1), 31 - p)
        cand_ref[...] = _key_to_float(res_ref[...] + bit)
        cnts = []
        for rc in range(tq // SEARCH_ROWS):
            rows = slice(rc * SEARCH_ROWS, (rc + 1) * SEARCH_ROWS)
            cand = cand_ref[rows, :]

            def it_body(it, cnt, rows=rows, cand=cand):
                for j in range(cpt):
                    cnt = cnt + jnp.where(sc_ref[cpt * it + j, rows, :] >= cand, 1.0, 0.0)
                return cnt

            cnts.append(lax.fori_loop(0, n_it, it_body, jnp.zeros((SEARCH_ROWS, V7X_LANES), F32)))
        tot = jnp.sum(jnp.concatenate(cnts, axis=0), axis=1, keepdims=True)
        ok = tot >= float(topk)
        res = res_ref[...]
        res_ref[...] = jnp.where(ok, res + bit, res)
        cnt_ref[...] = jnp.where(ok, tot, cnt_ref[...])
        return c

    cnt_ref[...] = (jnp.full(cnt_ref.shape, DSA_TK, I32) * n_it).astype(F32)
    lax.fori_loop(0, 32, pass_body, 0)
    res = res_ref[...]
    tau = _key_to_float(jnp.maximum(res, KEY_NEG_INF + 1))
    cand_ref[...] = tau

    excess = jnp.where(res > KEY_NEG_INF, cnt_ref[...] - float(topk), 0.0).astype(I32)
    lane = lax.broadcasted_iota(I32, (tq, V7X_LANES), 1)
    n_chunks = n_it * cpt

    def drop_body(r, c):
        active = excess > r

        def min_body(ch, mv):
            sc = sc_ref[ch]
            return jnp.minimum(mv, jnp.where(sc >= tau, sc, jnp.inf))

        mv = lax.fori_loop(0, n_chunks, min_body, jnp.full((tq, V7X_LANES), jnp.inf, F32))
        mval = jnp.min(mv, axis=1, keepdims=True)

        def idx_body(ch, ix):
            col = (lane + ch * V7X_LANES).astype(F32)
            return jnp.maximum(ix, jnp.where(sc_ref[ch] == mval, col, -1.0))

        ix = lax.fori_loop(0, n_chunks, idx_body, jnp.full((tq, V7X_LANES), -1.0, F32))
        idx = jnp.max(ix, axis=1, keepdims=True)

        def drop_chunk(ch, c2):
            col = (lane + ch * V7X_LANES).astype(F32)
            sc = sc_ref[ch]
            sc_ref[ch] = jnp.where(active, jnp.where(col == idx, -jnp.inf, sc), sc)
            return c2

        lax.fori_loop(0, n_chunks, drop_chunk, 0)
        return c

    lax.fori_loop(0, jnp.max(excess), drop_body, 0)

    m_ref[...] = jnp.full(m_ref.shape, NEG_INIT, F32)
    l_ref[...] = jnp.zeros(l_ref.shape, F32)
    acc_ref[...] = jnp.zeros(acc_ref.shape, F32)
    ones = jnp.ones((DSA_TK, HEAD_DIM), BF16)

    def attn_body(it, c):
        tau = cand_ref[...]
        neg = jnp.concatenate(
            [jnp.where(sc_ref[cpt * it + j] >= tau, 0.0, -jnp.inf) for j in range(cpt)], axis=1)
        ks = pl.ds(pl.multiple_of(it * DSA_TK, DSA_TK), DSA_TK)
        for part in range(N_HEADS // ATTN_HEADS):
            g = part * ATTN_HEADS // rep
            hs = slice(part * ATTN_HEADS, (part + 1) * ATTN_HEADS)
            cols = slice(g * HEAD_DIM, (g + 1) * HEAD_DIM)
            qg = q_ref[hs].reshape(ATTN_HEADS * tq, HEAD_DIM)
            s = lax.dot_general(qg, k_ref[ks, cols], _NT, preferred_element_type=F32)
            s = s.reshape(ATTN_HEADS, tq, DSA_TK) + neg[None]
            m_prev = m_ref[hs]
            m_new = jnp.maximum(m_prev, jnp.max(s, axis=2, keepdims=True))
            alpha = jnp.exp2(m_prev - m_new)
            p = jnp.concatenate(
                [jnp.exp2(s[..., j * V7X_LANES:(j + 1) * V7X_LANES] - m_new) for j in range(cpt)], axis=2)
            v_aug = jnp.concatenate([v_ref[ks, cols], ones], axis=1)
            pv = jnp.dot(p.reshape(ATTN_HEADS * tq, DSA_TK).astype(BF16), v_aug,
                         preferred_element_type=F32).reshape(ATTN_HEADS, tq, 2 * HEAD_DIM)
            acc_ref[hs] = alpha * acc_ref[hs] + pv[..., :HEAD_DIM]
            l_ref[hs] = alpha * l_ref[hs] + pv[..., HEAD_DIM:]
            m_ref[hs] = m_new
        return c

    lax.fori_loop(0, n_it, attn_body, 0)

    for h in range(N_HEADS):
        cols = slice(h * HEAD_DIM, (h + 1) * HEAD_DIM)
        z = z_ref[:, cols].astype(F32)
        o_ref[:, cols] = ((acc_ref[h] / l_ref[h]) * (z * _sigmoid(z))).astype(BF16)


def _dsa(q, qi, w_idx, proj, z_block, k_block, kidx):
    s = kidx.shape[0]
    tq = DSA_TQ
    width = N_HEADS * HEAD_DIM
    kvw = N_KV_HEADS * HEAD_DIM
    topk = min(TOPK_MAX, s // 4)
    held = pl.Buffered(1)
    return pl.pallas_call(
        functools.partial(_dsa_kernel, topk=topk),
        out_shape=jax.ShapeDtypeStruct((s, width), BF16),
        grid=(s // tq,),
        in_specs=[
            pl.BlockSpec((N_HEADS, tq, HEAD_DIM), lambda i: (0, i, 0)),
            pl.BlockSpec((IDX_HEADS, tq, IDX_DIM), lambda i: (0, i, 0)),
            pl.BlockSpec((tq, IDX_HEADS), lambda i: (i, 0)),
            pl.BlockSpec((tq, width), lambda i: (i, z_block)),
            pl.BlockSpec((s, kvw), lambda i: (0, k_block), pipeline_mode=held),
            pl.BlockSpec((s, kvw), lambda i: (0, k_block + 1), pipeline_mode=held),
            pl.BlockSpec((s, IDX_DIM), lambda i: (0, 0), pipeline_mode=held),
        ],
        out_specs=pl.BlockSpec((tq, width), lambda i: (i, 0)),
        scratch_shapes=[
            pltpu.VMEM((s // V7X_LANES, tq, V7X_LANES), F32),
            pltpu.VMEM((IDX_HEADS, tq, V7X_LANES), F32),
            pltpu.VMEM((tq, V7X_LANES), I32),
            pltpu.VMEM((tq, V7X_LANES), F32),
            pltpu.VMEM((tq, V7X_LANES), F32),
            pltpu.VMEM((N_HEADS, tq, V7X_LANES), F32),
            pltpu.VMEM((N_HEADS, tq, V7X_LANES), F32),
            pltpu.VMEM((N_HEADS, tq, HEAD_DIM), F32),
        ],
        compiler_params=_cparams("arbitrary"),
        name="dsa",
    )(q, qi, w_idx, proj, proj, proj, kidx)


def _tail_kernel(ys_ref, ya_ref, gs_ref, ga_ref, x_ref, p_ref, wso_ref, wao_ref, wo_ref, wp_ref, wg_ref,
                 gple_ref, gpost_ref, gfin_ref, o_ref):
    o_s = jnp.dot(ys_ref[...], wso_ref[...], preferred_element_type=F32)
    o_a = jnp.dot(ya_ref[...], wao_ref[...], preferred_element_type=F32)
    merged = (gs_ref[...].astype(F32) * o_s + ga_ref[...].astype(F32) * o_a).astype(BF16)
    x1 = x_ref[...] + jnp.dot(merged, wo_ref[...], preferred_element_type=F32)
    e = _rms(jnp.dot(p_ref[...].astype(BF16), wp_ref[...], preferred_element_type=F32), gpost_ref[...])
    xn = _rms(x1, gple_ref[...]).astype(BF16)
    gate = _sigmoid(jnp.dot(xn, wg_ref[...], preferred_element_type=F32))
    o_ref[...] = _rms(x1 + gate * e, gfin_ref[...])


def _tail(y_ssm, y_attn, proj, gate_col, x, p, w_so, w_ao, w_o, w_ple, w_gate, g_ple, g_post, g_final,
          *, tm=256):
    s, d = x.shape
    kdim = y_ssm.shape[1]
    pd = p.shape[1]
    g0 = gate_col // d
    rows = lambda width, blk=0: pl.BlockSpec((tm, width), lambda i: (i, blk))
    held = lambda shape: pl.BlockSpec(shape, lambda i: (0, 0), pipeline_mode=pl.Buffered(1))
    vec = pl.BlockSpec((1, d), lambda i: (0, 0))
    return pl.pallas_call(
        _tail_kernel,
        out_shape=jax.ShapeDtypeStruct((s, d), F32),
        grid=(s // tm,),
        in_specs=[rows(kdim), rows(kdim), rows(d, g0), rows(d, g0 + 1), rows(d), rows(pd),
                  held((kdim, d)), held((kdim, d)), held((d, d)), held((pd, d)), held((d, d)),
                  vec, vec, vec],
        out_specs=rows(d),
        compiler_params=_cparams("parallel"),
        name="tail",
    )(y_ssm, y_attn, proj, proj, x, p, w_so, w_ao, w_o, w_ple, w_gate, g_ple, g_post, g_final)


def kernel(x, p, g_mix, w_in, g_q, w_uq, w_uq_idx, g_kidx, a_re, a_im, log_dt, b_re, b_im, c_re, c_im,
           d_skip, w_glu, w_ssm_out, w_attn_out, w_o, g_ple, w_ple_gate, w_ple, g_ple_post, g_final):
    bsz, s, d = x.shape
    assert bsz == 1 and p.shape[0] == 1, "kernel is specialised to one sequence and one layer"
    x = x.reshape(s, d)
    ssm_w = d // 2
    attn_w = N_HEADS * HEAD_DIM
    kvw = N_KV_HEADS * HEAD_DIM
    n_a = 2 * ssm_w + Q_LORA_RANK + 2 * kvw + attn_w
    n_i = IDX_DIM + IDX_HEADS
    row = lambda v: v.reshape(1, -1)

    proj, c_q, kidx, w_idx = _proj(x, row(g_mix), jnp.swapaxes(w_in[:1], 1, 2), n_a, 2 * ssm_w,
                                   n_a + n_i, row(g_kidx))

    n_chunks = s // S5_CHUNK
    cp, bbt, e, a1, a2 = _s5_operators(a_re[0], a_im[0], log_dt[0], b_re[0], b_im[0], c_re[0], c_im[0],
                                       levels=n_chunks.bit_length() - 1)
    y = _s5(proj, ssm_w, cp, bbt, e, a1, a2)
    y_ssm = _s5_post(y, proj, row(d_skip), w_glu[0].astype(BF16))

    w_q = jnp.concatenate([w_uq[0], w_uq_idx[0]], axis=1).astype(BF16)
    q, qi = _qproj(c_q, 0, row(g_q), w_q)
    k_off = 2 * ssm_w + Q_LORA_RANK
    z_block = (k_off + 2 * kvw) // attn_w
    y_attn = _dsa(q, qi, w_idx, proj, z_block, k_off // kvw, kidx)

    out = _tail(y_ssm, y_attn, proj, n_a, x, p[0, 0], w_ssm_out[0].astype(BF16), w_attn_out[0].astype(BF16),
                w_o[0].astype(BF16), w_ple[0].astype(BF16), w_ple_gate[0].astype(BF16),
                row(g_ple), row(g_ple_post), row(g_final))
    return out.reshape(bsz, s, d)
```

```python
import functools
import math

import numpy as np
import jax
import jax.numpy as jnp
from jax import lax
from jax.experimental import pallas as pl
from jax.experimental.pallas import tpu as pltpu

F32 = jnp.float32
BF16 = jnp.bfloat16
I32 = jnp.int32

EPS = 1e-6
SSM_GROUP = 16
N_HEADS = 8
N_KV_HEADS = 2
HEAD_DIM = 128
Q_LORA_RANK = 512
IDX_HEADS = 16
IDX_DIM = 64
TOPK_MAX = 256

V7X_LANES = 128
V7X_VMEM_LIMIT = 56 * 1024 * 1024

S5_CHUNK = 32
DSA_TQ = 512
DSA_TK = 512
SEARCH_ROWS = 128
ATTN_HEADS = 1
SCORE_HEADS = 2

INT_MIN = -(2 ** 31)
KEY_NEG_INF = -2139095041
NEG_INIT = -1e30

_NT = (((1,), (1,)), ((), ()))


def _cparams(*sem):
    return pltpu.CompilerParams(dimension_semantics=sem, vmem_limit_bytes=V7X_VMEM_LIMIT)


def _rms(x, g):
    ms = jnp.mean(x * x, axis=-1, keepdims=True)
    return x * lax.rsqrt(ms + EPS) * g


def _sigmoid(x):
    return 1.0 / (1.0 + jnp.exp(-x))


def _gelu_tanh(x):
    c = math.sqrt(2.0 / math.pi)
    return 0.5 * x * (1.0 + jnp.tanh(c * (x + 0.044715 * (x * x * x))))


def _proj_kernel(x_ref, g_ref, wa_ref, wg_ref, wi_ref, gk_ref, ob_ref, cq_ref, ki_ref, widx_ref, h_ref,
                 *, n_a_tiles, cq_tile, n_cq_tiles, w_scale):
    j = pl.program_id(1)

    @pl.when(j == 0)
    def _():
        h = _rms(x_ref[...], g_ref[...]).astype(BF16)
        h_ref[...] = h
        r = lax.dot_general(h, wi_ref[...].astype(BF16), _NT, preferred_element_type=F32)
        ki_ref[...] = _rms(r[:, :IDX_DIM], gk_ref[...]).astype(BF16)
        widx_ref[...] = r[:, IDX_DIM:IDX_DIM + IDX_HEADS] * w_scale

    @pl.when(j < n_a_tiles)
    def _():
        acc = lax.dot_general(h_ref[...], wa_ref[...].astype(BF16), _NT, preferred_element_type=F32)
        ob_ref[...] = acc.astype(BF16)

        @pl.when(jnp.logical_and(j >= cq_tile, j < cq_tile + n_cq_tiles))
        def _():
            cq_ref[...] = acc

    @pl.when(j >= n_a_tiles)
    def _():
        acc = lax.dot_general(h_ref[...], wg_ref[0].astype(BF16), _NT, preferred_element_type=F32)
        ob_ref[...] = _sigmoid(acc).astype(BF16)


def _proj(x, g, w_t, n_a, cq_col, gate_row, gk, *, tm=2048, tn=256):
    s, d = x.shape
    tm = min(tm, s)
    n_g = w_t.shape[1] - gate_row
    n_a_tiles = n_a // tn
    cq_tile, n_cq_tiles = cq_col // tn, Q_LORA_RANK // tn
    return pl.pallas_call(
        functools.partial(_proj_kernel, n_a_tiles=n_a_tiles, cq_tile=cq_tile, n_cq_tiles=n_cq_tiles,
                          w_scale=(IDX_HEADS ** -0.5) * (IDX_DIM ** -0.5)),
        out_shape=(jax.ShapeDtypeStruct((s, n_a + n_g), BF16),
                   jax.ShapeDtypeStruct((s, Q_LORA_RANK), F32),
                   jax.ShapeDtypeStruct((s, IDX_DIM), BF16),
                   jax.ShapeDtypeStruct((s, IDX_HEADS), F32)),
        grid=(s // tm, (n_a + n_g) // tn),
        in_specs=[
            pl.BlockSpec((tm, d), lambda i, j: (i, 0), pipeline_mode=pl.Buffered(1)),
            pl.BlockSpec((1, d), lambda i, j: (0, 0)),
            pl.BlockSpec((None, tn, d), lambda i, j: (0, jnp.minimum(j, n_a_tiles - 1), 0)),
            pl.BlockSpec((pl.Element(1), pl.Element(tn), pl.Element(d)),
                         lambda i, j: (0, pl.multiple_of(gate_row + jnp.maximum(j - n_a_tiles, 0) * tn, 8), 0)),
            pl.BlockSpec((None, V7X_LANES, d), lambda i, j: (0, n_a // V7X_LANES, 0)),
            pl.BlockSpec((1, IDX_DIM), lambda i, j: (0, 0)),
        ],
        out_specs=(pl.BlockSpec((tm, tn), lambda i, j: (i, j)),
                   pl.BlockSpec((tm, tn), lambda i, j: (i, jnp.clip(j - cq_tile, 0, n_cq_tiles - 1))),
                   pl.BlockSpec((tm, IDX_DIM), lambda i, j: (i, 0)),
                   pl.BlockSpec((tm, IDX_HEADS), lambda i, j: (i, 0))),
        scratch_shapes=[pltpu.VMEM((tm, d), BF16)],
        compiler_params=_cparams("parallel", "arbitrary"),
        name="proj",
    )(x, g, w_t, w_t, w_t, gk)


def _qproj_kernel(c_ref, g_ref, w_ref, q_ref, qi_ref, *, q_scale):
    cq = _rms(c_ref[...], g_ref[...]).astype(BF16)
    r = jnp.dot(cq, w_ref[...], preferred_element_type=F32)
    for h in range(N_HEADS):
        q_ref[h] = (r[:, h * HEAD_DIM:(h + 1) * HEAD_DIM] * q_scale).astype(BF16)
    base = N_HEADS * HEAD_DIM
    for h in range(IDX_HEADS):
        qi_ref[h] = r[:, base + h * IDX_DIM: base + (h + 1) * IDX_DIM].astype(BF16)


def _qproj(proj_a, cq_block, g, w, *, tm=512):
    s = proj_a.shape[0]
    return pl.pallas_call(
        functools.partial(_qproj_kernel, q_scale=HEAD_DIM ** -0.5 * math.log2(math.e)),
        out_shape=(jax.ShapeDtypeStruct((N_HEADS, s, HEAD_DIM), BF16),
                   jax.ShapeDtypeStruct((IDX_HEADS, s, IDX_DIM), BF16)),
        grid=(s // tm,),
        in_specs=[
            pl.BlockSpec((tm, Q_LORA_RANK), lambda i: (i, cq_block)),
            pl.BlockSpec((1, Q_LORA_RANK), lambda i: (0, 0)),
            pl.BlockSpec(w.shape, lambda i: (0, 0)),
        ],
        out_specs=(pl.BlockSpec((N_HEADS, tm, HEAD_DIM), lambda i: (0, i, 0)),
                   pl.BlockSpec((IDX_HEADS, tm, IDX_DIM), lambda i: (0, i, 0))),
        compiler_params=_cparams("parallel"),
        name="qproj",
    )(proj_a, g, w)


def _cmul(ar, ai, br, bi):
    return ar * br - ai * bi, ar * bi + ai * br


def _s5_operators(a_re, a_im, log_dt, b_re, b_im, c_re, c_im, levels):
    t_len = S5_CHUNK
    g_n, n_st = a_re.shape
    dt = jnp.exp(log_dt)[:, None]
    mag = jnp.exp(dt * a_re)
    abr = mag * jnp.cos(dt * a_im)
    abi = mag * jnp.sin(dt * a_im)
    den = a_re * a_re + a_im * a_im
    nr = abr - 1.0
    f_re = (nr * a_re + abi * a_im) / den
    f_im = (abi * a_re - nr * a_im) / den
    bt_re = b_re.transpose(0, 2, 1)
    bt_im = b_im.transpose(0, 2, 1)
    bbt_re = f_re[:, None, :] * bt_re - f_im[:, None, :] * bt_im
    bbt_im = f_re[:, None, :] * bt_im + f_im[:, None, :] * bt_re

    nbits = t_len.bit_length()
    sq = [(abr, abi)]
    for _ in range(nbits - 1):
        sq.append(_cmul(*sq[-1], *sq[-1]))
    j = np.arange(t_len + 1)
    pr = jnp.ones((g_n, t_len + 1, n_st), F32)
    pi = jnp.zeros((g_n, t_len + 1, n_st), F32)
    for b in range(nbits):
        bit = jnp.asarray(((j >> b) & 1).astype(bool))[None, :, None]
        mr, mi = _cmul(pr, pi, sq[b][0][:, None, :], sq[b][1][:, None, :])
        pr = jnp.where(bit, mr, pr)
        pi = jnp.where(bit, mi, pi)

    prj, pij = pr[:, :, None, :], pi[:, :, None, :]
    cp_re = c_re[:, None] * prj - c_im[:, None] * pij
    cp_im = c_re[:, None] * pij + c_im[:, None] * prj
    cp = jnp.concatenate([cp_re, -cp_im], axis=-1).reshape(g_n, (t_len + 1) * SSM_GROUP, 2 * n_st)

    prr, pir = prj[:, t_len - 1::-1], pij[:, t_len - 1::-1]
    e_re = prr * bbt_re[:, None] - pir * bbt_im[:, None]
    e_im = prr * bbt_im[:, None] + pir * bbt_re[:, None]
    e = jnp.concatenate([e_re, e_im], axis=-1).reshape(g_n, t_len * SSM_GROUP, 2 * n_st)

    lv_r, lv_i = [pr[:, t_len]], [pi[:, t_len]]
    for _ in range(levels - 1):
        r2, i2 = _cmul(lv_r[-1], lv_i[-1], lv_r[-1], lv_i[-1])
        lv_r.append(r2)
        lv_i.append(i2)
    ar = jnp.stack(lv_r, axis=1)
    ai = jnp.stack(lv_i, axis=1)
    a1 = jnp.concatenate([ar, ar], axis=-1)
    a2 = jnp.concatenate([-ai, ai], axis=-1)
    return cp, jnp.concatenate([bbt_re, bbt_im], axis=-1), e.astype(BF16), a1, a2


def _s5_kernel(*refs, t_len):
    gpt = V7X_LANES // SSM_GROUP
    (u_ref, cp_ref, bbt_ref, e_ref, a1_ref, a2_ref, y_ref,
     rep_ref, master_ref, ebd_ref, fbd_ref, tm_ref) = refs
    n_chunks = u_ref.shape[0] // t_len
    ct = t_len * SSM_GROUP
    n2 = cp_ref.shape[2]
    pair = 2 * V7X_LANES

    @pl.when(pl.program_id(0) == 0)
    def _():
        ebd_ref[...] = jnp.zeros(ebd_ref.shape, BF16)
        fbd_ref[...] = jnp.zeros(fbd_ref.shape, BF16)
        r = lax.broadcasted_iota(I32, rep_ref.shape, 0)
        q = lax.broadcasted_iota(I32, rep_ref.shape, 1)
        same_lag = (r // SSM_GROUP) == (q // V7X_LANES)
        same_ch = (r % SSM_GROUP) == (q % SSM_GROUP)
        rep_ref[...] = jnp.where(same_lag, jnp.where(same_ch, 1.0, 0.0), 0.0).astype(BF16)

    kt8 = jnp.concatenate(
        [lax.dot_general(bbt_ref[k], cp_ref[k, :ct, :], _NT, precision=lax.Precision.HIGHEST,
                         preferred_element_type=F32) for k in range(gpt)], axis=0)
    dall = jnp.dot(kt8.astype(BF16), rep_ref[...], preferred_element_type=F32)
    rg = lax.broadcasted_iota(I32, dall.shape, 0) // SSM_GROUP
    lg = (lax.broadcasted_iota(I32, dall.shape, 1) % V7X_LANES) // SSM_GROUP
    dall = jnp.where(rg == lg, dall, 0.0).astype(BF16)
    for b in range(t_len - 1):
        r0 = (t_len - 2 - b) * V7X_LANES
        master_ref[b * V7X_LANES:(b + 1) * V7X_LANES, :] = dall[:, r0:r0 + pair]
    master_ref[(t_len - 1) * V7X_LANES:, :] = jnp.concatenate(
        [jnp.zeros((V7X_LANES, V7X_LANES), BF16), dall[:, :V7X_LANES]], axis=1)
    for s in range(t_len):
        for k in range(gpt):
            rows = slice(s * V7X_LANES + k * SSM_GROUP, s * V7X_LANES + (k + 1) * SSM_GROUP)
            cols = slice(k * n2, (k + 1) * n2)
            ebd_ref[rows, cols] = e_ref[k, s * SSM_GROUP:(s + 1) * SSM_GROUP, :]
            fbd_ref[rows, cols] = cp_ref[k, (s + 1) * SSM_GROUP:(s + 2) * SSM_GROUP, :].astype(BF16)

    tm_ref[...] = u_ref[...].astype(F32)
    u = jnp.concatenate([tm_ref[pl.ds(s, n_chunks, stride=t_len), :].astype(BF16) for s in range(t_len)],
                        axis=1)
    hend = jnp.dot(u, ebd_ref[...], preferred_element_type=F32)
    row = lax.broadcasted_iota(I32, (hend.shape[0], n2), 0)
    carries = []
    for k in range(gpt):
        p = hend[:, k * n2:(k + 1) * n2]
        for lv in range(a1_ref.shape[1]):
            sh = 1 << lv
            a1 = a1_ref[k, lv:lv + 1, :]
            a2 = a2_ref[k, lv:lv + 1, :]
            ps = jnp.where(row >= sh, pltpu.roll(p, sh, axis=0), 0.0)
            p = p + a1 * ps + a2 * pltpu.roll(ps, n2 // 2, axis=1)
        carries.append(jnp.where(row >= 1, pltpu.roll(p, 1, axis=0), 0.0).astype(BF16))
    carry = jnp.concatenate(carries, axis=1)

    for tp in range(t_len // 2):
        kdim = (tp + 1) * pair
        y2 = jnp.dot(u[:, :kdim], master_ref[(t_len - 2 - 2 * tp) * V7X_LANES:, :],
                     preferred_element_type=F32)
        y2 = y2 + lax.dot_general(carry, fbd_ref[tp * pair:(tp + 1) * pair, :], _NT,
                                  preferred_element_type=F32)
        tm_ref[(2 * tp) * n_chunks:(2 * tp + 1) * n_chunks, :] = y2[:, :V7X_LANES]
        tm_ref[(2 * tp + 1) * n_chunks:(2 * tp + 2) * n_chunks, :] = y2[:, V7X_LANES:]

    for c in range(n_chunks):
        y_ref[c * t_len:(c + 1) * t_len, :] = tm_ref[pl.ds(c, t_len, stride=n_chunks), :]


def _s5(proj, ssm_w, cp, bbt, e, a1, a2):
    s, width = proj.shape
    t_len = S5_CHUNK
    n_chunks = s // t_len
    gpt = V7X_LANES // SSM_GROUP
    ct = t_len * SSM_GROUP
    n2 = e.shape[-1]
    levels = a1.shape[1]
    assert n_chunks == 1 << levels and n2 == V7X_LANES
    grp = lambda rows: pl.BlockSpec((gpt, rows, n2), lambda g: (g, 0, 0))
    col = pl.BlockSpec((s, V7X_LANES), lambda g: (0, g))
    return pl.pallas_call(
        functools.partial(_s5_kernel, t_len=t_len),
        out_shape=jax.ShapeDtypeStruct((s, ssm_w), F32),
        grid=(ssm_w // V7X_LANES,),
        in_specs=[col, grp(ct + SSM_GROUP), grp(SSM_GROUP), grp(ct), grp(levels), grp(levels)],
        out_specs=col,
        scratch_shapes=[
            pltpu.VMEM((ct, t_len * V7X_LANES), BF16),
            pltpu.VMEM((t_len * V7X_LANES, 2 * V7X_LANES), BF16),
            pltpu.VMEM((t_len * V7X_LANES, gpt * n2), BF16),
            pltpu.VMEM((t_len * V7X_LANES, gpt * n2), BF16),
            pltpu.VMEM((s, V7X_LANES), F32),
        ],
        compiler_params=_cparams("arbitrary"),
        name="s5",
    )(proj, cp, bbt, e, a1, a2)


def _s5_post_kernel(y_ref, u_ref, z_ref, d_ref, w_ref, o_ref):
    width = y_ref.shape[1]
    yy = _gelu_tanh(y_ref[...] + d_ref[...] * u_ref[...].astype(F32))
    r = jnp.dot(yy.astype(BF16), w_ref[...], preferred_element_type=F32)
    glu = r[:, :width] * _sigmoid(r[:, width:])
    z = z_ref[...].astype(F32)
    o_ref[...] = (glu * (z * _sigmoid(z))).astype(BF16)


def _s5_post(y, proj_a, d_skip, w_glu, *, tm=512):
    s, width = y.shape
    return pl.pallas_call(
        _s5_post_kernel,
        out_shape=jax.ShapeDtypeStruct((s, width), BF16),
        grid=(s // tm,),
        in_specs=[
            pl.BlockSpec((tm, width), lambda i: (i, 0)),
            pl.BlockSpec((tm, width), lambda i: (i, 0)),
            pl.BlockSpec((tm, width), lambda i: (i, 1)),
            pl.BlockSpec((1, width), lambda i: (0, 0)),
            pl.BlockSpec(w_glu.shape, lambda i: (0, 0)),
        ],
        out_specs=pl.BlockSpec((tm, width), lambda i: (i, 0)),
        compiler_params=_cparams("parallel"),
        name="s5_post",
    )(y, proj_a, proj_a, d_skip, w_glu)


def _key_to_float(key):
    bits = key ^ (lax.shift_right_arithmetic(key, 31) & 0x7FFFFFFF)
    return lax.bitcast_convert_type(bits, F32)


def _dsa_kernel(q_ref, qi_ref, w_ref, z_ref, k_ref, v_ref, ki_ref, o_ref,
                sc_ref, wb_ref, res_ref, cand_ref, cnt_ref, m_ref, l_ref, acc_ref, *, topk):
    tq = DSA_TQ
    halves = tq // V7X_LANES
    cpt = DSA_TK // V7X_LANES
    rep = N_HEADS // N_KV_HEADS
    i = pl.program_id(0)
    n_valid = halves * (i + 1)
    n_it = (n_valid + cpt - 1) // cpt

    for h in range(IDX_HEADS):
        wb_ref[h] = jnp.broadcast_to(w_ref[:, h:h + 1], (tq, V7X_LANES))
    def score_tile(kt, diag):
        ki = ki_ref[pl.ds(pl.multiple_of(kt * tq, tq), tq), :]
        r = [lax.dot_general(qi_ref[g * SCORE_HEADS:(g + 1) * SCORE_HEADS].reshape(SCORE_HEADS * tq, IDX_DIM),
                             ki, _NT, preferred_element_type=F32)
             for g in range(IDX_HEADS // SCORE_HEADS)]
        for half in range(halves):
            lanes = slice(half * V7X_LANES, (half + 1) * V7X_LANES)
            acc = jnp.zeros((tq, V7X_LANES), F32)
            for h in range(IDX_HEADS):
                hg, hl = divmod(h, SCORE_HEADS)
                acc = acc + wb_ref[h] * jnp.maximum(r[hg][hl * tq:(hl + 1) * tq, lanes], 0.0)
            if diag:
                col = lax.broadcasted_iota(I32, acc.shape, 1) + half * V7X_LANES
                rowi = lax.broadcasted_iota(I32, acc.shape, 0)
                acc = jnp.where(col <= rowi, acc, -jnp.inf)
            sc_ref[halves * kt + half] = acc

    def score_body(kt, c):
        score_tile(kt, False)
        return c

    lax.fori_loop(0, i, score_body, 0)
    score_tile(i, True)

    for j in range(cpt - halves):
        @pl.when(n_valid + j < n_it * cpt)
        def _():
            sc_ref[n_valid + j] = jnp.full((tq, V7X_LANES), -jnp.inf, F32)

    res_ref[...] = jnp.full(res_ref.shape, INT_MIN, I32)

    def pass_body(p, c):
        bit = lax.shift_left(jnp.int32(1), 31 - p)
        cnts = []
        for rc in range(tq // SEARCH_ROWS):
            rows = slice(rc * SEARCH_ROWS, (rc + 1) * SEARCH_ROWS)
            cand = _key_to_float(res_ref[rows, :] + bit)

            def it_body(it, cnt, rows=rows, cand=cand):
                for j in range(cpt):
                    cnt = cnt + jnp.where(sc_ref[cpt * it + j, rows, :] >= cand, 1.0, 0.0)
                return cnt

            cnts.append(lax.fori_loop(0, n_it, it_body, jnp.zeros((SEARCH_ROWS, V7X_LANES), F32)))
        tot = jnp.sum(jnp.concatenate(cnts, axis=0), axis=1, keepdims=True)
        ok = tot >= float(topk)
        res = res_ref[...]
        res_ref[...] = jnp.where(ok, res + bit, res)
        cnt_ref[...] = jnp.where(ok, tot, cnt_ref[...])
        return c

    cnt_ref[...] = (jnp.full(cnt_ref.shape, DSA_TK, I32) * n_it).astype(F32)
    lax.fori_loop(0, 32, pass_body, 0)
    res = res_ref[...]
    tau = _key_to_float(jnp.maximum(res, KEY_NEG_INF + 1))
    cand_ref[...] = tau

    excess = jnp.where(res > KEY_NEG_INF, cnt_ref[...] - float(topk), 0.0).astype(I32)
    lane = lax.broadcasted_iota(I32, (tq, V7X_LANES), 1)
    n_chunks = n_it * cpt

    def drop_body(r, c):
        active = excess > r

        def min_body(ch, mv):
            sc = sc_ref[ch]
            return jnp.minimum(mv, jnp.where(sc >= tau, sc, jnp.inf))

        mv = lax.fori_loop(0, n_chunks, min_body, jnp.full((tq, V7X_LANES), jnp.inf, F32))
        mval = jnp.min(mv, axis=1, keepdims=True)

        def idx_body(ch, ix):
            col = (lane + ch * V7X_LANES).astype(F32)
            return jnp.maximum(ix, jnp.where(sc_ref[ch] == mval, col, -1.0))

        ix = lax.fori_loop(0, n_chunks, idx_body, jnp.full((tq, V7X_LANES), -1.0, F32))
        idx = jnp.max(ix, axis=1, keepdims=True)

        def drop_chunk(ch, c2):
            col = (lane + ch * V7X_LANES).astype(F32)
            sc = sc_ref[ch]
            sc_ref[ch] = jnp.where(active, jnp.where(col == idx, -jnp.inf, sc), sc)
            return c2

        lax.fori_loop(0, n_chunks, drop_chunk, 0)
        return c

    lax.fori_loop(0, jnp.max(excess), drop_body, 0)

    m_ref[...] = jnp.full(m_ref.shape, NEG_INIT, F32)
    l_ref[...] = jnp.zeros(l_ref.shape, F32)
    acc_ref[...] = jnp.zeros(acc_ref.shape, F32)
    ones = jnp.ones((DSA_TK, HEAD_DIM), BF16)

    def attn_body(it, c):
        tau = cand_ref[...]
        neg = jnp.concatenate(
            [jnp.where(sc_ref[cpt * it + j] >= tau, 0.0, -jnp.inf) for j in range(cpt)], axis=1)
        ks = pl.ds(pl.multiple_of(it * DSA_TK, DSA_TK), DSA_TK)
        for part in range(N_HEADS // ATTN_HEADS):
            g = part * ATTN_HEADS // rep
            hs = slice(part * ATTN_HEADS, (part + 1) * ATTN_HEADS)
            cols = slice(g * HEAD_DIM, (g + 1) * HEAD_DIM)
            qg = q_ref[hs].reshape(ATTN_HEADS * tq, HEAD_DIM)
            s = lax.dot_general(qg, k_ref[ks, cols], _NT, preferred_element_type=F32)
            s = s.reshape(ATTN_HEADS, tq, DSA_TK) + neg[None]
            m_prev = m_ref[hs]
            m_new = jnp.maximum(m_prev, jnp.max(s, axis=2, keepdims=True))
            alpha = jnp.exp2(m_prev - m_new)
            p = jnp.concatenate(
                [jnp.exp2(s[..., j * V7X_LANES:(j + 1) * V7X_LANES] - m_new) for j in range(cpt)], axis=2)
            v_aug = jnp.concatenate([v_ref[ks, cols], ones], axis=1)
            pv = jnp.dot(p.reshape(ATTN_HEADS * tq, DSA_TK).astype(BF16), v_aug,
                         preferred_element_type=F32).reshape(ATTN_HEADS, tq, 2 * HEAD_DIM)
            acc_ref[hs] = alpha * acc_ref[hs] + pv[..., :HEAD_DIM]
            l_ref[hs] = alpha * l_ref[hs] + pv[..., HEAD_DIM:]
            m_ref[hs] = m_new
        return c

    lax.fori_loop(0, n_it, attn_body, 0)

    for h in range(N_HEADS):
        cols = slice(h * HEAD_DIM, (h + 1) * HEAD_DIM)
        z = z_ref[:, cols].astype(F32)
        o_ref[:, cols] = ((acc_ref[h] / l_ref[h]) * (z * _sigmoid(z))).astype(BF16)


def _dsa(q, qi, w_idx, proj, z_block, k_block, kidx):
    s = kidx.shape[0]
    tq = DSA_TQ
    width = N_HEADS * HEAD_DIM
    kvw = N_KV_HEADS * HEAD_DIM
    topk = min(TOPK_MAX, s // 4)
    held = pl.Buffered(1)
    return pl.pallas_call(
        functools.partial(_dsa_kernel, topk=topk),
        out_shape=jax.ShapeDtypeStruct((s, width), BF16),
        grid=(s // tq,),
        in_specs=[
            pl.BlockSpec((N_HEADS, tq, HEAD_DIM), lambda i: (0, i, 0)),
            pl.BlockSpec((IDX_HEADS, tq, IDX_DIM), lambda i: (0, i, 0)),
            pl.BlockSpec((tq, IDX_HEADS), lambda i: (i, 0)),
            pl.BlockSpec((tq, width), lambda i: (i, z_block)),
            pl.BlockSpec((s, kvw), lambda i: (0, k_block), pipeline_mode=held),
            pl.BlockSpec((s, kvw), lambda i: (0, k_block + 1), pipeline_mode=held),
            pl.BlockSpec((s, IDX_DIM), lambda i: (0, 0), pipeline_mode=held),
        ],
        out_specs=pl.BlockSpec((tq, width), lambda i: (i, 0)),
        scratch_shapes=[
            pltpu.VMEM((s // V7X_LANES, tq, V7X_LANES), F32),
            pltpu.VMEM((IDX_HEADS, tq, V7X_LANES), F32),
            pltpu.VMEM((tq, V7X_LANES), I32),
            pltpu.VMEM((tq, V7X_LANES), F32),
            pltpu.VMEM((tq, V7X_LANES), F32),
            pltpu.VMEM((N_HEADS, tq, V7X_LANES), F32),
            pltpu.VMEM((N_HEADS, tq, V7X_LANES), F32),
            pltpu.VMEM((N_HEADS, tq, HEAD_DIM), F32),
        ],
        compiler_params=_cparams("arbitrary"),
        name="dsa",
    )(q, qi, w_idx, proj, proj, proj, kidx)


def _tail_kernel(ys_ref, ya_ref, gs_ref, ga_ref, x_ref, p_ref, wso_ref, wao_ref, wo_ref, wp_ref, wg_ref,
                 gple_ref, gpost_ref, gfin_ref, o_ref):
    o_s = jnp.dot(ys_ref[...], wso_ref[...], preferred_element_type=F32)
    o_a = jnp.dot(ya_ref[...], wao_ref[...], preferred_element_type=F32)
    merged = (gs_ref[...].astype(F32) * o_s + ga_ref[...].astype(F32) * o_a).astype(BF16)
    x1 = x_ref[...] + jnp.dot(merged, wo_ref[...], preferred_element_type=F32)
    e = _rms(jnp.dot(p_ref[...].astype(BF16), wp_ref[...], preferred_element_type=F32), gpost_ref[...])
    xn = _rms(x1, gple_ref[...]).astype(BF16)
    gate = _sigmoid(jnp.dot(xn, wg_ref[...], preferred_element_type=F32))
    o_ref[...] = _rms(x1 + gate * e, gfin_ref[...])


def _tail(y_ssm, y_attn, proj, gate_col, x, p, w_so, w_ao, w_o, w_ple, w_gate, g_ple, g_post, g_final,
          *, tm=256):
    s, d = x.shape
    kdim = y_ssm.shape[1]
    pd = p.shape[1]
    g0 = gate_col // d
    rows = lambda width, blk=0: pl.BlockSpec((tm, width), lambda i: (i, blk))
    held = lambda shape: pl.BlockSpec(shape, lambda i: (0, 0), pipeline_mode=pl.Buffered(1))
    vec = pl.BlockSpec((1, d), lambda i: (0, 0))
    return pl.pallas_call(
        _tail_kernel,
        out_shape=jax.ShapeDtypeStruct((s, d), F32),
        grid=(s // tm,),
        in_specs=[rows(kdim), rows(kdim), rows(d, g0), rows(d, g0 + 1), rows(d), rows(pd),
                  held((kdim, d)), held((kdim, d)), held((d, d)), held((pd, d)), held((d, d)),
                  vec, vec, vec],
        out_specs=rows(d),
        compiler_params=_cparams("parallel"),
        name="tail",
    )(y_ssm, y_attn, proj, proj, x, p, w_so, w_ao, w_o, w_ple, w_gate, g_ple, g_post, g_final)


def kernel(x, p, g_mix, w_in, g_q, w_uq, w_uq_idx, g_kidx, a_re, a_im, log_dt, b_re, b_im, c_re, c_im,
           d_skip, w_glu, w_ssm_out, w_attn_out, w_o, g_ple, w_ple_gate, w_ple, g_ple_post, g_final):
    bsz, s, d = x.shape
    assert bsz == 1 and p.shape[0] == 1, "kernel is specialised to one sequence and one layer"
    x = x.reshape(s, d)
    ssm_w = d // 2
    attn_w = N_HEADS * HEAD_DIM
    kvw = N_KV_HEADS * HEAD_DIM
    n_a = 2 * ssm_w + Q_LORA_RANK + 2 * kvw + attn_w
    n_i = IDX_DIM + IDX_HEADS
    row = lambda v: v.reshape(1, -1)

    proj, c_q, kidx, w_idx = _proj(x, row(g_mix), jnp.swapaxes(w_in[:1], 1, 2), n_a, 2 * ssm_w,
                                   n_a + n_i, row(g_kidx))

    n_chunks = s // S5_CHUNK
    cp, bbt, e, a1, a2 = _s5_operators(a_re[0], a_im[0], log_dt[0], b_re[0], b_im[0], c_re[0], c_im[0],
                                       levels=n_chunks.bit_length() - 1)
    y = _s5(proj, ssm_w, cp, bbt, e, a1, a2)
    y_ssm = _s5_post(y, proj, row(d_skip), w_glu[0].astype(BF16))

    w_q = jnp.concatenate([w_uq[0], w_uq_idx[0]], axis=1).astype(BF16)
    q, qi = _qproj(c_q, 0, row(g_q), w_q)
    k_off = 2 * ssm_w + Q_LORA_RANK
    z_block = (k_off + 2 * kvw) // attn_w
    y_attn = _dsa(q, qi, w_idx, proj, z_block, k_off // kvw, kidx)

    out = _tail(y_ssm, y_attn, proj, n_a, x, p[0, 0], w_ssm_out[0].astype(BF16), w_attn_out[0].astype(BF16),
                w_o[0].astype(BF16), w_ple[0].astype(BF16), w_ple_gate[0].astype(BF16),
                row(g_ple), row(g_ple_post), row(g_final))
    return out.reshape(bsz, s, d)
```

```python
import functools
import math

import numpy as np
import jax
import jax.numpy as jnp
from jax import lax
from jax.experimental import pallas as pl
from jax.experimental.pallas import tpu as pltpu

F32 = jnp.float32
BF16 = jnp.bfloat16
I32 = jnp.int32

EPS = 1e-6
SSM_GROUP = 16
N_HEADS = 8
N_KV_HEADS = 2
HEAD_DIM = 128
Q_LORA_RANK = 512
IDX_HEADS = 16
IDX_DIM = 64
TOPK_MAX = 256

V7X_LANES = 128
V7X_VMEM_LIMIT = 56 * 1024 * 1024

S5_CHUNK = 32
DSA_TQ = 512
DSA_TK = 512
SEARCH_ROWS = 128
SEARCH_GROUP = 4
ATTN_HEADS = 1
SCORE_HEADS = 2

INT_MIN = -(2 ** 31)
KEY_NEG_INF = -2139095041
NEG_INIT = -1e30

_NT = (((1,), (1,)), ((), ()))


def _cparams(*sem):
    return pltpu.CompilerParams(dimension_semantics=sem, vmem_limit_bytes=V7X_VMEM_LIMIT)


def _rms(x, g):
    ms = jnp.mean(x * x, axis=-1, keepdims=True)
    return x * lax.rsqrt(ms + EPS) * g


def _sigmoid(x):
    return 1.0 / (1.0 + jnp.exp(-x))


def _gelu_tanh(x):
    c = math.sqrt(2.0 / math.pi)
    return 0.5 * x * (1.0 + jnp.tanh(c * (x + 0.044715 * (x * x * x))))


def _proj_kernel(x_ref, g_ref, wa_ref, wg_ref, wi_ref, gk_ref, ob_ref, cq_ref, ki_ref, widx_ref, h_ref,
                 *, n_a_tiles, cq_tile, n_cq_tiles, w_scale):
    j = pl.program_id(1)

    @pl.when(j == 0)
    def _():
        h = _rms(x_ref[...], g_ref[...]).astype(BF16)
        h_ref[...] = h
        r = lax.dot_general(h, wi_ref[...].astype(BF16), _NT, preferred_element_type=F32)
        ki_ref[...] = _rms(r[:, :IDX_DIM], gk_ref[...]).astype(BF16)
        widx_ref[...] = r[:, IDX_DIM:IDX_DIM + IDX_HEADS] * w_scale

    @pl.when(j < n_a_tiles)
    def _():
        acc = lax.dot_general(h_ref[...], wa_ref[...].astype(BF16), _NT, preferred_element_type=F32)
        ob_ref[...] = acc.astype(BF16)

        @pl.when(jnp.logical_and(j >= cq_tile, j < cq_tile + n_cq_tiles))
        def _():
            cq_ref[...] = acc

    @pl.when(j >= n_a_tiles)
    def _():
        acc = lax.dot_general(h_ref[...], wg_ref[0].astype(BF16), _NT, preferred_element_type=F32)
        ob_ref[...] = _sigmoid(acc).astype(BF16)


def _proj(x, g, w_t, n_a, cq_col, gate_row, gk, *, tm=2048, tn=256):
    s, d = x.shape
    tm = min(tm, s)
    n_g = w_t.shape[1] - gate_row
    n_a_tiles = n_a // tn
    cq_tile, n_cq_tiles = cq_col // tn, Q_LORA_RANK // tn
    return pl.pallas_call(
        functools.partial(_proj_kernel, n_a_tiles=n_a_tiles, cq_tile=cq_tile, n_cq_tiles=n_cq_tiles,
                          w_scale=(IDX_HEADS ** -0.5) * (IDX_DIM ** -0.5)),
        out_shape=(jax.ShapeDtypeStruct((s, n_a + n_g), BF16),
                   jax.ShapeDtypeStruct((s, Q_LORA_RANK), F32),
                   jax.ShapeDtypeStruct((s, IDX_DIM), BF16),
                   jax.ShapeDtypeStruct((s, IDX_HEADS), F32)),
        grid=(s // tm, (n_a + n_g) // tn),
        in_specs=[
            pl.BlockSpec((tm, d), lambda i, j: (i, 0), pipeline_mode=pl.Buffered(1)),
            pl.BlockSpec((1, d), lambda i, j: (0, 0)),
            pl.BlockSpec((None, tn, d), lambda i, j: (0, jnp.minimum(j, n_a_tiles - 1), 0)),
            pl.BlockSpec((pl.Element(1), pl.Element(tn), pl.Element(d)),
                         lambda i, j: (0, pl.multiple_of(gate_row + jnp.maximum(j - n_a_tiles, 0) * tn, 8), 0)),
            pl.BlockSpec((None, V7X_LANES, d), lambda i, j: (0, n_a // V7X_LANES, 0)),
            pl.BlockSpec((1, IDX_DIM), lambda i, j: (0, 0)),
        ],
        out_specs=(pl.BlockSpec((tm, tn), lambda i, j: (i, j)),
                   pl.BlockSpec((tm, tn), lambda i, j: (i, jnp.clip(j - cq_tile, 0, n_cq_tiles - 1))),
                   pl.BlockSpec((tm, IDX_DIM), lambda i, j: (i, 0)),
                   pl.BlockSpec((tm, IDX_HEADS), lambda i, j: (i, 0))),
        scratch_shapes=[pltpu.VMEM((tm, d), BF16)],
        compiler_params=_cparams("parallel", "arbitrary"),
        name="proj",
    )(x, g, w_t, w_t, w_t, gk)


def _qproj_kernel(c_ref, g_ref, w_ref, q_ref, qi_ref, *, q_scale):
    cq = _rms(c_ref[...], g_ref[...]).astype(BF16)
    r = jnp.dot(cq, w_ref[...], preferred_element_type=F32)
    for h in range(N_HEADS):
        q_ref[h] = (r[:, h * HEAD_DIM:(h + 1) * HEAD_DIM] * q_scale).astype(BF16)
    base = N_HEADS * HEAD_DIM
    for h in range(IDX_HEADS):
        qi_ref[h] = r[:, base + h * IDX_DIM: base + (h + 1) * IDX_DIM].astype(BF16)


def _qproj(proj_a, cq_block, g, w, *, tm=512):
    s = proj_a.shape[0]
    return pl.pallas_call(
        functools.partial(_qproj_kernel, q_scale=HEAD_DIM ** -0.5 * math.log2(math.e)),
        out_shape=(jax.ShapeDtypeStruct((N_HEADS, s, HEAD_DIM), BF16),
                   jax.ShapeDtypeStruct((IDX_HEADS, s, IDX_DIM), BF16)),
        grid=(s // tm,),
        in_specs=[
            pl.BlockSpec((tm, Q_LORA_RANK), lambda i: (i, cq_block)),
            pl.BlockSpec((1, Q_LORA_RANK), lambda i: (0, 0)),
            pl.BlockSpec(w.shape, lambda i: (0, 0)),
        ],
        out_specs=(pl.BlockSpec((N_HEADS, tm, HEAD_DIM), lambda i: (0, i, 0)),
                   pl.BlockSpec((IDX_HEADS, tm, IDX_DIM), lambda i: (0, i, 0))),
        compiler_params=_cparams("parallel"),
        name="qproj",
    )(proj_a, g, w)


def _cmul(ar, ai, br, bi):
    return ar * br - ai * bi, ar * bi + ai * br


def _s5_operators(a_re, a_im, log_dt, b_re, b_im, c_re, c_im, levels):
    t_len = S5_CHUNK
    g_n, n_st = a_re.shape
    dt = jnp.exp(log_dt)[:, None]
    mag = jnp.exp(dt * a_re)
    abr = mag * jnp.cos(dt * a_im)
    abi = mag * jnp.sin(dt * a_im)
    den = a_re * a_re + a_im * a_im
    nr = abr - 1.0
    f_re = (nr * a_re + abi * a_im) / den
    f_im = (abi * a_re - nr * a_im) / den
    bt_re = b_re.transpose(0, 2, 1)
    bt_im = b_im.transpose(0, 2, 1)
    bbt_re = f_re[:, None, :] * bt_re - f_im[:, None, :] * bt_im
    bbt_im = f_re[:, None, :] * bt_im + f_im[:, None, :] * bt_re

    nbits = t_len.bit_length()
    sq = [(abr, abi)]
    for _ in range(nbits - 1):
        sq.append(_cmul(*sq[-1], *sq[-1]))
    j = np.arange(t_len + 1)
    pr = jnp.ones((g_n, t_len + 1, n_st), F32)
    pi = jnp.zeros((g_n, t_len + 1, n_st), F32)
    for b in range(nbits):
        bit = jnp.asarray(((j >> b) & 1).astype(bool))[None, :, None]
        mr, mi = _cmul(pr, pi, sq[b][0][:, None, :], sq[b][1][:, None, :])
        pr = jnp.where(bit, mr, pr)
        pi = jnp.where(bit, mi, pi)

    prj, pij = pr[:, :, None, :], pi[:, :, None, :]
    cp_re = c_re[:, None] * prj - c_im[:, None] * pij
    cp_im = c_re[:, None] * pij + c_im[:, None] * prj
    cp = jnp.concatenate([cp_re, -cp_im], axis=-1).reshape(g_n, (t_len + 1) * SSM_GROUP, 2 * n_st)

    prr, pir = prj[:, t_len - 1::-1], pij[:, t_len - 1::-1]
    e_re = prr * bbt_re[:, None] - pir * bbt_im[:, None]
    e_im = prr * bbt_im[:, None] + pir * bbt_re[:, None]
    e = jnp.concatenate([e_re, e_im], axis=-1).reshape(g_n, t_len * SSM_GROUP, 2 * n_st)

    lv_r, lv_i = [pr[:, t_len]], [pi[:, t_len]]
    for _ in range(levels - 1):
        r2, i2 = _cmul(lv_r[-1], lv_i[-1], lv_r[-1], lv_i[-1])
        lv_r.append(r2)
        lv_i.append(i2)
    ar = jnp.stack(lv_r, axis=1)
    ai = jnp.stack(lv_i, axis=1)
    a1 = jnp.concatenate([ar, ar], axis=-1)
    a2 = jnp.concatenate([-ai, ai], axis=-1)
    return cp, jnp.concatenate([bbt_re, bbt_im], axis=-1), e.astype(BF16), a1, a2


def _s5_kernel(*refs, t_len):
    gpt = V7X_LANES // SSM_GROUP
    (u_ref, cp_ref, bbt_ref, e_ref, a1_ref, a2_ref, y_ref,
     rep_ref, master_ref, ebd_ref, fbd_ref, tm_ref) = refs
    n_chunks = u_ref.shape[0] // t_len
    ct = t_len * SSM_GROUP
    n2 = cp_ref.shape[2]
    pair = 2 * V7X_LANES

    @pl.when(pl.program_id(0) == 0)
    def _():
        ebd_ref[...] = jnp.zeros(ebd_ref.shape, BF16)
        fbd_ref[...] = jnp.zeros(fbd_ref.shape, BF16)
        r = lax.broadcasted_iota(I32, rep_ref.shape, 0)
        q = lax.broadcasted_iota(I32, rep_ref.shape, 1)
        same_lag = (r // SSM_GROUP) == (q // V7X_LANES)
        same_ch = (r % SSM_GROUP) == (q % SSM_GROUP)
        rep_ref[...] = jnp.where(same_lag, jnp.where(same_ch, 1.0, 0.0), 0.0).astype(BF16)

    kt8 = jnp.concatenate(
        [lax.dot_general(bbt_ref[k], cp_ref[k, :ct, :], _NT, precision=lax.Precision.HIGHEST,
                         preferred_element_type=F32) for k in range(gpt)], axis=0)
    dall = jnp.dot(kt8.astype(BF16), rep_ref[...], preferred_element_type=F32)
    rg = lax.broadcasted_iota(I32, dall.shape, 0) // SSM_GROUP
    lg = (lax.broadcasted_iota(I32, dall.shape, 1) % V7X_LANES) // SSM_GROUP
    dall = jnp.where(rg == lg, dall, 0.0).astype(BF16)
    for b in range(t_len - 1):
        r0 = (t_len - 2 - b) * V7X_LANES
        master_ref[b * V7X_LANES:(b + 1) * V7X_LANES, :] = dall[:, r0:r0 + pair]
    master_ref[(t_len - 1) * V7X_LANES:, :] = jnp.concatenate(
        [jnp.zeros((V7X_LANES, V7X_LANES), BF16), dall[:, :V7X_LANES]], axis=1)
    for s in range(t_len):
        for k in range(gpt):
            rows = slice(s * V7X_LANES + k * SSM_GROUP, s * V7X_LANES + (k + 1) * SSM_GROUP)
            cols = slice(k * n2, (k + 1) * n2)
            ebd_ref[rows, cols] = e_ref[k, s * SSM_GROUP:(s + 1) * SSM_GROUP, :]
            fbd_ref[rows, cols] = cp_ref[k, (s + 1) * SSM_GROUP:(s + 2) * SSM_GROUP, :].astype(BF16)

    tm_ref[...] = u_ref[...].astype(F32)
    u = jnp.concatenate([tm_ref[pl.ds(s, n_chunks, stride=t_len), :].astype(BF16) for s in range(t_len)],
                        axis=1)
    hend = jnp.dot(u, ebd_ref[...], preferred_element_type=F32)
    row = lax.broadcasted_iota(I32, (hend.shape[0], n2), 0)
    carries = []
    for k in range(gpt):
        p = hend[:, k * n2:(k + 1) * n2]
        for lv in range(a1_ref.shape[1]):
            sh = 1 << lv
            a1 = a1_ref[k, lv:lv + 1, :]
            a2 = a2_ref[k, lv:lv + 1, :]
            ps = jnp.where(row >= sh, pltpu.roll(p, sh, axis=0), 0.0)
            p = p + a1 * ps + a2 * pltpu.roll(ps, n2 // 2, axis=1)
        carries.append(jnp.where(row >= 1, pltpu.roll(p, 1, axis=0), 0.0).astype(BF16))
    carry = jnp.concatenate(carries, axis=1)

    for tp in range(t_len // 2):
        kdim = (tp + 1) * pair
        y2 = jnp.dot(u[:, :kdim], master_ref[(t_len - 2 - 2 * tp) * V7X_LANES:, :],
                     preferred_element_type=F32)
        y2 = y2 + lax.dot_general(carry, fbd_ref[tp * pair:(tp + 1) * pair, :], _NT,
                                  preferred_element_type=F32)
        tm_ref[(2 * tp) * n_chunks:(2 * tp + 1) * n_chunks, :] = y2[:, :V7X_LANES]
        tm_ref[(2 * tp + 1) * n_chunks:(2 * tp + 2) * n_chunks, :] = y2[:, V7X_LANES:]

    for c in range(n_chunks):
        y_ref[c * t_len:(c + 1) * t_len, :] = tm_ref[pl.ds(c, t_len, stride=n_chunks), :]


def _s5(proj, ssm_w, cp, bbt, e, a1, a2):
    s, width = proj.shape
    t_len = S5_CHUNK
    n_chunks = s // t_len
    gpt = V7X_LANES // SSM_GROUP
    ct = t_len * SSM_GROUP
    n2 = e.shape[-1]
    levels = a1.shape[1]
    assert n_chunks == 1 << levels and n2 == V7X_LANES
    grp = lambda rows: pl.BlockSpec((gpt, rows, n2), lambda g: (g, 0, 0))
    col = pl.BlockSpec((s, V7X_LANES), lambda g: (0, g))
    return pl.pallas_call(
        functools.partial(_s5_kernel, t_len=t_len),
        out_shape=jax.ShapeDtypeStruct((s, ssm_w), F32),
        grid=(ssm_w // V7X_LANES,),
        in_specs=[col, grp(ct + SSM_GROUP), grp(SSM_GROUP), grp(ct), grp(levels), grp(levels)],
        out_specs=col,
        scratch_shapes=[
            pltpu.VMEM((ct, t_len * V7X_LANES), BF16),
            pltpu.VMEM((t_len * V7X_LANES, 2 * V7X_LANES), BF16),
            pltpu.VMEM((t_len * V7X_LANES, gpt * n2), BF16),
            pltpu.VMEM((t_len * V7X_LANES, gpt * n2), BF16),
            pltpu.VMEM((s, V7X_LANES), F32),
        ],
        compiler_params=_cparams("arbitrary"),
        name="s5",
    )(proj, cp, bbt, e, a1, a2)


def _s5_post_kernel(y_ref, u_ref, z_ref, d_ref, w_ref, o_ref):
    width = y_ref.shape[1]
    yy = _gelu_tanh(y_ref[...] + d_ref[...] * u_ref[...].astype(F32))
    r = jnp.dot(yy.astype(BF16), w_ref[...], preferred_element_type=F32)
    glu = r[:, :width] * _sigmoid(r[:, width:])
    z = z_ref[...].astype(F32)
    o_ref[...] = (glu * (z * _sigmoid(z))).astype(BF16)


def _s5_post(y, proj_a, d_skip, w_glu, *, tm=512):
    s, width = y.shape
    return pl.pallas_call(
        _s5_post_kernel,
        out_shape=jax.ShapeDtypeStruct((s, width), BF16),
        grid=(s // tm,),
        in_specs=[
            pl.BlockSpec((tm, width), lambda i: (i, 0)),
            pl.BlockSpec((tm, width), lambda i: (i, 0)),
            pl.BlockSpec((tm, width), lambda i: (i, 1)),
            pl.BlockSpec((1, width), lambda i: (0, 0)),
            pl.BlockSpec(w_glu.shape, lambda i: (0, 0)),
        ],
        out_specs=pl.BlockSpec((tm, width), lambda i: (i, 0)),
        compiler_params=_cparams("parallel"),
        name="s5_post",
    )(y, proj_a, proj_a, d_skip, w_glu)


def _key_to_float(key):
    bits = key ^ (lax.shift_right_arithmetic(key, 31) & 0x7FFFFFFF)
    return lax.bitcast_convert_type(bits, F32)


def _dsa_kernel(q_ref, qi_ref, w_ref, z_ref, k_ref, v_ref, ki_ref, o_ref,
                sc_ref, wb_ref, res_ref, cand_ref, cnt_ref, m_ref, l_ref, acc_ref, *, topk):
    tq = DSA_TQ
    halves = tq // V7X_LANES
    cpt = DSA_TK // V7X_LANES
    rep = N_HEADS // N_KV_HEADS
    i = pl.program_id(0)
    n_valid = halves * (i + 1)
    n_it = (n_valid + cpt - 1) // cpt

    for h in range(IDX_HEADS):
        wb_ref[h] = jnp.broadcast_to(w_ref[:, h:h + 1], (tq, V7X_LANES))
    def score_tile(kt, diag):
        ki = ki_ref[pl.ds(pl.multiple_of(kt * tq, tq), tq), :]
        r = [lax.dot_general(qi_ref[g * SCORE_HEADS:(g + 1) * SCORE_HEADS].reshape(SCORE_HEADS * tq, IDX_DIM),
                             ki, _NT, preferred_element_type=F32)
             for g in range(IDX_HEADS // SCORE_HEADS)]
        for half in range(halves):
            lanes = slice(half * V7X_LANES, (half + 1) * V7X_LANES)
            acc = jnp.zeros((tq, V7X_LANES), F32)
            for h in range(IDX_HEADS):
                hg, hl = divmod(h, SCORE_HEADS)
                acc = acc + wb_ref[h] * jnp.maximum(r[hg][hl * tq:(hl + 1) * tq, lanes], 0.0)
            if diag:
                col = lax.broadcasted_iota(I32, acc.shape, 1) + half * V7X_LANES
                rowi = lax.broadcasted_iota(I32, acc.shape, 0)
                acc = jnp.where(col <= rowi, acc, -jnp.inf)
            sc_ref[halves * kt + half] = acc

    def score_body(kt, c):
        score_tile(kt, False)
        return c

    lax.fori_loop(0, i, score_body, 0)
    score_tile(i, True)

    for j in range(cpt - halves):
        @pl.when(n_valid + j < n_it * cpt)
        def _():
            sc_ref[n_valid + j] = jnp.full((tq, V7X_LANES), -jnp.inf, F32)

    res_ref[...] = jnp.full(res_ref.shape, INT_MIN, I32)

    def pass_body(p, c):
        bit = lax.shift_left(jnp.int32(1), 31 - p)
        cnts = []
        for rc in range(tq // SEARCH_ROWS):
            rows = slice(rc * SEARCH_ROWS, (rc + 1) * SEARCH_ROWS)
            cand = _key_to_float(res_ref[rows, :] + bit)

            def it_body(it, cnt, rows=rows, cand=cand):
                for j in range(cpt):
                    cnt = cnt + jnp.where(sc_ref[cpt * it + j, rows, :] >= cand, 1.0, 0.0)
                return cnt

            cnts.append(lax.fori_loop(0, n_it, it_body, jnp.zeros((SEARCH_ROWS, V7X_LANES), F32)))
        tot = jnp.sum(jnp.concatenate(cnts, axis=0), axis=1, keepdims=True)
        ok = tot >= float(topk)
        res = res_ref[...]
        res_ref[...] = jnp.where(ok, res + bit, res)
        cnt_ref[...] = jnp.where(ok, tot, cnt_ref[...])
        return c

    cnt_ref[...] = (jnp.full(cnt_ref.shape, DSA_TK, I32) * n_it).astype(F32)

    def group_body(state):
        g, _ = state
        lax.fori_loop(g * SEARCH_GROUP, (g + 1) * SEARCH_GROUP, pass_body, 0)
        settled = jnp.min(jnp.where(cnt_ref[...] == float(topk), 1.0, 0.0)) > 0.0
        return g + 1, settled.astype(I32)

    def group_cond(state):
        g, settled = state
        return jnp.logical_and(g < 32 // SEARCH_GROUP, settled == 0)

    lax.while_loop(group_cond, group_body, (jnp.int32(0), jnp.int32(0)))
    res = res_ref[...]
    tau = _key_to_float(jnp.maximum(res, KEY_NEG_INF + 1))
    cand_ref[...] = tau

    excess = jnp.where(res > KEY_NEG_INF, cnt_ref[...] - float(topk), 0.0).astype(I32)
    lane = lax.broadcasted_iota(I32, (tq, V7X_LANES), 1)
    n_chunks = n_it * cpt

    def drop_body(r, c):
        active = excess > r

        def min_body(ch, mv):
            sc = sc_ref[ch]
            return jnp.minimum(mv, jnp.where(sc >= tau, sc, jnp.inf))

        mv = lax.fori_loop(0, n_chunks, min_body, jnp.full((tq, V7X_LANES), jnp.inf, F32))
        mval = jnp.min(mv, axis=1, keepdims=True)

        def idx_body(ch, ix):
            col = (lane + ch * V7X_LANES).astype(F32)
            return jnp.maximum(ix, jnp.where(sc_ref[ch] == mval, col, -1.0))

        ix = lax.fori_loop(0, n_chunks, idx_body, jnp.full((tq, V7X_LANES), -1.0, F32))
        idx = jnp.max(ix, axis=1, keepdims=True)

        def drop_chunk(ch, c2):
            col = (lane + ch * V7X_LANES).astype(F32)
            sc = sc_ref[ch]
            sc_ref[ch] = jnp.where(active, jnp.where(col == idx, -jnp.inf, sc), sc)
            return c2

        lax.fori_loop(0, n_chunks, drop_chunk, 0)
        return c

    lax.fori_loop(0, jnp.max(excess), drop_body, 0)

    m_ref[...] = jnp.full(m_ref.shape, NEG_INIT, F32)
    l_ref[...] = jnp.zeros(l_ref.shape, F32)
    acc_ref[...] = jnp.zeros(acc_ref.shape, F32)
    ones = jnp.ones((DSA_TK, HEAD_DIM), BF16)

    def attn_body(it, c):
        tau = cand_ref[...]
        neg = jnp.concatenate(
            [jnp.where(sc_ref[cpt * it + j] >= tau, 0.0, -jnp.inf) for j in range(cpt)], axis=1)
        ks = pl.ds(pl.multiple_of(it * DSA_TK, DSA_TK), DSA_TK)
        for part in range(N_HEADS // ATTN_HEADS):
            g = part * ATTN_HEADS // rep
            hs = slice(part * ATTN_HEADS, (part + 1) * ATTN_HEADS)
            cols = slice(g * HEAD_DIM, (g + 1) * HEAD_DIM)
            qg = q_ref[hs].reshape(ATTN_HEADS * tq, HEAD_DIM)
            s = lax.dot_general(qg, k_ref[ks, cols], _NT, preferred_element_type=F32)
            s = s.reshape(ATTN_HEADS, tq, DSA_TK) + neg[None]
            m_prev = m_ref[hs]
            m_new = jnp.maximum(m_prev, jnp.max(s, axis=2, keepdims=True))
            alpha = jnp.exp2(m_prev - m_new)
            p = jnp.concatenate(
                [jnp.exp2(s[..., j * V7X_LANES:(j + 1) * V7X_LANES] - m_new) for j in range(cpt)], axis=2)
            v_aug = jnp.concatenate([v_ref[ks, cols], ones], axis=1)
            pv = jnp.dot(p.reshape(ATTN_HEADS * tq, DSA_TK).astype(BF16), v_aug,
                         preferred_element_type=F32).reshape(ATTN_HEADS, tq, 2 * HEAD_DIM)
            acc_ref[hs] = alpha * acc_ref[hs] + pv[..., :HEAD_DIM]
            l_ref[hs] = alpha * l_ref[hs] + pv[..., HEAD_DIM:]
            m_ref[hs] = m_new
        return c

    lax.fori_loop(0, n_it, attn_body, 0)

    for h in range(N_HEADS):
        cols = slice(h * HEAD_DIM, (h + 1) * HEAD_DIM)
        z = z_ref[:, cols].astype(F32)
        o_ref[:, cols] = ((acc_ref[h] / l_ref[h]) * (z * _sigmoid(z))).astype(BF16)


def _dsa(q, qi, w_idx, proj, z_block, k_block, kidx):
    s = kidx.shape[0]
    tq = DSA_TQ
    width = N_HEADS * HEAD_DIM
    kvw = N_KV_HEADS * HEAD_DIM
    topk = min(TOPK_MAX, s // 4)
    held = pl.Buffered(1)
    return pl.pallas_call(
        functools.partial(_dsa_kernel, topk=topk),
        out_shape=jax.ShapeDtypeStruct((s, width), BF16),
        grid=(s // tq,),
        in_specs=[
            pl.BlockSpec((N_HEADS, tq, HEAD_DIM), lambda i: (0, i, 0)),
            pl.BlockSpec((IDX_HEADS, tq, IDX_DIM), lambda i: (0, i, 0)),
            pl.BlockSpec((tq, IDX_HEADS), lambda i: (i, 0)),
            pl.BlockSpec((tq, width), lambda i: (i, z_block)),
            pl.BlockSpec((s, kvw), lambda i: (0, k_block), pipeline_mode=held),
            pl.BlockSpec((s, kvw), lambda i: (0, k_block + 1), pipeline_mode=held),
            pl.BlockSpec((s, IDX_DIM), lambda i: (0, 0), pipeline_mode=held),
        ],
        out_specs=pl.BlockSpec((tq, width), lambda i: (i, 0)),
        scratch_shapes=[
            pltpu.VMEM((s // V7X_LANES, tq, V7X_LANES), F32),
            pltpu.VMEM((IDX_HEADS, tq, V7X_LANES), F32),
            pltpu.VMEM((tq, V7X_LANES), I32),
            pltpu.VMEM((tq, V7X_LANES), F32),
            pltpu.VMEM((tq, V7X_LANES), F32),
            pltpu.VMEM((N_HEADS, tq, V7X_LANES), F32),
            pltpu.VMEM((N_HEADS, tq, V7X_LANES), F32),
            pltpu.VMEM((N_HEADS, tq, HEAD_DIM), F32),
        ],
        compiler_params=_cparams("arbitrary"),
        name="dsa",
    )(q, qi, w_idx, proj, proj, proj, kidx)


def _tail_kernel(ys_ref, ya_ref, gs_ref, ga_ref, x_ref, p_ref, wso_ref, wao_ref, wo_ref, wp_ref, wg_ref,
                 gple_ref, gpost_ref, gfin_ref, o_ref):
    o_s = jnp.dot(ys_ref[...], wso_ref[...], preferred_element_type=F32)
    o_a = jnp.dot(ya_ref[...], wao_ref[...], preferred_element_type=F32)
    merged = (gs_ref[...].astype(F32) * o_s + ga_ref[...].astype(F32) * o_a).astype(BF16)
    x1 = x_ref[...] + jnp.dot(merged, wo_ref[...], preferred_element_type=F32)
    e = _rms(jnp.dot(p_ref[...].astype(BF16), wp_ref[...], preferred_element_type=F32), gpost_ref[...])
    xn = _rms(x1, gple_ref[...]).astype(BF16)
    gate = _sigmoid(jnp.dot(xn, wg_ref[...], preferred_element_type=F32))
    o_ref[...] = _rms(x1 + gate * e, gfin_ref[...])


def _tail(y_ssm, y_attn, proj, gate_col, x, p, w_so, w_ao, w_o, w_ple, w_gate, g_ple, g_post, g_final,
          *, tm=256):
    s, d = x.shape
    kdim = y_ssm.shape[1]
    pd = p.shape[1]
    g0 = gate_col // d
    rows = lambda width, blk=0: pl.BlockSpec((tm, width), lambda i: (i, blk))
    held = lambda shape: pl.BlockSpec(shape, lambda i: (0, 0), pipeline_mode=pl.Buffered(1))
    vec = pl.BlockSpec((1, d), lambda i: (0, 0))
    return pl.pallas_call(
        _tail_kernel,
        out_shape=jax.ShapeDtypeStruct((s, d), F32),
        grid=(s // tm,),
        in_specs=[rows(kdim), rows(kdim), rows(d, g0), rows(d, g0 + 1), rows(d), rows(pd),
                  held((kdim, d)), held((kdim, d)), held((d, d)), held((pd, d)), held((d, d)),
                  vec, vec, vec],
        out_specs=rows(d),
        compiler_params=_cparams("parallel"),
        name="tail",
    )(y_ssm, y_attn, proj, proj, x, p, w_so, w_ao, w_o, w_ple, w_gate, g_ple, g_post, g_final)


def kernel(x, p, g_mix, w_in, g_q, w_uq, w_uq_idx, g_kidx, a_re, a_im, log_dt, b_re, b_im, c_re, c_im,
           d_skip, w_glu, w_ssm_out, w_attn_out, w_o, g_ple, w_ple_gate, w_ple, g_ple_post, g_final):
    bsz, s, d = x.shape
    assert bsz == 1 and p.shape[0] == 1, "kernel is specialised to one sequence and one layer"
    x = x.reshape(s, d)
    ssm_w = d // 2
    attn_w = N_HEADS * HEAD_DIM
    kvw = N_KV_HEADS * HEAD_DIM
    n_a = 2 * ssm_w + Q_LORA_RANK + 2 * kvw + attn_w
    n_i = IDX_DIM + IDX_HEADS
    row = lambda v: v.reshape(1, -1)

    proj, c_q, kidx, w_idx = _proj(x, row(g_mix), jnp.swapaxes(w_in[:1], 1, 2), n_a, 2 * ssm_w,
                                   n_a + n_i, row(g_kidx))

    n_chunks = s // S5_CHUNK
    cp, bbt, e, a1, a2 = _s5_operators(a_re[0], a_im[0], log_dt[0], b_re[0], b_im[0], c_re[0], c_im[0],
                                       levels=n_chunks.bit_length() - 1)
    y = _s5(proj, ssm_w, cp, bbt, e, a1, a2)
    y_ssm = _s5_post(y, proj, row(d_skip), w_glu[0].astype(BF16))

    w_q = jnp.concatenate([w_uq[0], w_uq_idx[0]], axis=1).astype(BF16)
    q, qi = _qproj(c_q, 0, row(g_q), w_q)
    k_off = 2 * ssm_w + Q_LORA_RANK
    z_block = (k_off + 2 * kvw) // attn_w
    y_attn = _dsa(q, qi, w_idx, proj, z_block, k_off // kvw, kidx)

    out = _tail(y_ssm, y_attn, proj, n_a, x, p[0, 0], w_ssm_out[0].astype(BF16), w_attn_out[0].astype(BF16),
                w_o[0].astype(BF16), w_ple[0].astype(BF16), w_ple_gate[0].astype(BF16),
                row(g_ple), row(g_ple_post), row(g_final))
    return out.reshape(bsz, s, d)
```

```python
import functools
import math

import numpy as np
import jax
import jax.numpy as jnp
from jax import lax
from jax.experimental import pallas as pl
from jax.experimental.pallas import tpu as pltpu

F32 = jnp.float32
BF16 = jnp.bfloat16
I32 = jnp.int32

EPS = 1e-6
SSM_GROUP = 16
N_HEADS = 8
N_KV_HEADS = 2
HEAD_DIM = 128
Q_LORA_RANK = 512
IDX_HEADS = 16
IDX_DIM = 64
TOPK_MAX = 256

V7X_LANES = 128
V7X_VMEM_LIMIT = 56 * 1024 * 1024

S5_CHUNK = 16
DSA_TQ = 512
DSA_TK = 512
SEARCH_ROWS = 128
SEARCH_GROUP = 4
ATTN_HEADS = 1
SCORE_HEADS = 2

INT_MIN = -(2 ** 31)
KEY_NEG_INF = -2139095041
NEG_INIT = -1e30

_NT = (((1,), (1,)), ((), ()))


def _cparams(*sem):
    return pltpu.CompilerParams(dimension_semantics=sem, vmem_limit_bytes=V7X_VMEM_LIMIT)


def _rms(x, g):
    ms = jnp.mean(x * x, axis=-1, keepdims=True)
    return x * lax.rsqrt(ms + EPS) * g


def _sigmoid(x):
    return 1.0 / (1.0 + jnp.exp(-x))


def _gelu_tanh(x):
    c = math.sqrt(2.0 / math.pi)
    return 0.5 * x * (1.0 + jnp.tanh(c * (x + 0.044715 * (x * x * x))))


def _proj_kernel(x_ref, g_ref, wa_ref, wg_ref, wi_ref, gk_ref, ob_ref, cq_ref, ki_ref, widx_ref, h_ref,
                 *, n_a_tiles, cq_tile, n_cq_tiles, w_scale):
    j = pl.program_id(1)

    @pl.when(j == 0)
    def _():
        h = _rms(x_ref[...], g_ref[...]).astype(BF16)
        h_ref[...] = h
        r = lax.dot_general(h, wi_ref[...].astype(BF16), _NT, preferred_element_type=F32)
        ki_ref[...] = _rms(r[:, :IDX_DIM], gk_ref[...]).astype(BF16)
        widx_ref[...] = r[:, IDX_DIM:IDX_DIM + IDX_HEADS] * w_scale

    @pl.when(j < n_a_tiles)
    def _():
        acc = lax.dot_general(h_ref[...], wa_ref[...].astype(BF16), _NT, preferred_element_type=F32)
        ob_ref[...] = acc.astype(BF16)

        @pl.when(jnp.logical_and(j >= cq_tile, j < cq_tile + n_cq_tiles))
        def _():
            cq_ref[...] = acc

    @pl.when(j >= n_a_tiles)
    def _():
        acc = lax.dot_general(h_ref[...], wg_ref[0].astype(BF16), _NT, preferred_element_type=F32)
        ob_ref[...] = _sigmoid(acc).astype(BF16)


def _proj(x, g, w_t, n_a, cq_col, gate_row, gk, *, tm=2048, tn=256):
    s, d = x.shape
    tm = min(tm, s)
    n_g = w_t.shape[1] - gate_row
    n_a_tiles = n_a // tn
    cq_tile, n_cq_tiles = cq_col // tn, Q_LORA_RANK // tn
    return pl.pallas_call(
        functools.partial(_proj_kernel, n_a_tiles=n_a_tiles, cq_tile=cq_tile, n_cq_tiles=n_cq_tiles,
                          w_scale=(IDX_HEADS ** -0.5) * (IDX_DIM ** -0.5)),
        out_shape=(jax.ShapeDtypeStruct((s, n_a + n_g), BF16),
                   jax.ShapeDtypeStruct((s, Q_LORA_RANK), F32),
                   jax.ShapeDtypeStruct((s, IDX_DIM), BF16),
                   jax.ShapeDtypeStruct((s, IDX_HEADS), F32)),
        grid=(s // tm, (n_a + n_g) // tn),
        in_specs=[
            pl.BlockSpec((tm, d), lambda i, j: (i, 0), pipeline_mode=pl.Buffered(1)),
            pl.BlockSpec((1, d), lambda i, j: (0, 0)),
            pl.BlockSpec((None, tn, d), lambda i, j: (0, jnp.minimum(j, n_a_tiles - 1), 0)),
            pl.BlockSpec((pl.Element(1), pl.Element(tn), pl.Element(d)),
                         lambda i, j: (0, pl.multiple_of(gate_row + jnp.maximum(j - n_a_tiles, 0) * tn, 8), 0)),
            pl.BlockSpec((None, V7X_LANES, d), lambda i, j: (0, n_a // V7X_LANES, 0)),
            pl.BlockSpec((1, IDX_DIM), lambda i, j: (0, 0)),
        ],
        out_specs=(pl.BlockSpec((tm, tn), lambda i, j: (i, j)),
                   pl.BlockSpec((tm, tn), lambda i, j: (i, jnp.clip(j - cq_tile, 0, n_cq_tiles - 1))),
                   pl.BlockSpec((tm, IDX_DIM), lambda i, j: (i, 0)),
                   pl.BlockSpec((tm, IDX_HEADS), lambda i, j: (i, 0))),
        scratch_shapes=[pltpu.VMEM((tm, d), BF16)],
        compiler_params=_cparams("parallel", "arbitrary"),
        name="proj",
    )(x, g, w_t, w_t, w_t, gk)


def _qproj_kernel(c_ref, g_ref, w_ref, q_ref, qi_ref, *, q_scale):
    cq = _rms(c_ref[...], g_ref[...]).astype(BF16)
    r = jnp.dot(cq, w_ref[...], preferred_element_type=F32)
    for h in range(N_HEADS):
        q_ref[h] = (r[:, h * HEAD_DIM:(h + 1) * HEAD_DIM] * q_scale).astype(BF16)
    base = N_HEADS * HEAD_DIM
    for h in range(IDX_HEADS):
        qi_ref[h] = r[:, base + h * IDX_DIM: base + (h + 1) * IDX_DIM].astype(BF16)


def _qproj(proj_a, cq_block, g, w, *, tm=512):
    s = proj_a.shape[0]
    return pl.pallas_call(
        functools.partial(_qproj_kernel, q_scale=HEAD_DIM ** -0.5 * math.log2(math.e)),
        out_shape=(jax.ShapeDtypeStruct((N_HEADS, s, HEAD_DIM), BF16),
                   jax.ShapeDtypeStruct((IDX_HEADS, s, IDX_DIM), BF16)),
        grid=(s // tm,),
        in_specs=[
            pl.BlockSpec((tm, Q_LORA_RANK), lambda i: (i, cq_block)),
            pl.BlockSpec((1, Q_LORA_RANK), lambda i: (0, 0)),
            pl.BlockSpec(w.shape, lambda i: (0, 0)),
        ],
        out_specs=(pl.BlockSpec((N_HEADS, tm, HEAD_DIM), lambda i: (0, i, 0)),
                   pl.BlockSpec((IDX_HEADS, tm, IDX_DIM), lambda i: (0, i, 0))),
        compiler_params=_cparams("parallel"),
        name="qproj",
    )(proj_a, g, w)


def _cmul(ar, ai, br, bi):
    return ar * br - ai * bi, ar * bi + ai * br


def _s5_operators(a_re, a_im, log_dt, b_re, b_im, c_re, c_im, levels):
    t_len = S5_CHUNK
    g_n, n_st = a_re.shape
    dt = jnp.exp(log_dt)[:, None]
    mag = jnp.exp(dt * a_re)
    abr = mag * jnp.cos(dt * a_im)
    abi = mag * jnp.sin(dt * a_im)
    den = a_re * a_re + a_im * a_im
    nr = abr - 1.0
    f_re = (nr * a_re + abi * a_im) / den
    f_im = (abi * a_re - nr * a_im) / den
    bt_re = b_re.transpose(0, 2, 1)
    bt_im = b_im.transpose(0, 2, 1)
    bbt_re = f_re[:, None, :] * bt_re - f_im[:, None, :] * bt_im
    bbt_im = f_re[:, None, :] * bt_im + f_im[:, None, :] * bt_re

    nbits = t_len.bit_length()
    sq = [(abr, abi)]
    for _ in range(nbits - 1):
        sq.append(_cmul(*sq[-1], *sq[-1]))
    j = np.arange(t_len + 1)
    pr = jnp.ones((g_n, t_len + 1, n_st), F32)
    pi = jnp.zeros((g_n, t_len + 1, n_st), F32)
    for b in range(nbits):
        bit = jnp.asarray(((j >> b) & 1).astype(bool))[None, :, None]
        mr, mi = _cmul(pr, pi, sq[b][0][:, None, :], sq[b][1][:, None, :])
        pr = jnp.where(bit, mr, pr)
        pi = jnp.where(bit, mi, pi)

    prj, pij = pr[:, :, None, :], pi[:, :, None, :]
    cp_re = c_re[:, None] * prj - c_im[:, None] * pij
    cp_im = c_re[:, None] * pij + c_im[:, None] * prj
    cp = jnp.concatenate([cp_re, -cp_im], axis=-1).reshape(g_n, (t_len + 1) * SSM_GROUP, 2 * n_st)

    prr, pir = prj[:, t_len - 1::-1], pij[:, t_len - 1::-1]
    e_re = prr * bbt_re[:, None] - pir * bbt_im[:, None]
    e_im = prr * bbt_im[:, None] + pir * bbt_re[:, None]
    e = jnp.concatenate([e_re, e_im], axis=-1).reshape(g_n, t_len * SSM_GROUP, 2 * n_st)

    lv_r, lv_i = [pr[:, t_len]], [pi[:, t_len]]
    for _ in range(levels - 1):
        r2, i2 = _cmul(lv_r[-1], lv_i[-1], lv_r[-1], lv_i[-1])
        lv_r.append(r2)
        lv_i.append(i2)
    ar = jnp.stack(lv_r, axis=1)
    ai = jnp.stack(lv_i, axis=1)
    a1 = jnp.concatenate([ar, ar], axis=-1)
    a2 = jnp.concatenate([-ai, ai], axis=-1)
    return cp, jnp.concatenate([bbt_re, bbt_im], axis=-1), e.astype(BF16), a1, a2


def _s5_kernel(*refs, t_len):
    gpt = V7X_LANES // SSM_GROUP
    (u_ref, cp_ref, bbt_ref, e_ref, a1_ref, a2_ref, y_ref,
     rep_ref, master_ref, ebd_ref, fbd_ref, tm_ref) = refs
    n_chunks = u_ref.shape[0] // t_len
    ct = t_len * SSM_GROUP
    n2 = cp_ref.shape[2]
    pair = 2 * V7X_LANES

    @pl.when(pl.program_id(0) == 0)
    def _():
        ebd_ref[...] = jnp.zeros(ebd_ref.shape, BF16)
        fbd_ref[...] = jnp.zeros(fbd_ref.shape, BF16)
        r = lax.broadcasted_iota(I32, rep_ref.shape, 0)
        q = lax.broadcasted_iota(I32, rep_ref.shape, 1)
        same_lag = (r // SSM_GROUP) == (q // V7X_LANES)
        same_ch = (r % SSM_GROUP) == (q % SSM_GROUP)
        rep_ref[...] = jnp.where(same_lag, jnp.where(same_ch, 1.0, 0.0), 0.0).astype(BF16)

    kt8 = jnp.concatenate(
        [lax.dot_general(bbt_ref[k], cp_ref[k, :ct, :], _NT, precision=lax.Precision.HIGHEST,
                         preferred_element_type=F32) for k in range(gpt)], axis=0)
    dall = jnp.dot(kt8.astype(BF16), rep_ref[...], preferred_element_type=F32)
    rg = lax.broadcasted_iota(I32, dall.shape, 0) // SSM_GROUP
    lg = (lax.broadcasted_iota(I32, dall.shape, 1) % V7X_LANES) // SSM_GROUP
    dall = jnp.where(rg == lg, dall, 0.0).astype(BF16)
    for b in range(t_len - 1):
        r0 = (t_len - 2 - b) * V7X_LANES
        master_ref[b * V7X_LANES:(b + 1) * V7X_LANES, :] = dall[:, r0:r0 + pair]
    master_ref[(t_len - 1) * V7X_LANES:, :] = jnp.concatenate(
        [jnp.zeros((V7X_LANES, V7X_LANES), BF16), dall[:, :V7X_LANES]], axis=1)
    for s in range(t_len):
        for k in range(gpt):
            rows = slice(s * V7X_LANES + k * SSM_GROUP, s * V7X_LANES + (k + 1) * SSM_GROUP)
            cols = slice(k * n2, (k + 1) * n2)
            ebd_ref[rows, cols] = e_ref[k, s * SSM_GROUP:(s + 1) * SSM_GROUP, :]
            fbd_ref[rows, cols] = cp_ref[k, (s + 1) * SSM_GROUP:(s + 2) * SSM_GROUP, :].astype(BF16)

    tm_ref[...] = u_ref[...].astype(F32)
    u = jnp.concatenate([tm_ref[pl.ds(s, n_chunks, stride=t_len), :].astype(BF16) for s in range(t_len)],
                        axis=1)
    hend = jnp.dot(u, ebd_ref[...], preferred_element_type=F32)
    row = lax.broadcasted_iota(I32, (hend.shape[0], n2), 0)
    carries = []
    for k in range(gpt):
        p = hend[:, k * n2:(k + 1) * n2]
        for lv in range(a1_ref.shape[1]):
            sh = 1 << lv
            a1 = a1_ref[k, lv:lv + 1, :]
            a2 = a2_ref[k, lv:lv + 1, :]
            ps = jnp.where(row >= sh, pltpu.roll(p, sh, axis=0), 0.0)
            p = p + a1 * ps + a2 * pltpu.roll(ps, n2 // 2, axis=1)
        carries.append(jnp.where(row >= 1, pltpu.roll(p, 1, axis=0), 0.0).astype(BF16))
    carry = jnp.concatenate(carries, axis=1)

    for tp in range(t_len // 2):
        kdim = (tp + 1) * pair
        y2 = jnp.dot(u[:, :kdim], master_ref[(t_len - 2 - 2 * tp) * V7X_LANES:, :],
                     preferred_element_type=F32)
        y2 = y2 + lax.dot_general(carry, fbd_ref[tp * pair:(tp + 1) * pair, :], _NT,
                                  preferred_element_type=F32)
        tm_ref[(2 * tp) * n_chunks:(2 * tp + 1) * n_chunks, :] = y2[:, :V7X_LANES]
        tm_ref[(2 * tp + 1) * n_chunks:(2 * tp + 2) * n_chunks, :] = y2[:, V7X_LANES:]

    for c in range(n_chunks):
        y_ref[c * t_len:(c + 1) * t_len, :] = tm_ref[pl.ds(c, t_len, stride=n_chunks), :]


def _s5(proj, ssm_w, cp, bbt, e, a1, a2):
    s, width = proj.shape
    t_len = S5_CHUNK
    n_chunks = s // t_len
    gpt = V7X_LANES // SSM_GROUP
    ct = t_len * SSM_GROUP
    n2 = e.shape[-1]
    levels = a1.shape[1]
    assert n_chunks == 1 << levels and n2 == V7X_LANES
    grp = lambda rows: pl.BlockSpec((gpt, rows, n2), lambda g: (g, 0, 0))
    col = pl.BlockSpec((s, V7X_LANES), lambda g: (0, g))
    return pl.pallas_call(
        functools.partial(_s5_kernel, t_len=t_len),
        out_shape=jax.ShapeDtypeStruct((s, ssm_w), F32),
        grid=(ssm_w // V7X_LANES,),
        in_specs=[col, grp(ct + SSM_GROUP), grp(SSM_GROUP), grp(ct), grp(levels), grp(levels)],
        out_specs=col,
        scratch_shapes=[
            pltpu.VMEM((ct, t_len * V7X_LANES), BF16),
            pltpu.VMEM((t_len * V7X_LANES, 2 * V7X_LANES), BF16),
            pltpu.VMEM((t_len * V7X_LANES, gpt * n2), BF16),
            pltpu.VMEM((t_len * V7X_LANES, gpt * n2), BF16),
            pltpu.VMEM((s, V7X_LANES), F32),
        ],
        compiler_params=_cparams("arbitrary"),
        name="s5",
    )(proj, cp, bbt, e, a1, a2)


def _s5_post_kernel(y_ref, u_ref, z_ref, d_ref, w_ref, o_ref):
    width = y_ref.shape[1]
    yy = _gelu_tanh(y_ref[...] + d_ref[...] * u_ref[...].astype(F32))
    r = jnp.dot(yy.astype(BF16), w_ref[...], preferred_element_type=F32)
    glu = r[:, :width] * _sigmoid(r[:, width:])
    z = z_ref[...].astype(F32)
    o_ref[...] = (glu * (z * _sigmoid(z))).astype(BF16)


def _s5_post(y, proj_a, d_skip, w_glu, *, tm=512):
    s, width = y.shape
    return pl.pallas_call(
        _s5_post_kernel,
        out_shape=jax.ShapeDtypeStruct((s, width), BF16),
        grid=(s // tm,),
        in_specs=[
            pl.BlockSpec((tm, width), lambda i: (i, 0)),
            pl.BlockSpec((tm, width), lambda i: (i, 0)),
            pl.BlockSpec((tm, width), lambda i: (i, 1)),
            pl.BlockSpec((1, width), lambda i: (0, 0)),
            pl.BlockSpec(w_glu.shape, lambda i: (0, 0)),
        ],
        out_specs=pl.BlockSpec((tm, width), lambda i: (i, 0)),
        compiler_params=_cparams("parallel"),
        name="s5_post",
    )(y, proj_a, proj_a, d_skip, w_glu)


def _key_to_float(key):
    bits = key ^ (lax.shift_right_arithmetic(key, 31) & 0x7FFFFFFF)
    return lax.bitcast_convert_type(bits, F32)


def _dsa_kernel(q_ref, qi_ref, w_ref, z_ref, k_ref, v_ref, ki_ref, o_ref,
                sc_ref, wb_ref, res_ref, cand_ref, cnt_ref, m_ref, l_ref, acc_ref, *, topk):
    tq = DSA_TQ
    halves = tq // V7X_LANES
    cpt = DSA_TK // V7X_LANES
    rep = N_HEADS // N_KV_HEADS
    i = pl.program_id(0)
    n_valid = halves * (i + 1)
    n_it = (n_valid + cpt - 1) // cpt

    for h in range(IDX_HEADS):
        wb_ref[h] = jnp.broadcast_to(w_ref[:, h:h + 1], (tq, V7X_LANES))
    def score_tile(kt, diag):
        ki = ki_ref[pl.ds(pl.multiple_of(kt * tq, tq), tq), :]
        r = [lax.dot_general(qi_ref[g * SCORE_HEADS:(g + 1) * SCORE_HEADS].reshape(SCORE_HEADS * tq, IDX_DIM),
                             ki, _NT, preferred_element_type=F32)
             for g in range(IDX_HEADS // SCORE_HEADS)]
        for half in range(halves):
            lanes = slice(half * V7X_LANES, (half + 1) * V7X_LANES)
            acc = jnp.zeros((tq, V7X_LANES), F32)
            for h in range(IDX_HEADS):
                hg, hl = divmod(h, SCORE_HEADS)
                acc = acc + wb_ref[h] * jnp.maximum(r[hg][hl * tq:(hl + 1) * tq, lanes], 0.0)
            if diag:
                col = lax.broadcasted_iota(I32, acc.shape, 1) + half * V7X_LANES
                rowi = lax.broadcasted_iota(I32, acc.shape, 0)
                acc = jnp.where(col <= rowi, acc, -jnp.inf)
            sc_ref[halves * kt + half] = acc

    def score_body(kt, c):
        score_tile(kt, False)
        return c

    lax.fori_loop(0, i, score_body, 0)
    score_tile(i, True)

    for j in range(cpt - halves):
        @pl.when(n_valid + j < n_it * cpt)
        def _():
            sc_ref[n_valid + j] = jnp.full((tq, V7X_LANES), -jnp.inf, F32)

    res_ref[...] = jnp.full(res_ref.shape, INT_MIN, I32)

    def pass_body(p, c):
        bit = lax.shift_left(jnp.int32(1), 31 - p)
        cnts = []
        for rc in range(tq // SEARCH_ROWS):
            rows = slice(rc * SEARCH_ROWS, (rc + 1) * SEARCH_ROWS)
            cand = _key_to_float(res_ref[rows, :] + bit)

            def it_body(it, cnt, rows=rows, cand=cand):
                for j in range(cpt):
                    cnt = cnt + jnp.where(sc_ref[cpt * it + j, rows, :] >= cand, 1.0, 0.0)
                return cnt

            cnts.append(lax.fori_loop(0, n_it, it_body, jnp.zeros((SEARCH_ROWS, V7X_LANES), F32)))
        tot = jnp.sum(jnp.concatenate(cnts, axis=0), axis=1, keepdims=True)
        ok = tot >= float(topk)
        res = res_ref[...]
        res_ref[...] = jnp.where(ok, res + bit, res)
        cnt_ref[...] = jnp.where(ok, tot, cnt_ref[...])
        return c

    cnt_ref[...] = (jnp.full(cnt_ref.shape, DSA_TK, I32) * n_it).astype(F32)

    def group_body(state):
        g, _ = state
        lax.fori_loop(g * SEARCH_GROUP, (g + 1) * SEARCH_GROUP, pass_body, 0)
        settled = jnp.min(jnp.where(cnt_ref[...] == float(topk), 1.0, 0.0)) > 0.0
        return g + 1, settled.astype(I32)

    def group_cond(state):
        g, settled = state
        return jnp.logical_and(g < 32 // SEARCH_GROUP, settled == 0)

    lax.while_loop(group_cond, group_body, (jnp.int32(0), jnp.int32(0)))
    res = res_ref[...]
    tau = _key_to_float(jnp.maximum(res, KEY_NEG_INF + 1))
    cand_ref[...] = tau

    excess = jnp.where(res > KEY_NEG_INF, cnt_ref[...] - float(topk), 0.0).astype(I32)
    lane = lax.broadcasted_iota(I32, (tq, V7X_LANES), 1)
    n_chunks = n_it * cpt

    def drop_body(r, c):
        active = excess > r

        def min_body(ch, mv):
            sc = sc_ref[ch]
            return jnp.minimum(mv, jnp.where(sc >= tau, sc, jnp.inf))

        mv = lax.fori_loop(0, n_chunks, min_body, jnp.full((tq, V7X_LANES), jnp.inf, F32))
        mval = jnp.min(mv, axis=1, keepdims=True)

        def idx_body(ch, ix):
            col = (lane + ch * V7X_LANES).astype(F32)
            return jnp.maximum(ix, jnp.where(sc_ref[ch] == mval, col, -1.0))

        ix = lax.fori_loop(0, n_chunks, idx_body, jnp.full((tq, V7X_LANES), -1.0, F32))
        idx = jnp.max(ix, axis=1, keepdims=True)

        def drop_chunk(ch, c2):
            col = (lane + ch * V7X_LANES).astype(F32)
            sc = sc_ref[ch]
            sc_ref[ch] = jnp.where(active, jnp.where(col == idx, -jnp.inf, sc), sc)
            return c2

        lax.fori_loop(0, n_chunks, drop_chunk, 0)
        return c

    lax.fori_loop(0, jnp.max(excess), drop_body, 0)

    m_ref[...] = jnp.full(m_ref.shape, NEG_INIT, F32)
    l_ref[...] = jnp.zeros(l_ref.shape, F32)
    acc_ref[...] = jnp.zeros(acc_ref.shape, F32)
    ones = jnp.ones((DSA_TK, HEAD_DIM), BF16)

    def attn_body(it, c):
        tau = cand_ref[...]
        neg = jnp.concatenate(
            [jnp.where(sc_ref[cpt * it + j] >= tau, 0.0, -jnp.inf) for j in range(cpt)], axis=1)
        ks = pl.ds(pl.multiple_of(it * DSA_TK, DSA_TK), DSA_TK)
        for part in range(N_HEADS // ATTN_HEADS):
            g = part * ATTN_HEADS // rep
            hs = slice(part * ATTN_HEADS, (part + 1) * ATTN_HEADS)
            cols = slice(g * HEAD_DIM, (g + 1) * HEAD_DIM)
            qg = q_ref[hs].reshape(ATTN_HEADS * tq, HEAD_DIM)
            s = lax.dot_general(qg, k_ref[ks, cols], _NT, preferred_element_type=F32)
            s = s.reshape(ATTN_HEADS, tq, DSA_TK) + neg[None]
            m_prev = m_ref[hs]
            m_new = jnp.maximum(m_prev, jnp.max(s, axis=2, keepdims=True))
            alpha = jnp.exp2(m_prev - m_new)
            p = jnp.concatenate(
                [jnp.exp2(s[..., j * V7X_LANES:(j + 1) * V7X_LANES] - m_new) for j in range(cpt)], axis=2)
            v_aug = jnp.concatenate([v_ref[ks, cols], ones], axis=1)
            pv = jnp.dot(p.reshape(ATTN_HEADS * tq, DSA_TK).astype(BF16), v_aug,
                         preferred_element_type=F32).reshape(ATTN_HEADS, tq, 2 * HEAD_DIM)
            acc_ref[hs] = alpha * acc_ref[hs] + pv[..., :HEAD_DIM]
            l_ref[hs] = alpha * l_ref[hs] + pv[..., HEAD_DIM:]
            m_ref[hs] = m_new
        return c

    lax.fori_loop(0, n_it, attn_body, 0)

    for h in range(N_HEADS):
        cols = slice(h * HEAD_DIM, (h + 1) * HEAD_DIM)
        z = z_ref[:, cols].astype(F32)
        o_ref[:, cols] = ((acc_ref[h] / l_ref[h]) * (z * _sigmoid(z))).astype(BF16)


def _dsa(q, qi, w_idx, proj, z_block, k_block, kidx):
    s = kidx.shape[0]
    tq = DSA_TQ
    width = N_HEADS * HEAD_DIM
    kvw = N_KV_HEADS * HEAD_DIM
    topk = min(TOPK_MAX, s // 4)
    held = pl.Buffered(1)
    return pl.pallas_call(
        functools.partial(_dsa_kernel, topk=topk),
        out_shape=jax.ShapeDtypeStruct((s, width), BF16),
        grid=(s // tq,),
        in_specs=[
            pl.BlockSpec((N_HEADS, tq, HEAD_DIM), lambda i: (0, i, 0)),
            pl.BlockSpec((IDX_HEADS, tq, IDX_DIM), lambda i: (0, i, 0)),
            pl.BlockSpec((tq, IDX_HEADS), lambda i: (i, 0)),
            pl.BlockSpec((tq, width), lambda i: (i, z_block)),
            pl.BlockSpec((s, kvw), lambda i: (0, k_block), pipeline_mode=held),
            pl.BlockSpec((s, kvw), lambda i: (0, k_block + 1), pipeline_mode=held),
            pl.BlockSpec((s, IDX_DIM), lambda i: (0, 0), pipeline_mode=held),
        ],
        out_specs=pl.BlockSpec((tq, width), lambda i: (i, 0)),
        scratch_shapes=[
            pltpu.VMEM((s // V7X_LANES, tq, V7X_LANES), F32),
            pltpu.VMEM((IDX_HEADS, tq, V7X_LANES), F32),
            pltpu.VMEM((tq, V7X_LANES), I32),
            pltpu.VMEM((tq, V7X_LANES), F32),
            pltpu.VMEM((tq, V7X_LANES), F32),
            pltpu.VMEM((N_HEADS, tq, V7X_LANES), F32),
            pltpu.VMEM((N_HEADS, tq, V7X_LANES), F32),
            pltpu.VMEM((N_HEADS, tq, HEAD_DIM), F32),
        ],
        compiler_params=_cparams("arbitrary"),
        name="dsa",
    )(q, qi, w_idx, proj, proj, proj, kidx)


def _tail_kernel(ys_ref, ya_ref, gs_ref, ga_ref, x_ref, p_ref, wso_ref, wao_ref, wo_ref, wp_ref, wg_ref,
                 gple_ref, gpost_ref, gfin_ref, o_ref):
    o_s = jnp.dot(ys_ref[...], wso_ref[...], preferred_element_type=F32)
    o_a = jnp.dot(ya_ref[...], wao_ref[...], preferred_element_type=F32)
    merged = (gs_ref[...].astype(F32) * o_s + ga_ref[...].astype(F32) * o_a).astype(BF16)
    x1 = x_ref[...] + jnp.dot(merged, wo_ref[...], preferred_element_type=F32)
    e = _rms(jnp.dot(p_ref[...].astype(BF16), wp_ref[...], preferred_element_type=F32), gpost_ref[...])
    xn = _rms(x1, gple_ref[...]).astype(BF16)
    gate = _sigmoid(jnp.dot(xn, wg_ref[...], preferred_element_type=F32))
    o_ref[...] = _rms(x1 + gate * e, gfin_ref[...])


def _tail(y_ssm, y_attn, proj, gate_col, x, p, w_so, w_ao, w_o, w_ple, w_gate, g_ple, g_post, g_final,
          *, tm=256):
    s, d = x.shape
    kdim = y_ssm.shape[1]
    pd = p.shape[1]
    g0 = gate_col // d
    rows = lambda width, blk=0: pl.BlockSpec((tm, width), lambda i: (i, blk))
    held = lambda shape: pl.BlockSpec(shape, lambda i: (0, 0), pipeline_mode=pl.Buffered(1))
    vec = pl.BlockSpec((1, d), lambda i: (0, 0))
    return pl.pallas_call(
        _tail_kernel,
        out_shape=jax.ShapeDtypeStruct((s, d), F32),
        grid=(s // tm,),
        in_specs=[rows(kdim), rows(kdim), rows(d, g0), rows(d, g0 + 1), rows(d), rows(pd),
                  held((kdim, d)), held((kdim, d)), held((d, d)), held((pd, d)), held((d, d)),
                  vec, vec, vec],
        out_specs=rows(d),
        compiler_params=_cparams("parallel"),
        name="tail",
    )(y_ssm, y_attn, proj, proj, x, p, w_so, w_ao, w_o, w_ple, w_gate, g_ple, g_post, g_final)


def kernel(x, p, g_mix, w_in, g_q, w_uq, w_uq_idx, g_kidx, a_re, a_im, log_dt, b_re, b_im, c_re, c_im,
           d_skip, w_glu, w_ssm_out, w_attn_out, w_o, g_ple, w_ple_gate, w_ple, g_ple_post, g_final):
    bsz, s, d = x.shape
    assert bsz == 1 and p.shape[0] == 1, "kernel is specialised to one sequence and one layer"
    x = x.reshape(s, d)
    ssm_w = d // 2
    attn_w = N_HEADS * HEAD_DIM
    kvw = N_KV_HEADS * HEAD_DIM
    n_a = 2 * ssm_w + Q_LORA_RANK + 2 * kvw + attn_w
    n_i = IDX_DIM + IDX_HEADS
    row = lambda v: v.reshape(1, -1)

    proj, c_q, kidx, w_idx = _proj(x, row(g_mix), jnp.swapaxes(w_in[:1], 1, 2), n_a, 2 * ssm_w,
                                   n_a + n_i, row(g_kidx))

    n_chunks = s // S5_CHUNK
    cp, bbt, e, a1, a2 = _s5_operators(a_re[0], a_im[0], log_dt[0], b_re[0], b_im[0], c_re[0], c_im[0],
                                       levels=n_chunks.bit_length() - 1)
    y = _s5(proj, ssm_w, cp, bbt, e, a1, a2)
    y_ssm = _s5_post(y, proj, row(d_skip), w_glu[0].astype(BF16))

    w_q = jnp.concatenate([w_uq[0], w_uq_idx[0]], axis=1).astype(BF16)
    q, qi = _qproj(c_q, 0, row(g_q), w_q)
    k_off = 2 * ssm_w + Q_LORA_RANK
    z_block = (k_off + 2 * kvw) // attn_w
    y_attn = _dsa(q, qi, w_idx, proj, z_block, k_off // kvw, kidx)

    out = _tail(y_ssm, y_attn, proj, n_a, x, p[0, 0], w_ssm_out[0].astype(BF16), w_attn_out[0].astype(BF16),
                w_o[0].astype(BF16), w_ple[0].astype(BF16), w_ple_gate[0].astype(BF16),
                row(g_ple), row(g_ple_post), row(g_final))
    return out.reshape(bsz, s, d)
```
